```python
import math
import jax
import jax.numpy as jnp
from jax import lax
import numpy as np

D_MODEL = 1024
BATCH = 4
SEQ = 4096
DEPTH = 4
DEC_BATCH = 32
DEC_SEQ = 1
PAST_LEN = 8192
PAGE_SIZE = 128

H_A = 4
DK_A = 64
DV_A = 2 * DK_A
W_A = H_A * DV_A
H_B = 4
P_B = 64
G_B = 2
N_B = 64
CONV_W = 4
W_B = H_B * P_B
XBC_B = W_B + 2 * G_B * N_B
CHUNK_B = 128
H_C = 4
DK_C = 32
DV_C = 64
W_C = H_C * DV_C
GATE_RANK = 16
GATE_NORM = 16.0
CHUNK_C = 64
MIX_W = W_A + W_B + W_C
T5_BUCKETS = 32
T5_MAX_DIST = 128
D_FF = 2816
N_EXPERTS = 8
TOP_K = 2
MOE_BLOCK = 128
N_DENSE = (DEPTH + 1) // 2
N_MOE = DEPTH // 2
Q_BLOCK = 128
EPS = 1e-6
POOL_FACTOR = 1.25
OFF_AQ = 0
OFF_AK = OFF_AQ + 2 * H_A * DK_A
OFF_AV = OFF_AK + 2 * H_A * DK_A
OFF_BZ = OFF_AV + W_A
OFF_BX = OFF_BZ + W_B
OFF_BDT = OFF_BX + XBC_B
OFF_CQ = OFF_BDT + H_B
OFF_CK = OFF_CQ + H_C * DK_C
OFF_CV = OFF_CK + H_C * DK_C
OFF_CG = OFF_CV + W_C
OFF_CLR = OFF_CG + W_C
D_IN = OFF_CLR + GATE_RANK

kernel_name = 'hybrid_diffattn_ssd_gla_moe_decode_step'


def rmsnorm(x):
    xf = x.astype(jnp.float32)
    return (xf * lax.rsqrt(jnp.mean(xf * xf, axis=-1, keepdims=True) + EPS)).astype(x.dtype)


def t5_bucket(dist):
    max_exact = T5_BUCKETS // 2
    n = jnp.maximum(dist, 0)
    nf = jnp.maximum(n, 1).astype(jnp.float32)
    large = max_exact + (jnp.log(nf / max_exact) / math.log(T5_MAX_DIST / max_exact)
                         * (T5_BUCKETS - max_exact)).astype(jnp.int32)
    return jnp.where(n < max_exact, n, jnp.minimum(large, T5_BUCKETS - 1))


def diff_attn_core(q, k, v, q_pos, k_pos, rel_bias, lam):
    s = jnp.einsum('bqhmd,bkhmd->bhmqk', q, k).astype(jnp.float32) * (DK_A ** -0.5)
    dist = q_pos[:, None] - k_pos[None, :]
    bias = jnp.transpose(rel_bias[t5_bucket(dist)], (2, 0, 1)).astype(jnp.float32)
    s = jnp.where(dist >= 0, s + bias[None, :, None], -jnp.inf)
    p = jax.nn.softmax(s, axis=-1)
    w = p[:, :, 0] - lam * p[:, :, 1]
    return jnp.einsum('bhqk,bkhv->bqhv', w.astype(v.dtype), v)


def diff_attn_prompt(q, k, v, rel_bias, lam):
    bsz, seq = q.shape[:2]
    nq = seq // Q_BLOCK
    qb = jnp.moveaxis(q.reshape(bsz, nq, Q_BLOCK, H_A, 2, DK_A), 1, 0)
    k_pos = jnp.arange(seq)

    def block(args):
        q_blk, i = args
        return diff_attn_core(q_blk, k, v, i * Q_BLOCK + jnp.arange(Q_BLOCK), k_pos, rel_bias, lam)

    o = lax.map(block, (qb, jnp.arange(nq)))
    return jnp.moveaxis(o, 0, 1).reshape(bsz, seq, H_A, DV_A)


def causal_conv(u, buf, w, b):
    seq = u.shape[1]
    up = jnp.concatenate([buf.astype(u.dtype), u], axis=1)
    out = b + up[:, 0:seq] * w[0]
    for j in range(1, CONV_W):
        out = out + up[:, j:j + seq] * w[j]
    return jax.nn.silu(out), up[:, seq:]


def ssd_chunked(x, dt, a, bm, cm):
    bsz, seq = x.shape[:2]
    nc = seq // CHUNK_B

    def chunks(t):
        return jnp.moveaxis(t.reshape(bsz, nc, CHUNK_B, *t.shape[2:]), 1, 0)

    mask = jnp.tril(jnp.ones((CHUNK_B, CHUNK_B), dtype=bool))[None, :, :, None]

    def step(state, inp):
        xc, dtc, bc, cc = inp
        cum = jnp.cumsum(dtc * a, axis=1)
        decay = jnp.exp(jnp.where(mask, cum[:, :, None] - cum[:, None], -jnp.inf))
        scores = jnp.einsum('bthn,bshn->btsh', cc, bc) * decay
        y = jnp.einsum('btsh,bsh,bshp->bthp', scores, dtc, xc)
        y = y + jnp.einsum('bthn,bhpn->bthp', cc, state) * jnp.exp(cum)[..., None]
        wgt = jnp.exp(cum[:, -1:] - cum) * dtc
        state = (state * jnp.exp(cum[:, -1])[:, :, None, None]
                 + jnp.einsum('bsh,bshp,bshn->bhpn', wgt, xc, bc))
        return state, y

    s0 = jnp.zeros((bsz, H_B, P_B, N_B), jnp.float32)
    state, y = lax.scan(step, s0, (chunks(x), chunks(dt), chunks(bm), chunks(cm)))
    return jnp.moveaxis(y, 0, 1).reshape(bsz, seq, H_B, P_B), state


def ssd_recurrent(x, dt, a, bm, cm, state):
    def step(s, inp):
        xt, dtt, bt, ct = inp
        s = s * jnp.exp(dtt * a)[:, :, None, None] + jnp.einsum('bh,bhp,bhn->bhpn', dtt, xt, bt)
        return s, jnp.einsum('bhn,bhpn->bhp', ct, s)

    state, y = lax.scan(step, state, tuple(jnp.moveaxis(t, 1, 0) for t in (x, dt, bm, cm)))
    return jnp.moveaxis(y, 0, 1), state


def gla_chunked(q, k, v, g):
    bsz, seq = q.shape[:2]
    nc = seq // CHUNK_C

    def chunks(t):
        return jnp.moveaxis(t.reshape(bsz, nc, CHUNK_C, *t.shape[2:]), 1, 0)

    mask = jnp.tril(jnp.ones((CHUNK_C, CHUNK_C), dtype=bool))[None, :, :, None, None]

    def step(state, inp):
        qc, kc, vc, gc = inp
        b = jnp.cumsum(gc, axis=1)
        decay = jnp.exp(jnp.where(mask, b[:, :, None] - b[:, None], -jnp.inf))
        att = jnp.einsum('bthk,bshk,btshk->btsh', qc, kc, decay)
        o = (jnp.einsum('btsh,bshv->bthv', att, vc)
             + jnp.einsum('bthk,bhkv->bthv', qc * jnp.exp(b), state))
        b_last = b[:, -1]
        state = (state * jnp.exp(b_last)[..., None]
                 + jnp.einsum('bshk,bshv->bhkv', kc * jnp.exp(b_last[:, None] - b), vc))
        return state, o

    s0 = jnp.zeros((bsz, H_C, DK_C, DV_C), jnp.float32)
    state, o = lax.scan(step, s0, (chunks(q), chunks(k), chunks(v), chunks(g)))
    return jnp.moveaxis(o, 0, 1).reshape(bsz, seq, H_C, DV_C), state


def gla_recurrent(q, k, v, g, state):
    def step(s, inp):
        qt, kt, vt, gt = inp
        s = s * jnp.exp(gt)[..., None] + kt[..., None] * vt[:, :, None, :]
        return s, jnp.einsum('bhk,bhkv->bhv', qt, s)

    state, o = lax.scan(step, state, tuple(jnp.moveaxis(t, 1, 0) for t in (q, k, v, g)))
    return jnp.moveaxis(o, 0, 1), state


def swiglu(h, wg, wu, wd):
    return jnp.dot(jax.nn.silu(jnp.dot(h, wg)) * jnp.dot(h, wu), wd)


def moe_ffn(h, router_w, wg, wu, wd):
    n_tok, d = h.shape
    logits = jnp.dot(h, router_w).astype(jnp.float32)
    top_v, top_i = lax.top_k(logits, TOP_K)
    gates = jax.nn.softmax(top_v, axis=-1).astype(h.dtype)
    n_asg = n_tok * TOP_K
    flat_e = top_i.reshape(n_asg)
    flat_tok = jnp.arange(n_asg, dtype=jnp.int32) // TOP_K
    flat_g = gates.reshape(n_asg)
    order = jnp.argsort(flat_e)
    se = flat_e[order]
    counts = jnp.zeros(N_EXPERTS, jnp.int32).at[flat_e].add(1)
    padded = (counts + MOE_BLOCK - 1) // MOE_BLOCK * MOE_BLOCK
    pad_end = jnp.cumsum(padded)
    pad_start = pad_end - padded
    start = jnp.cumsum(counts) - counts
    dest = pad_start[se] + jnp.arange(n_asg, dtype=jnp.int32) - start[se]
    n_blocks = -(-n_asg // MOE_BLOCK) + N_EXPERTS
    n_rows = n_blocks * MOE_BLOCK
    row_tok = jnp.zeros(n_rows, jnp.int32).at[dest].set(flat_tok[order])
    row_w = jnp.zeros(n_rows, h.dtype).at[dest].set(flat_g[order])
    blk_e = jnp.minimum(jnp.searchsorted(pad_end, jnp.arange(n_blocks, dtype=jnp.int32) * MOE_BLOCK,
                                         side='right'), N_EXPERTS - 1)
    xb = h[row_tok].reshape(n_blocks, MOE_BLOCK, d)

    def expert_block(args):
        x_blk, e = args
        return swiglu(x_blk, wg[e], wu[e], wd[e])

    yb = lax.map(expert_block, (xb, blk_e)).reshape(n_rows, d)
    return jax.ops.segment_sum(yb * row_w[:, None], row_tok, num_segments=n_tok)


def _run_group(x, c, cache_k, cache_v, page_table, state_conv, state_ssm, state_gla, params):
    (w_ada, b_ada, w_in, w_out, rel_bias, lambda_qk, subln_g, conv_w, conv_b, dt_bias, a_log,
     d_skip, ssd_norm_g, gla_w_lr, gla_b_lr, gla_norm_g, ffn_w_gate, ffn_w_up, ffn_w_down,
     router_w, moe_w_gate, moe_w_up, moe_w_down, final_norm_g) = params
    prompt = cache_k is None
    bsz, seq, _ = x.shape
    f32 = jnp.float32
    grp = jnp.arange(H_B) // (H_B // G_B)
    new_k, new_v, new_conv, new_ssm, new_gla = [], [], [], [], []
    for l in range(DEPTH):
        mod = jnp.dot(jax.nn.silu(c), w_ada[l]) + b_ada[l]
        sh1, sc1, g1, sh2, sc2, g2 = jnp.split(mod[:, None, :], 6, axis=-1)
        h = rmsnorm(x) * (1 + sc1) + sh1
        z = jnp.dot(h, w_in[l])

        q = z[..., OFF_AQ:OFF_AK].reshape(bsz, seq, H_A, 2, DK_A)
        k = z[..., OFF_AK:OFF_AV].reshape(bsz, seq, H_A, 2, DK_A)
        v = z[..., OFF_AV:OFF_BZ].reshape(bsz, seq, H_A, DV_A)
        lam_init = 0.8 - 0.6 * math.exp(-0.3 * l)
        lqk = lambda_qk[l].astype(f32)
        lam = jnp.exp(jnp.sum(lqk[0] * lqk[1])) - jnp.exp(jnp.sum(lqk[2] * lqk[3])) + lam_init
        if prompt:
            oa = diff_attn_prompt(q, k, v, rel_bias, lam)
        else:
            kp = cache_k[l, page_table]
            past = kp.shape[1] * kp.shape[2]
            kp = kp.reshape(bsz, past, H_A, 2, DK_A).astype(k.dtype)
            vp = cache_v[l, page_table].reshape(bsz, past, H_A, DV_A).astype(v.dtype)
            oa = diff_attn_core(q, jnp.concatenate([kp, k], axis=1), jnp.concatenate([vp, v], axis=1),
                                past + jnp.arange(seq), jnp.arange(past + seq), rel_bias, lam)
        oa = (rmsnorm(oa) * subln_g[l] * (1.0 - lam_init)).reshape(bsz, seq, W_A)
        new_k.append(k.reshape(bsz, seq, 2 * H_A, DK_A))
        new_v.append(v)

        conv_buf = jnp.zeros((bsz, CONV_W - 1, XBC_B), x.dtype) if prompt else state_conv[l]
        xbc, buf = causal_conv(z[..., OFF_BX:OFF_BDT], conv_buf, conv_w[l], conv_b[l])
        xs = xbc[..., :W_B].reshape(bsz, seq, H_B, P_B).astype(f32)
        bm = xbc[..., W_B:W_B + G_B * N_B].reshape(bsz, seq, G_B, N_B)[:, :, grp].astype(f32)
        cm = xbc[..., W_B + G_B * N_B:].reshape(bsz, seq, G_B, N_B)[:, :, grp].astype(f32)
        dt = jax.nn.softplus(z[..., OFF_BDT:OFF_CQ].astype(f32) + dt_bias[l].astype(f32))
        a = -jnp.exp(a_log[l].astype(f32))
        if prompt:
            yb, sb = ssd_chunked(xs, dt, a, bm, cm)
        else:
            yb, sb = ssd_recurrent(xs, dt, a, bm, cm, state_ssm[l].astype(f32))
        yb = (yb + d_skip[l].astype(f32)[:, None] * xs).reshape(bsz, seq, W_B)
        ob = (rmsnorm(yb * jax.nn.silu(z[..., OFF_BZ:OFF_BX].astype(f32))) * ssd_norm_g[l]).astype(x.dtype)
        new_conv.append(buf)
        new_ssm.append(sb.astype(x.dtype))

        qc = z[..., OFF_CQ:OFF_CK].reshape(bsz, seq, H_C, DK_C).astype(f32) * (DK_C ** -0.5)
        kc = z[..., OFF_CK:OFF_CV].reshape(bsz, seq, H_C, DK_C).astype(f32)
        vc = z[..., OFF_CV:OFF_CG].reshape(bsz, seq, H_C, DV_C).astype(f32)
        gk = jax.nn.log_sigmoid((jnp.dot(z[..., OFF_CLR:D_IN], gla_w_lr[l]) + gla_b_lr[l]).astype(f32)) / GATE_NORM
        gk = gk.reshape(bsz, seq, H_C, DK_C)
        if prompt:
            oc, sg = gla_chunked(qc, kc, vc, gk)
        else:
            oc, sg = gla_recurrent(qc, kc, vc, gk, state_gla[l].astype(f32))
        oc = (rmsnorm(oc) * gla_norm_g[l]).reshape(bsz, seq, W_C) * jax.nn.silu(z[..., OFF_CG:OFF_CLR].astype(f32))
        new_gla.append(sg.astype(x.dtype))

        mix = jnp.dot(jnp.concatenate([oa, ob, oc.astype(x.dtype)], axis=-1), w_out[l])
        x = x + g1 * mix

        h = rmsnorm(x) * (1 + sc2) + sh2
        i = l // 2
        if l % 2 == 0:
            f = swiglu(h, ffn_w_gate[i], ffn_w_up[i], ffn_w_down[i])
        else:
            f = moe_ffn(h.reshape(bsz * seq, D_MODEL), router_w[i], moe_w_gate[i], moe_w_up[i],
                        moe_w_down[i]).reshape(x.shape)
        x = x + g2 * f
    y = rmsnorm(x) * final_norm_g
    return (y, jnp.stack(new_k), jnp.stack(new_v), jnp.stack(new_conv), jnp.stack(new_ssm), jnp.stack(new_gla))


def setup_inputs(seed: int = 0) -> dict:
    key = jax.random.key(seed)
    keys = iter(jax.random.split(key, 48))
    f32 = jnp.float32
    n_pages = PAST_LEN // PAGE_SIZE
    n_pool = int(math.ceil(POOL_FACTOR * DEC_BATCH * n_pages))

    def nrm(shape, scale):
        return scale * jax.random.normal(next(keys), shape, f32)

    x_prompt = nrm((BATCH, SEQ, D_MODEL), 1.0)
    x_sample = nrm((DEC_BATCH, DEC_SEQ, D_MODEL), 1.0)
    c_prompt = nrm((BATCH, D_MODEL), 1.0)
    c_sample = nrm((DEC_BATCH, D_MODEL), 1.0)
    cache_k = nrm((DEPTH, n_pool, PAGE_SIZE, 2 * H_A, DK_A), 1.0)
    cache_v = nrm((DEPTH, n_pool, PAGE_SIZE, H_A, DV_A), 1.0)
    page_table = jax.random.permutation(next(keys), n_pool)[:DEC_BATCH * n_pages].reshape(
        DEC_BATCH, n_pages).astype(jnp.int32)
    state_conv = nrm((DEPTH, DEC_BATCH, CONV_W - 1, XBC_B), 1.0)
    state_ssm = nrm((DEPTH, DEC_BATCH, H_B, P_B, N_B), 0.5)
    state_gla = nrm((DEPTH, DEC_BATCH, H_C, DK_C, DV_C), 0.5)

    w_ada = nrm((DEPTH, D_MODEL, 6 * D_MODEL), 0.5 * D_MODEL ** -0.5)
    b_ada = nrm((DEPTH, 6 * D_MODEL), 0.02)
    w_in = nrm((DEPTH, D_MODEL, D_IN), D_MODEL ** -0.5)
    w_out = nrm((DEPTH, MIX_W, D_MODEL), MIX_W ** -0.5)
    rel_bias = nrm((T5_BUCKETS, H_A), 0.5)
    lambda_qk = nrm((DEPTH, 4, DK_A), 0.1)
    subln_g = 1.0 + nrm((DEPTH, DV_A), 0.02)
    conv_w = jax.random.uniform(next(keys), (DEPTH, CONV_W, XBC_B), f32, -0.5, 0.5)
    conv_b = nrm((DEPTH, XBC_B), 0.02)
    dt0 = jnp.exp(jax.random.uniform(next(keys), (DEPTH, H_B), f32, math.log(1e-3), math.log(1e-1)))
    dt_bias = dt0 + jnp.log(-jnp.expm1(-dt0))
    a_log = jnp.log(jax.random.uniform(next(keys), (DEPTH, H_B), f32, 1.0, 16.0))
    d_skip = 1.0 + nrm((DEPTH, H_B), 0.1)
    ssd_norm_g = 1.0 + nrm((DEPTH, W_B), 0.02)
    gla_w_lr = nrm((DEPTH, GATE_RANK, H_C * DK_C), GATE_RANK ** -0.5)
    gla_b_lr = nrm((DEPTH, H_C * DK_C), 0.1)
    gla_norm_g = 1.0 + nrm((DEPTH, DV_C), 0.02)
    ffn_w_gate = nrm((N_DENSE, D_MODEL, D_FF), D_MODEL ** -0.5)
    ffn_w_up = nrm((N_DENSE, D_MODEL, D_FF), D_MODEL ** -0.5)
    ffn_w_down = nrm((N_DENSE, D_FF, D_MODEL), D_FF ** -0.5)
    router_w = nrm((N_MOE, D_MODEL, N_EXPERTS), D_MODEL ** -0.5)
    moe_w_gate = nrm((N_MOE, N_EXPERTS, D_MODEL, D_FF), D_MODEL ** -0.5)
    moe_w_up = nrm((N_MOE, N_EXPERTS, D_MODEL, D_FF), D_MODEL ** -0.5)
    moe_w_down = nrm((N_MOE, N_EXPERTS, D_FF, D_MODEL), D_FF ** -0.5)
    final_norm_g = 1.0 + nrm((D_MODEL,), 0.02)
    return {'x_prompt': x_prompt, 'x_sample': x_sample, 'c_prompt': c_prompt, 'c_sample': c_sample,
            'cache_k': cache_k, 'cache_v': cache_v, 'page_table': page_table,
            'state_conv': state_conv, 'state_ssm': state_ssm, 'state_gla': state_gla,
            'w_ada': w_ada, 'b_ada': b_ada, 'w_in': w_in, 'w_out': w_out, 'rel_bias': rel_bias,
            'lambda_qk': lambda_qk, 'subln_g': subln_g, 'conv_w': conv_w, 'conv_b': conv_b,
            'dt_bias': dt_bias, 'a_log': a_log, 'd_skip': d_skip, 'ssd_norm_g': ssd_norm_g,
            'gla_w_lr': gla_w_lr, 'gla_b_lr': gla_b_lr, 'gla_norm_g': gla_norm_g,
            'ffn_w_gate': ffn_w_gate, 'ffn_w_up': ffn_w_up, 'ffn_w_down': ffn_w_down,
            'router_w': router_w, 'moe_w_gate': moe_w_gate, 'moe_w_up': moe_w_up,
            'moe_w_down': moe_w_down, 'final_norm_g': final_norm_g}


def reference(x_prompt, x_sample, c_prompt, c_sample, cache_k, cache_v, page_table,
              state_conv, state_ssm, state_gla, w_ada, b_ada, w_in, w_out, rel_bias,
              lambda_qk, subln_g, conv_w, conv_b, dt_bias, a_log, d_skip, ssd_norm_g,
              gla_w_lr, gla_b_lr, gla_norm_g, ffn_w_gate, ffn_w_up, ffn_w_down,
              router_w, moe_w_gate, moe_w_up, moe_w_down, final_norm_g):
    params = (w_ada, b_ada, w_in, w_out, rel_bias, lambda_qk, subln_g, conv_w, conv_b, dt_bias,
              a_log, d_skip, ssd_norm_g, gla_w_lr, gla_b_lr, gla_norm_g, ffn_w_gate, ffn_w_up,
              ffn_w_down, router_w, moe_w_gate, moe_w_up, moe_w_down, final_norm_g)
    y_prompt, k_prompt, v_prompt, conv_prompt, ssm_prompt, gla_prompt = _run_group(
        x_prompt, c_prompt, None, None, None, None, None, None, params)
    y_sample, k_sample, v_sample, conv_sample, ssm_sample, gla_sample = _run_group(
        x_sample, c_sample, cache_k, cache_v, page_table, state_conv, state_ssm, state_gla, params)
    return (y_prompt, y_sample, k_prompt, v_prompt, conv_prompt, ssm_prompt, gla_prompt,
            k_sample, v_sample, conv_sample, ssm_sample, gla_sample)
```

```python
import functools
import math

import numpy as np
import jax
import jax.numpy as jnp
from jax import lax
from jax.experimental import pallas as pl
from jax.experimental.pallas import tpu as pltpu

F32 = jnp.float32
BF16 = jnp.bfloat16
I32 = jnp.int32

D_MODEL = 1024
H_A, DK_A, DV_A = 4, 64, 128
W_A = H_A * DV_A
H_B, P_B, G_B, N_B, CONV_W = 4, 64, 2, 64, 4
W_B = H_B * P_B
XBC_B = W_B + 2 * G_B * N_B
H_C, DK_C, DV_C = 4, 32, 64
W_C = H_C * DV_C
GATE_RANK, GATE_NORM = 16, 16.0
T5_BUCKETS, T5_MAX_DIST = 32, 128
D_FF, N_EXPERTS, TOP_K = 2816, 8, 2
EPS = 1e-6
PAGE = 128
CHUNK = 128
OFF_AQ = 0
OFF_AK = OFF_AQ + 2 * H_A * DK_A
OFF_AV = OFF_AK + 2 * H_A * DK_A
OFF_BZ = OFF_AV + W_A
OFF_BX = OFF_BZ + W_B
OFF_BDT = OFF_BX + XBC_B
OFF_CQ = OFF_BDT + H_B
OFF_CK = OFF_CQ + H_C * DK_C
OFF_CV = OFF_CK + H_C * DK_C
OFF_CG = OFF_CV + W_C
OFF_CLR = OFF_CG + W_C
D_IN = OFF_CLR + GATE_RANK
PK_REST = 3 * W_A
R_BZ, R_BX, R_CQ, R_CK, R_CV, R_CG, R_MISC = 0, 256, 768, 896, 1024, 1280, 1536
W_REST = R_MISC + 128
D_PACK = PK_REST + W_REST
MISC_DT, MISC_LR = 0, H_B
NEG = -1e30
VMEM_LIMIT = 56 * 1024 * 1024


def _cparams(sem):
    return pltpu.CompilerParams(dimension_semantics=sem, vmem_limit_bytes=VMEM_LIMIT)


def _rms(x):
    return x * lax.rsqrt(jnp.mean(x * x, axis=-1, keepdims=True) + EPS)


def _silu(x):
    return x * jax.nn.sigmoid(x)


def _softplus(x):
    return jnp.maximum(x, 0.0) + jnp.log1p(jnp.exp(-jnp.abs(x)))


def _split2(x):
    hi = x.astype(BF16)
    return hi, (x - hi.astype(F32)).astype(BF16)


def _split3(x):
    hi = x.astype(BF16)
    r = x - hi.astype(F32)
    mid = r.astype(BF16)
    return hi, mid, (r - mid.astype(F32)).astype(BF16)


def _dotf(a, b):
    return jnp.dot(a, b, preferred_element_type=F32)


def _lmat_exact(mat01, x):
    hi, mid, lo = _split3(x)
    return _dotf(mat01, hi) + _dotf(mat01, mid) + _dotf(mat01, lo)


def _rmat_exact(x, mat01, pieces=3):
    if pieces == 2:
        hi, lo = _split2(x)
        return _dotf(hi, mat01) + _dotf(lo, mat01)
    hi, mid, lo = _split3(x)
    return _dotf(hi, mat01) + _dotf(mid, mat01) + _dotf(lo, mat01)


def _iota(shape, axis):
    return lax.broadcasted_iota(I32, shape, axis)


def _ada_kernel(c_ref, w_ref, b_ref, o_ref):
    c = c_ref[...]
    o_ref[...] = _dotf(_silu(c).astype(BF16), w_ref[...].astype(BF16)) + b_ref[...]


def ada_modulation(c_all, w_ada, b_ada):
    depth, d, n = w_ada.shape
    bc = c_all.shape[0]
    tn = 1536 if n % 1536 == 0 else n
    return pl.pallas_call(
        _ada_kernel,
        out_shape=jax.ShapeDtypeStruct((depth, bc, n), F32),
        grid=(depth, n // tn),
        in_specs=[pl.BlockSpec((bc, d), lambda l, j: (0, 0)),
                  pl.BlockSpec((None, d, tn), lambda l, j: (l, 0, j)),
                  pl.BlockSpec((None, 1, tn), lambda l, j: (l, 0, j))],
        out_specs=pl.BlockSpec((None, bc, tn), lambda l, j: (l, 0, j)),
        compiler_params=_cparams(("arbitrary", "arbitrary")),
        name="ada_modulation",
    )(c_all, w_ada, b_ada.reshape(depth, 1, n))


def _mod_spec(tm, per_row, chunk):
    if per_row:
        return pl.BlockSpec((None, None, tm, D_MODEL), lambda g, i, l: (l[0], g, i, chunk))
    return pl.BlockSpec((None, None, 1, D_MODEL), lambda g, i, l: (l[0], g, 0, chunk))


def _in_proj_kernel(l_ref, x_ref, sh_ref, sc_ref, w_ref, q_ref, k_ref, kb_ref, v_ref, vb_ref, r_ref):
    del l_ref
    h = (_rms(x_ref[...]) * (1.0 + sc_ref[...]) + sh_ref[...]).astype(BF16)
    q_ref[...] = (_dotf(h, w_ref[:, 0:W_A]) * (DK_A ** -0.5)).astype(BF16)
    k = _dotf(h, w_ref[:, W_A:2 * W_A])
    k_ref[...] = k
    kb_ref[...] = k.astype(BF16)
    v = _dotf(h, w_ref[:, 2 * W_A:3 * W_A])
    v_ref[...] = v
    vb_ref[...] = v.astype(BF16)
    r_ref[...] = _dotf(h, w_ref[:, PK_REST:D_PACK])


def in_projection(lidx, x, mod, w_pack, tm):
    g, r, d = x.shape
    per_row = mod.shape[2] > 1
    row = lambda w, dt: jax.ShapeDtypeStruct((g, r, w), dt)
    ospec = lambda w: pl.BlockSpec((None, tm, w), lambda gi, i, l: (gi, i, 0))
    grid_spec = pltpu.PrefetchScalarGridSpec(
        num_scalar_prefetch=1, grid=(g, r // tm),
        in_specs=[pl.BlockSpec((None, tm, d), lambda gi, i, l: (gi, i, 0)),
                  _mod_spec(tm, per_row, 0), _mod_spec(tm, per_row, 1),
                  pl.BlockSpec((None, d, D_PACK), lambda gi, i, l: (l[0], 0, 0))],
        out_specs=[ospec(W_A), ospec(W_A), ospec(W_A), ospec(W_A), ospec(W_A), ospec(W_REST)])
    return pl.pallas_call(
        _in_proj_kernel,
        out_shape=[row(W_A, BF16), row(W_A, F32), row(W_A, BF16), row(W_A, F32), row(W_A, BF16), row(W_REST, F32)],
        grid_spec=grid_spec,
        compiler_params=_cparams(("arbitrary", "arbitrary")),
        name="in_projection",
    )(lidx, x, mod, mod, w_pack)


def _lambda_value(lqk, lam_init):
    s01 = jnp.sum(lqk[0:1, :] * lqk[1:2, :], axis=1, keepdims=True)
    s23 = jnp.sum(lqk[2:3, :] * lqk[3:4, :], axis=1, keepdims=True)
    return jnp.exp(s01) - jnp.exp(s23) + lam_init


def _attn_prompt_kernel(l_ref, lam_ref, lqk_ref, g_ref, q_ref, k_ref, v_ref, bias_ref, o_ref,
                        m_ref, s_ref, acc_ref, *, blk):
    qi = pl.program_id(2)
    q = q_ref[...]
    lane = _iota(q.shape, 1)
    zero = jnp.zeros_like(q)
    qm = (jnp.where(lane < DK_A, q, zero), jnp.where(lane >= DK_A, q, zero))
    m_ref[...] = jnp.full(m_ref.shape, NEG, F32)
    s_ref[...] = jnp.zeros(s_ref.shape, F32)
    acc_ref[...] = jnp.zeros(acc_ref.shape, F32)

    def update(j, bias):
        start = pl.multiple_of(j * blk, blk)
        kb = k_ref[pl.ds(start, blk), :]
        vb = v_ref[pl.ds(start, blk), :]
        for m in range(2):
            s = lax.dot_general(qm[m], kb, (((1,), (1,)), ((), ())), preferred_element_type=F32)
            if bias is not None:
                s = s + bias
            m_old = m_ref[m]
            m_new = jnp.maximum(m_old, jnp.max(s, axis=1, keepdims=True))
            alpha = jnp.exp(m_old - m_new)
            p = jnp.exp(s - m_new)
            s_ref[m] = alpha * s_ref[m] + jnp.sum(p, axis=1, keepdims=True)
            acc_ref[m] = alpha * acc_ref[m] + _dotf(p.astype(BF16), vb)
            m_ref[m] = m_new

    def far(j, carry):
        update(j, None)
        return carry

    lax.fori_loop(0, jnp.maximum(qi - 1, 0), far, 0)

    @pl.when(qi >= 1)
    def _():
        update(qi - 1, bias_ref[1])

    update(qi, bias_ref[0])

    layer = l_ref[0]
    lam = _lambda_value(lqk_ref[...], lam_ref[2 * layer])
    o = acc_ref[0] * (1.0 / s_ref[0]) - lam * (acc_ref[1] * (1.0 / s_ref[1]))
    o_ref[...] = (_rms(o) * g_ref[...] * lam_ref[2 * layer + 1]).astype(BF16)


def _t5_table(rel_bias, n):
    dist = jnp.arange(n)
    max_exact = T5_BUCKETS // 2
    nf = jnp.maximum(dist, 1).astype(F32)
    large = max_exact + (jnp.log(nf / max_exact) / math.log(T5_MAX_DIST / max_exact)
                         * (T5_BUCKETS - max_exact)).astype(I32)
    bucket = jnp.where(dist < max_exact, dist, jnp.minimum(large, T5_BUCKETS - 1))
    return (rel_bias[bucket] - rel_bias[T5_BUCKETS - 1][None, :]).T.astype(F32)


def attention_prompt(lidx, lam_consts, lambda_qk, subln_g3, q, kb, vb, bias_tiles, blk):
    b, seq, _ = q.shape
    grid_spec = pltpu.PrefetchScalarGridSpec(
        num_scalar_prefetch=1, grid=(b, H_A, seq // blk),
        in_specs=[pl.BlockSpec(memory_space=pltpu.SMEM),
                  pl.BlockSpec((None, 4, DK_A), lambda bi, h, i, l: (l[0], 0, 0)),
                  pl.BlockSpec((None, 1, DV_A), lambda bi, h, i, l: (l[0], 0, 0)),
                  pl.BlockSpec((None, blk, DV_A), lambda bi, h, i, l: (bi, i, h)),
                  pl.BlockSpec((None, seq, DV_A), lambda bi, h, i, l: (bi, 0, h)),
                  pl.BlockSpec((None, seq, DV_A), lambda bi, h, i, l: (bi, 0, h)),
                  pl.BlockSpec((None, 2, blk, blk), lambda bi, h, i, l: (h, 0, 0, 0))],
        out_specs=pl.BlockSpec((None, blk, DV_A), lambda bi, h, i, l: (bi, i, h)),
        scratch_shapes=[pltpu.VMEM((2, blk, 1), F32), pltpu.VMEM((2, blk, 1), F32),
                        pltpu.VMEM((2, blk, DV_A), F32)])
    return pl.pallas_call(
        functools.partial(_attn_prompt_kernel, blk=blk),
        out_shape=jax.ShapeDtypeStruct((b, seq, W_A), BF16),
        grid_spec=grid_spec,
        compiler_params=_cparams(("arbitrary", "arbitrary", "arbitrary")),
        name="attention_prompt",
    )(lidx, lam_consts, lambda_qk, subln_g3, q, kb, vb, bias_tiles)


def _bc_prompt_kernel(l_ref, rest_ref, convw_ref, convb_ref, v128_ref, v256_ref, wlr_ref,
                      ob_ref, oc_ref, conv_out_ref, ssm_out_ref, gla_out_ref,
                      xpad_ref, sbd_ref, sg_ref, b_ref, oi_ref):
    del l_ref
    c = pl.program_id(1)
    q = CHUNK

    @pl.when(c == 0)
    def _():
        xpad_ref[0:8, :] = jnp.zeros((8, XBC_B), F32)
        sbd_ref[...] = jnp.zeros(sbd_ref.shape, F32)
        sg_ref[...] = jnp.zeros(sg_ref.shape, F32)

    xpad_ref[8:8 + q, :] = rest_ref[:, R_BX:R_BX + XBC_B]
    conv = convb_ref[...]
    for j in range(CONV_W):
        conv = conv + convw_ref[j:j + 1, :] * xpad_ref[5 + j:5 + j + q, :]
    xbc = _silu(conv)
    tail = xpad_ref[5 + q:8 + q, :]
    conv_out_ref[...] = tail
    xpad_ref[5:8, :] = tail

    row = _iota((q, q), 0)
    col = _iota((q, q), 1)
    causal = row >= col
    tril = jnp.where(causal, 1.0, 0.0).astype(BF16)
    lane1 = _iota((1, 128), 1)
    misc = rest_ref[:, R_MISC:R_MISC + 128]

    xs = xbc[:, 0:W_B]
    bmat = xbc[:, W_B:W_B + G_B * N_B].astype(BF16)
    cmat = xbc[:, W_B + G_B * N_B:XBC_B].astype(BF16)
    is_head = lane1 < H_B
    dt = jnp.where(is_head, _softplus(misc + v128_ref[0:1, :]), 0.0)
    a_neg = jnp.where(is_head, -jnp.exp(v128_ref[1:2, :]), 0.0)
    cum = _lmat_exact(tril, dt * a_neg)
    cum_t = cum.T
    ecum = jnp.exp(cum)
    wgt = jnp.exp(cum[q - 1:q, :] - cum) * dt
    expand = jnp.where(_iota((128, W_B), 0) == (_iota((128, W_B), 1) >> 6), 1.0, 0.0).astype(BF16)
    dt_e = _rmat_exact(dt, expand)
    ecum_e = _rmat_exact(ecum, expand)
    wgt_e = _rmat_exact(wgt, expand)
    grp_lane = _iota((q, 128), 1) >> 6
    zero_b = jnp.zeros_like(cmat)
    gmat = [lax.dot_general(jnp.where(grp_lane == g, cmat, zero_b), bmat, (((1,), (1,)), ((), ())),
                            preferred_element_type=F32) for g in range(G_B)]
    mcat = []
    for h in range(H_B):
        diff = cum[:, h:h + 1] - cum_t[h:h + 1, :]
        dec = jnp.exp(jnp.where(causal, diff, NEG))
        mcat.append((gmat[h // (H_B // G_B)] * dec).astype(BF16))
    mcat = jnp.concatenate(mcat, axis=1)
    head_lane = _iota((q, W_B), 1) >> 6
    dtx = xs * dt_e
    xbd = jnp.concatenate([jnp.where(head_lane == h, dtx, 0.0) for h in range(H_B)], axis=0).astype(BF16)
    state = sbd_ref[...]
    y = _dotf(mcat, xbd) + _dotf(cmat, state.astype(BF16)) * ecum_e
    upd = lax.dot_general(bmat, (xs * wgt_e).astype(BF16), (((0,), (0,)), ((), ())), preferred_element_type=F32)
    same_grp = (_iota((128, W_B), 0) >> 6) == (_iota((128, W_B), 1) >> 7)
    state = state * ecum_e[q - 1:q, :] + jnp.where(same_grp, upd, 0.0)
    sbd_ref[...] = state
    ssm_out_ref[...] = state
    y = y + v256_ref[0:1, :] * xs
    ob_ref[...] = (_rms(y * _silu(rest_ref[:, R_BZ:R_BZ + W_B])) * v256_ref[1:2, :]).astype(BF16)

    qg = rest_ref[:, R_CQ:R_CQ + 128] * (DK_C ** -0.5)
    kg = rest_ref[:, R_CK:R_CK + 128]
    vg = rest_ref[:, R_CV:R_CV + W_C]
    gk = _dotf(misc.astype(BF16), wlr_ref[...]) + v128_ref[2:3, :]
    gate = -_softplus(-gk) * (1.0 / GATE_NORM)
    bcum = _lmat_exact(tril, gate)
    b_last = bcum[q - 1:q, :]
    span = jnp.max(-b_last)
    qe = qg * jnp.exp(bcum)
    vhead = _iota((q, W_C), 1) >> 6

    @pl.when(span <= 80.0)
    def _():
        ke = kg * jnp.exp(-bcum)
        kbd = jnp.where((_iota((128, 4 * q), 0) >> 5) == (_iota((128, 4 * q), 1) >> 7),
                        jnp.tile(ke.T, (1, H_C)), 0.0).astype(BF16)
        att = _dotf(qe.astype(BF16), kbd)
        causal4 = _iota((q, H_C * q), 0) >= (_iota((q, H_C * q), 1) & (q - 1))
        att = jnp.where(causal4, att, 0.0).astype(BF16)
        vbd = jnp.concatenate([jnp.where(vhead == h, vg, 0.0) for h in range(H_C)], axis=0).astype(BF16)
        oi_ref[...] = _dotf(att, vbd)

    @pl.when(span > 80.0)
    def _():
        b_ref[...] = bcum
        ind = jnp.where((_iota((128, W_C), 0) >> 5) == (_iota((128, W_C), 1) >> 6), 1.0, 0.0).astype(BF16)
        trow = _iota((q, 128), 0)

        def body(grp, acc):
            base = pl.multiple_of(grp * 8, 8)
            ks8 = rest_ref[pl.ds(base, 8), R_CK:R_CK + 128]
            vs8 = rest_ref[pl.ds(base, 8), R_CV:R_CV + W_C]
            bs8 = b_ref[pl.ds(base, 8), :]
            for j in range(8):
                d = jnp.exp(jnp.minimum(bcum - bs8[j:j + 1, :], 0.0)) * qg * ks8[j:j + 1, :]
                d = jnp.where(trow >= base + j, d, 0.0)
                acc = acc + _dotf(d.astype(BF16), ind) * vs8[j:j + 1, :]
            return acc

        oi_ref[...] = lax.fori_loop(0, q // 8, body, jnp.zeros((q, W_C), F32))

    sg = sg_ref[...]
    o = oi_ref[...] + _dotf(qe.astype(BF16), sg.astype(BF16))
    k2 = (kg * jnp.exp(b_last - bcum)).astype(BF16)
    updg = lax.dot_general(k2, vg.astype(BF16), (((0,), (0,)), ((), ())), preferred_element_type=F32)
    same_head = (_iota((128, W_C), 0) >> 5) == (_iota((128, W_C), 1) >> 6)
    sg = sg * jnp.exp(bcum.T[:, q - 1:q]) + jnp.where(same_head, updg, 0.0)
    sg_ref[...] = sg
    gla_out_ref[...] = sg
    avg = jnp.where((_iota((W_C, W_C), 0) >> 6) == (_iota((W_C, W_C), 1) >> 6), 1.0, 0.0).astype(BF16)
    ms = _rmat_exact(o * o, avg) * (1.0 / DV_C)
    oc = o * lax.rsqrt(ms + EPS) * v256_ref[2:3, :] * _silu(rest_ref[:, R_CG:R_CG + W_C])
    oc_ref[...] = oc.astype(BF16)


def bc_mixers_prompt(lidx, rest, convw, convb3, vec128, vec256, wlr):
    b, seq, _ = rest.shape
    nc = seq // CHUNK
    par = lambda r, w: pl.BlockSpec((None, r, w), lambda bi, c, l: (l[0], 0, 0))
    st = lambda r, w: pl.BlockSpec((None, r, w), lambda bi, c, l: (bi, 0, 0))
    grid_spec = pltpu.PrefetchScalarGridSpec(
        num_scalar_prefetch=1, grid=(b, nc),
        in_specs=[pl.BlockSpec((None, CHUNK, W_REST), lambda bi, c, l: (bi, c, 0)),
                  par(CONV_W, XBC_B), par(1, XBC_B), par(8, 128), par(8, 256), par(128, 128)],
        out_specs=[pl.BlockSpec((None, CHUNK, W_B), lambda bi, c, l: (bi, c, 0)),
                   pl.BlockSpec((None, CHUNK, W_C), lambda bi, c, l: (bi, c, 0)),
                   st(CONV_W - 1, XBC_B), st(128, W_B), st(128, W_C)],
        scratch_shapes=[pltpu.VMEM((8 + CHUNK, XBC_B), F32), pltpu.VMEM((128, W_B), F32),
                        pltpu.VMEM((128, W_C), F32), pltpu.VMEM((CHUNK, 128), F32),
                        pltpu.VMEM((CHUNK, W_C), F32)])
    return pl.pallas_call(
        _bc_prompt_kernel,
        out_shape=[jax.ShapeDtypeStruct((b, seq, W_B), BF16), jax.ShapeDtypeStruct((b, seq, W_C), BF16),
                   jax.ShapeDtypeStruct((b, CONV_W - 1, XBC_B), F32),
                   jax.ShapeDtypeStruct((b, 128, W_B), F32), jax.ShapeDtypeStruct((b, 128, W_C), F32)],
        grid_spec=grid_spec,
        compiler_params=_cparams(("arbitrary", "arbitrary")),
        name="bc_mixers_prompt",
    )(lidx, rest, convw, convb3, vec128, vec256, wlr)


def _unpack_ssm_state(sbd):
    b = sbd.shape[0]
    s = sbd.reshape(b, G_B, N_B, H_B, P_B)
    per_head = [s[:, h // (H_B // G_B), :, h, :] for h in range(H_B)]
    return jnp.swapaxes(jnp.stack(per_head, axis=1), 2, 3)


def _unpack_gla_state(sg):
    b = sg.shape[0]
    s = sg.reshape(b, H_C, DK_C, H_C, DV_C)
    return jnp.stack([s[:, h, :, h, :] for h in range(H_C)], axis=1)


def _out_proj_kernel(l_ref, oa_ref, ob_ref, oc_ref, w_ref, x_ref, g1_ref, sh2_ref, sc2_ref, x1_ref, h2_ref):
    del l_ref
    mix = (_dotf(oa_ref[...], w_ref[0:W_A, :]) + _dotf(ob_ref[...], w_ref[W_A:W_A + W_B, :])
           + _dotf(oc_ref[...], w_ref[W_A + W_B:, :]))
    x1 = x_ref[...] + g1_ref[...] * mix
    x1_ref[...] = x1
    h2_ref[...] = (_rms(x1) * (1.0 + sc2_ref[...]) + sh2_ref[...]).astype(h2_ref.dtype)


def out_projection(lidx, oa, ob, oc, w_out, x, mod, tm, h2_dtype):
    g, r, d = x.shape
    per_row = mod.shape[2] > 1
    act = lambda w: pl.BlockSpec((None, tm, w), lambda gi, i, l: (gi, i, 0))
    grid_spec = pltpu.PrefetchScalarGridSpec(
        num_scalar_prefetch=1, grid=(g, r // tm),
        in_specs=[act(W_A), act(W_B), act(W_C),
                  pl.BlockSpec((None, d, d), lambda gi, i, l: (l[0], 0, 0)),
                  act(d), _mod_spec(tm, per_row, 2), _mod_spec(tm, per_row, 3), _mod_spec(tm, per_row, 4)],
        out_specs=[act(d), act(d)])
    return pl.pallas_call(
        _out_proj_kernel,
        out_shape=[jax.ShapeDtypeStruct((g, r, d), F32), jax.ShapeDtypeStruct((g, r, d), h2_dtype)],
        grid_spec=grid_spec,
        compiler_params=_cparams(("arbitrary", "arbitrary")),
        name="out_projection",
    )(lidx, oa, ob, oc, w_out, x, mod, mod, mod)


def _ffn_kernel(i_ref, h_ref, wg_ref, wu_ref, wd_ref, x1_ref, g2_ref, fg_ref, o_ref, acc_ref, *, final):
    del i_ref
    f = pl.program_id(2)

    @pl.when(f == 0)
    def _():
        acc_ref[...] = jnp.zeros(acc_ref.shape, F32)

    h = h_ref[...]
    a = (_silu(_dotf(h, wg_ref[...])) * _dotf(h, wu_ref[...])).astype(BF16)
    acc_ref[...] += _dotf(a, wd_ref[...])

    @pl.when(f == pl.num_programs(2) - 1)
    def _():
        x2 = x1_ref[...] + g2_ref[...] * acc_ref[...]
        o_ref[...] = _rms(x2) * fg_ref[...] if final else x2


def dense_ffn(iidx, lidx_mod, h2, wg, wu, wd, x1, mod, final_g, tm, tf, final):
    g, r, d = x1.shape
    per_row = mod.shape[2] > 1
    nf = D_FF // tf
    act = lambda w: pl.BlockSpec((None, tm, w), lambda gi, i, f, s: (gi, i, 0))
    if per_row:
        g2 = pl.BlockSpec((None, None, tm, d), lambda gi, i, f, s: (s[1], gi, i, 5))
    else:
        g2 = pl.BlockSpec((None, None, 1, d), lambda gi, i, f, s: (s[1], gi, 0, 5))
    grid_spec = pltpu.PrefetchScalarGridSpec(
        num_scalar_prefetch=1, grid=(g, r // tm, nf),
        in_specs=[act(d),
                  pl.BlockSpec((None, d, tf), lambda gi, i, f, s: (s[0], 0, f)),
                  pl.BlockSpec((None, d, tf), lambda gi, i, f, s: (s[0], 0, f)),
                  pl.BlockSpec((None, tf, d), lambda gi, i, f, s: (s[0], f, 0)),
                  act(d), g2, pl.BlockSpec((1, d), lambda gi, i, f, s: (0, 0))],
        out_specs=act(d),
        scratch_shapes=[pltpu.VMEM((tm, d), F32)])
    sidx = jnp.concatenate([iidx, lidx_mod])
    return pl.pallas_call(
        functools.partial(_ffn_kernel, final=final),
        out_shape=jax.ShapeDtypeStruct((g, r, d), F32),
        grid_spec=grid_spec,
        compiler_params=_cparams(("arbitrary", "arbitrary", "arbitrary")),
        name="dense_ffn",
    )(sidx, h2, wg, wu, wd, x1, mod, final_g)


def _route_kernel(i_ref, h_ref, rw_ref, tri_ref, idx_ref, gate_ref, cnt_ref, carry_ref):
    del i_ref
    i = pl.program_id(0)

    @pl.when(i == 0)
    def _():
        carry_ref[...] = jnp.zeros(carry_ref.shape, F32)

    hh, hl = _split2(h_ref[...])
    wh, wl = _split2(rw_ref[...])
    logits = _dotf(hh, wh) + _dotf(hl, wh) + _dotf(hh, wl)
    lane = _iota(logits.shape, 1).astype(F32)
    logits = jnp.where(lane < N_EXPERTS, logits, NEG)
    m1 = jnp.max(logits, axis=1, keepdims=True)
    i1 = jnp.min(jnp.where(logits == m1, lane, 128.0), axis=1, keepdims=True)
    rest = jnp.where(lane == i1, NEG, logits)
    m2 = jnp.max(rest, axis=1, keepdims=True)
    i2 = jnp.min(jnp.where(rest == m2, lane, 128.0), axis=1, keepdims=True)
    e = jnp.exp(m2 - m1)
    g1 = 1.0 / (1.0 + e)
    sel = jnp.where(lane == i1, 1.0, 0.0) + jnp.where(lane == i2, 1.0, 0.0)
    before = _dotf(tri_ref[...], sel.astype(BF16)) + carry_ref[...]
    r1 = jnp.sum(jnp.where(lane == i1, before, 0.0), axis=1, keepdims=True)
    r2 = jnp.sum(jnp.where(lane == i2, before, 0.0), axis=1, keepdims=True)
    carry_ref[...] = carry_ref[...] + jnp.sum(sel, axis=0, keepdims=True)
    lane8 = _iota(idx_ref.shape, 1).astype(F32)
    idx_ref[...] = jnp.where(lane8 == 0.0, i1, jnp.where(lane8 == 1.0, i2, jnp.where(lane8 == 2.0, r1, r2))).astype(I32)
    gate_ref[...] = jnp.where(lane8 == 0.0, g1, e * g1)
    cnt_ref[...] = jnp.broadcast_to(carry_ref[...], cnt_ref.shape)


def moe_route(iidx, h2, router_pad, tm):
    t, d = h2.shape
    tri = jnp.asarray(np.tril(np.ones((tm, tm), np.float32), -1), BF16)
    grid_spec = pltpu.PrefetchScalarGridSpec(
        num_scalar_prefetch=1, grid=(t // tm,),
        in_specs=[pl.BlockSpec((tm, d), lambda i, s: (i, 0)),
                  pl.BlockSpec((None, d, 128), lambda i, s: (s[0], 0, 0)),
                  pl.BlockSpec((tm, tm), lambda i, s: (0, 0))],
        out_specs=[pl.BlockSpec((tm, 8), lambda i, s: (i, 0)), pl.BlockSpec((tm, 8), lambda i, s: (i, 0)),
                   pl.BlockSpec((8, 128), lambda i, s: (0, 0))],
        scratch_shapes=[pltpu.VMEM((1, 128), F32)])
    return pl.pallas_call(
        _route_kernel,
        out_shape=[jax.ShapeDtypeStruct((t, 8), I32), jax.ShapeDtypeStruct((t, 8), F32),
                   jax.ShapeDtypeStruct((8, 128), F32)],
        grid_spec=grid_spec,
        compiler_params=_cparams(("arbitrary",)),
        name="moe_route",
    )(iidx, h2, router_pad, tri)


def _row_copy(src_ref, src_row, dst_ref, dst_row, sem):
    return pltpu.make_async_copy(src_ref.at[pl.ds(src_row, 1), :], dst_ref.at[pl.ds(dst_row, 1), :], sem)


def _dispatch_kernel(dest_ref, h_ref, xb_in_ref, xb_ref, sem):
    del xb_in_ref
    tm = h_ref.shape[0]

    def start(r, carry):
        for k in range(TOP_K):
            _row_copy(h_ref, r, xb_ref, dest_ref[0, TOP_K * r + k], sem).start()
        return carry

    def wait(r, carry):
        for k in range(TOP_K):
            _row_copy(h_ref, r, xb_ref, dest_ref[0, TOP_K * r + k], sem).wait()
        return carry

    lax.fori_loop(0, tm, start, 0)
    lax.fori_loop(0, tm, wait, 0)


def moe_dispatch(dest3, h2, n_rows, tm):
    t, d = h2.shape
    zeros = jnp.zeros((n_rows, d), F32)
    return pl.pallas_call(
        _dispatch_kernel,
        out_shape=jax.ShapeDtypeStruct((n_rows, d), F32),
        grid=(t // tm,),
        in_specs=[pl.BlockSpec((None, 1, TOP_K * tm), lambda i: (i, 0, 0), memory_space=pltpu.SMEM),
                  pl.BlockSpec((tm, d), lambda i: (i, 0)),
                  pl.BlockSpec(memory_space=pl.ANY)],
        out_specs=pl.BlockSpec(memory_space=pl.ANY),
        scratch_shapes=[pltpu.SemaphoreType.DMA(())],
        input_output_aliases={2: 0},
        compiler_params=pltpu.CompilerParams(dimension_semantics=("arbitrary",), vmem_limit_bytes=VMEM_LIMIT,
                                             has_side_effects=True),
        name="moe_dispatch",
    )(dest3, h2, zeros)


def _expert_kernel(s_ref, x_ref, wg_ref, wu_ref, wd_ref, y_ref, xb_ref, acc_ref):
    i = pl.program_id(0)
    f = pl.program_id(1)
    nb = pl.num_programs(0)

    @pl.when(i < s_ref[nb + 1])
    def _():
        @pl.when(f == 0)
        def _():
            xb_ref[...] = x_ref[...].astype(BF16)
            acc_ref[...] = jnp.zeros(acc_ref.shape, F32)

        x = xb_ref[...]
        a = (_silu(_dotf(x, wg_ref[...])) * _dotf(x, wu_ref[...])).astype(BF16)
        acc_ref[...] += _dotf(a, wd_ref[...])

        @pl.when(f == pl.num_programs(1) - 1)
        def _():
            y_ref[...] = acc_ref[...]

    @pl.when(jnp.logical_and(i >= s_ref[nb + 1], f == pl.num_programs(1) - 1))
    def _():
        y_ref[...] = jnp.zeros(y_ref.shape, F32)


def moe_experts(sidx, xb, wg, wu, wd, blk, tf):
    n_rows, d = xb.shape
    nb = n_rows // blk
    nf = D_FF // tf

    def row_map(i, f, s):
        return (jnp.minimum(i, s[nb + 1] - 1), 0)

    def f_of(i, f, s):
        return jnp.where(i < s[nb + 1], f, nf - 1)

    grid_spec = pltpu.PrefetchScalarGridSpec(
        num_scalar_prefetch=1, grid=(nb, nf),
        in_specs=[pl.BlockSpec((blk, d), row_map),
                  pl.BlockSpec((None, None, d, tf), lambda i, f, s: (s[nb], s[i], 0, f_of(i, f, s))),
                  pl.BlockSpec((None, None, d, tf), lambda i, f, s: (s[nb], s[i], 0, f_of(i, f, s))),
                  pl.BlockSpec((None, None, tf, d), lambda i, f, s: (s[nb], s[i], f_of(i, f, s), 0))],
        out_specs=pl.BlockSpec((blk, d), lambda i, f, s: (i, 0)),
        scratch_shapes=[pltpu.VMEM((blk, d), BF16), pltpu.VMEM((blk, d), F32)])
    return pl.pallas_call(
        _expert_kernel,
        out_shape=jax.ShapeDtypeStruct((n_rows, d), F32),
        grid_spec=grid_spec,
        compiler_params=_cparams(("arbitrary", "arbitrary")),
        name="moe_experts",
    )(sidx, xb, wg, wu, wd)


def _combine_kernel(dest_ref, yb_ref, gate_ref, x1_ref, g2_ref, fg_ref, o_ref, buf_ref, sem, *, final):
    tm = x1_ref.shape[0]

    def start(r, carry):
        for k in range(TOP_K):
            _row_copy(yb_ref, dest_ref[0, TOP_K * r + k], buf_ref.at[k], r, sem).start()
        return carry

    def wait(r, carry):
        for k in range(TOP_K):
            _row_copy(yb_ref, dest_ref[0, TOP_K * r + k], buf_ref.at[k], r, sem).wait()
        return carry

    lax.fori_loop(0, tm, start, 0)
    lax.fori_loop(0, tm, wait, 0)
    gate = gate_ref[...]
    f = gate[:, 0:1] * buf_ref[0] + gate[:, 1:2] * buf_ref[1]
    x2 = x1_ref[...] + g2_ref[...] * f
    o_ref[...] = _rms(x2) * fg_ref[...] if final else x2


def moe_combine(dest3, yb, gates, x1, mod, lidx_mod, final_g, tm, final):
    g, r, d = x1.shape
    per_row = mod.shape[1] > 1
    nt = r // tm
    del lidx_mod
    act = pl.BlockSpec((None, tm, d), lambda gi, i: (gi, i, 0))
    return pl.pallas_call(
        functools.partial(_combine_kernel, final=final),
        out_shape=jax.ShapeDtypeStruct((g, r, d), F32),
        grid=(g, nt),
        in_specs=[pl.BlockSpec((None, 1, TOP_K * tm), lambda gi, i: (gi * nt + i, 0, 0), memory_space=pltpu.SMEM),
                  pl.BlockSpec(memory_space=pl.ANY),
                  pl.BlockSpec((tm, 8), lambda gi, i: (gi * nt + i, 0)),
                  act,
                  (pl.BlockSpec((None, tm, d), lambda gi, i: (gi, i, 0)) if per_row
                   else pl.BlockSpec((None, 1, d), lambda gi, i: (gi, 0, 0))),
                  pl.BlockSpec((1, d), lambda gi, i: (0, 0))],
        out_specs=act,
        scratch_shapes=[pltpu.VMEM((TOP_K, tm, d), F32), pltpu.SemaphoreType.DMA(())],
        compiler_params=_cparams(("arbitrary", "arbitrary")),
        name="moe_combine",
    )(dest3, yb, gates, x1, mod, final_g)


def moe_ffn(layer_slot, h2, x1, mod_g2, router_pad, wg, wu, wd, final_g, tm, blk, tf, final):
    g, r, d = x1.shape
    t = g * r
    h2f = h2.reshape(t, d)
    iidx = jnp.array([layer_slot], I32)
    idx, gates, counts = moe_route(iidx, h2f, router_pad, tm)
    counts = counts[0, :N_EXPERTS].astype(I32)
    padded = (counts + blk - 1) // blk * blk
    pad_end = jnp.cumsum(padded)
    pad_start = pad_end - padded
    dest = pad_start[idx[:, 0:TOP_K]] + idx[:, TOP_K:2 * TOP_K]
    nb = -(-(t * TOP_K) // blk) + N_EXPERTS
    n_used = pad_end[-1] // blk
    blk_e = jnp.minimum(jnp.searchsorted(pad_end, jnp.arange(nb, dtype=I32) * blk, side='right'), N_EXPERTS - 1)
    blk_e = jnp.where(jnp.arange(nb) < n_used, blk_e, blk_e[jnp.maximum(n_used - 1, 0)]).astype(I32)
    dest3 = dest.astype(I32).reshape(t // tm, 1, TOP_K * tm)
    xb = moe_dispatch(dest3, h2f, nb * blk, tm)
    sidx = jnp.concatenate([blk_e, iidx, n_used.astype(I32)[None]])
    yb = moe_experts(sidx, xb, wg, wu, wd, blk, tf)
    return moe_combine(dest3, yb, gates, x1, mod_g2, None, final_g, tm, final)


def _attn_decode_kernel(pt_ref, lam_ref, lqk_ref, g_ref, q_ref, kn_ref, vn_ref, tab_ref, *refs, pps, n_pages):
    k_refs = refs[:pps]
    v_refs = refs[pps:2 * pps]
    o_ref, m_ref, s_ref, acc_ref = refs[2 * pps:]
    step = pl.program_id(1)
    layer = pt_ref[pt_ref.shape[0] - 1]
    rows = 2 * H_A

    @pl.when(step == 0)
    def _():
        m_ref[...] = jnp.full(m_ref.shape, NEG, F32)
        s_ref[...] = jnp.zeros(s_ref.shape, F32)
        acc_ref[...] = jnp.zeros(acc_ref.shape, F32)

    qrow = jnp.broadcast_to(q_ref[...].astype(F32), (rows, W_A))
    own = (_iota((rows, W_A), 1) >> 6) == _iota((rows, W_A), 0)
    q8 = jnp.where(own, qrow, 0.0).astype(BF16)

    def accumulate(s, pv):
        m_old = m_ref[...]
        m_new = jnp.maximum(m_old, jnp.max(s, axis=1, keepdims=True))
        alpha = jnp.exp(m_old - m_new)
        p = jnp.exp(s - m_new)
        s_ref[...] = alpha * s_ref[...] + jnp.sum(p, axis=1, keepdims=True)
        acc_ref[...] = alpha * acc_ref[...] + pv(p)
        m_ref[...] = m_new

    scores = [lax.dot_general(q8, k_refs[i][...].astype(BF16), (((1,), (1,)), ((), ())),
                              preferred_element_type=F32) for i in range(pps)]
    scores[pps - 1] = jnp.where(step == pl.num_programs(1) - 1, scores[pps - 1] + tab_ref[0:rows, :], scores[pps - 1])
    s_all = jnp.concatenate(scores, axis=1)

    def pv_pages(p):
        out = _dotf(p[:, 0:PAGE].astype(BF16), v_refs[0][...].astype(BF16))
        for i in range(1, pps):
            out = out + _dotf(p[:, i * PAGE:(i + 1) * PAGE].astype(BF16), v_refs[i][...].astype(BF16))
        return out

    accumulate(s_all, pv_pages)

    @pl.when(step == pl.num_programs(1) - 1)
    def _():
        kn = kn_ref[...].astype(BF16).astype(F32)
        vn = vn_ref[...].astype(BF16).astype(F32)
        s_new = jnp.sum(q8.astype(F32) * kn, axis=1, keepdims=True) + tab_ref[rows:2 * rows, 0:1]
        accumulate(s_new, lambda p: p.astype(BF16).astype(F32) * vn)
        o = acc_ref[...] * (1.0 / s_ref[...])
        lam = _lambda_value(lqk_ref[...], lam_ref[2 * layer])
        outs = []
        for h in range(H_A):
            blkh = o[:, h * DV_A:(h + 1) * DV_A]
            oh = blkh[2 * h:2 * h + 1, :] - lam * blkh[2 * h + 1:2 * h + 2, :]
            outs.append(_rms(oh) * g_ref[...] * lam_ref[2 * layer + 1])
        o_ref[...] = jnp.concatenate(outs, axis=1).astype(BF16)


def attention_decode(sidx, lam_consts, lambda_qk, subln_g3, q, k_new, v_new, tab, cache_k4, cache_v4, n_pages, pps):
    bd = q.shape[0]
    nl = sidx.shape[0] - 1
    steps = n_pages // pps
    row = pl.BlockSpec((None, 1, W_A), lambda b, s, pt: (b, 0, 0))

    def page_spec(i):
        return pl.BlockSpec((None, None, PAGE, W_A),
                            lambda b, s, pt: (pt[nl], pt[b * n_pages + s * pps + i], 0, 0))

    grid_spec = pltpu.PrefetchScalarGridSpec(
        num_scalar_prefetch=1, grid=(bd, steps),
        in_specs=[pl.BlockSpec(memory_space=pltpu.SMEM),
                  pl.BlockSpec((None, 4, DK_A), lambda b, s, pt: (pt[nl], 0, 0)),
                  pl.BlockSpec((None, 1, DV_A), lambda b, s, pt: (pt[nl], 0, 0)),
                  row, row, row,
                  pl.BlockSpec((16, PAGE), lambda b, s, pt: (0, 0))]
                 + [page_spec(i) for i in range(pps)] + [page_spec(i) for i in range(pps)],
        out_specs=row,
        scratch_shapes=[pltpu.VMEM((2 * H_A, 1), F32), pltpu.VMEM((2 * H_A, 1), F32),
                        pltpu.VMEM((2 * H_A, W_A), F32)])
    return pl.pallas_call(
        functools.partial(_attn_decode_kernel, pps=pps, n_pages=n_pages),
        out_shape=jax.ShapeDtypeStruct((bd, 1, W_A), BF16),
        grid_spec=grid_spec,
        compiler_params=_cparams(("arbitrary", "arbitrary")),
        name="attention_decode",
    )(sidx, lam_consts, lambda_qk, subln_g3, q, k_new, v_new, tab, *([cache_k4] * pps), *([cache_v4] * pps))


def _bc_decode_pre_kernel(l_ref, rest_ref, buf_ref, convw_ref, convb_ref, v128_ref, wlr_ref,
                          xbc_ref, nbuf_ref, dt_ref, dec_ref, eg_ref):
    del l_ref
    u = rest_ref[:, R_BX:R_BX + XBC_B]
    conv = convb_ref[...] + convw_ref[CONV_W - 1:CONV_W, :] * u
    for j in range(CONV_W - 1):
        conv = conv + convw_ref[j:j + 1, :] * buf_ref[j]
    xbc_ref[...] = _silu(conv)
    for j in range(CONV_W - 2):
        nbuf_ref[j] = buf_ref[j + 1]
    nbuf_ref[CONV_W - 2] = u
    misc = rest_ref[:, R_MISC:R_MISC + 128]
    dt = _softplus(misc + v128_ref[0:1, :])
    dt_ref[...] = dt
    dec_ref[...] = jnp.exp(dt * (-jnp.exp(v128_ref[1:2, :])))
    gk = _dotf(misc.astype(BF16), wlr_ref[...]) + v128_ref[2:3, :]
    eg_ref[...] = jnp.exp(-_softplus(-gk) * (1.0 / GATE_NORM))


def bc_decode_pre(lidx, rest, conv_state_t, convw, convb3, vec128, wlr):
    bd = rest.shape[0]
    par = lambda r, w: pl.BlockSpec((None, r, w), lambda i, l: (l[0], 0, 0))
    full = lambda w: pl.BlockSpec((bd, w), lambda i, l: (0, 0))
    grid_spec = pltpu.PrefetchScalarGridSpec(
        num_scalar_prefetch=1, grid=(1,),
        in_specs=[full(W_REST),
                  pl.BlockSpec((None, CONV_W - 1, bd, XBC_B), lambda i, l: (l[0], 0, 0, 0)),
                  par(CONV_W, XBC_B), par(1, XBC_B), par(8, 128), par(128, 128)],
        out_specs=[full(XBC_B), pl.BlockSpec((CONV_W - 1, bd, XBC_B), lambda i, l: (0, 0, 0)),
                   full(128), full(128), full(128)])
    return pl.pallas_call(
        _bc_decode_pre_kernel,
        out_shape=[jax.ShapeDtypeStruct((bd, XBC_B), F32), jax.ShapeDtypeStruct((CONV_W - 1, bd, XBC_B), F32),
                   jax.ShapeDtypeStruct((bd, 128), F32), jax.ShapeDtypeStruct((bd, 128), F32),
                   jax.ShapeDtypeStruct((bd, 128), F32)],
        grid_spec=grid_spec,
        compiler_params=_cparams(("arbitrary",)),
        name="bc_decode_pre",
    )(lidx, rest, conv_state_t, convw, convb3, vec128, wlr)


def _bc_decode_state_kernel(l_ref, ssm_ref, gla_ref, x4_ref, b4_ref, c4_ref, dt4_ref, dec4_ref, bz4_ref,
                            dsk_ref, ng_ref, q4_ref, k4_ref, v4_ref, eg4_ref, cg4_ref, gng_ref,
                            ssm_out_ref, gla_out_ref, ob4_ref, oc4_ref):
    del l_ref
    x4 = x4_ref[...]
    s = ssm_ref[...] * dec4_ref[...] + (dt4_ref[...] * x4) * b4_ref[...]
    ssm_out_ref[...] = s
    y = jnp.sum(c4_ref[...] * s, axis=3, keepdims=True) + dsk_ref[...] * x4
    yg = y * _silu(bz4_ref[...])
    ms = jnp.sum(jnp.sum(yg * yg, axis=2, keepdims=True), axis=1, keepdims=True) * (1.0 / W_B)
    ob4_ref[...] = yg * lax.rsqrt(ms + EPS) * ng_ref[...]
    sg = gla_ref[...] * eg4_ref[...] + k4_ref[...] * v4_ref[...]
    gla_out_ref[...] = sg
    o = jnp.sum(q4_ref[...] * sg, axis=2, keepdims=True)
    oc4_ref[...] = _rms(o) * gng_ref[...] * _silu(cg4_ref[...])


def bc_decode_state(lidx, state_ssm, state_gla, x4, b4, c4, dt4, dec4, bz4, dsk4, ng4, q4, k4, v4, eg4, cg4, gng4):
    bd = x4.shape[0]

    def full(shape):
        n = len(shape)
        return pl.BlockSpec(shape, lambda i, l: (0,) * n)

    def layer(shape):
        n = len(shape)
        return pl.BlockSpec((None,) + shape, lambda i, l: (l[0],) + (0,) * n)

    s_ssm = (bd, H_B, P_B, N_B)
    s_gla = (bd, H_C, DK_C, DV_C)
    col_b = (bd, H_B, P_B, 1)
    row_b = (bd, H_B, 1, N_B)
    one_b = (bd, H_B, 1, 1)
    col_c = (bd, H_C, DK_C, 1)
    row_c = (bd, H_C, 1, DV_C)
    grid_spec = pltpu.PrefetchScalarGridSpec(
        num_scalar_prefetch=1, grid=(1,),
        in_specs=[layer(s_ssm), layer(s_gla), full(col_b), full(row_b), full(row_b), full(one_b), full(one_b),
                  full(col_b), layer((1, H_B, 1, 1)), layer((1, H_B, P_B, 1)),
                  full(col_c), full(col_c), full(row_c), full(col_c), full(row_c), layer((1, 1, 1, DV_C))],
        out_specs=[full(s_ssm), full(s_gla), full(col_b), full(row_c)])
    return pl.pallas_call(
        _bc_decode_state_kernel,
        out_shape=[jax.ShapeDtypeStruct(s_ssm, F32), jax.ShapeDtypeStruct(s_gla, F32),
                   jax.ShapeDtypeStruct(col_b, F32), jax.ShapeDtypeStruct(row_c, F32)],
        grid_spec=grid_spec,
        compiler_params=_cparams(("arbitrary",)),
        name="bc_decode_state",
    )(lidx, state_ssm, state_gla, x4, b4, c4, dt4, dec4, bz4, dsk4, ng4, q4, k4, v4, eg4, cg4, gng4)


def _pack_params(w_in, rel_bias, subln_g, conv_b, dt_bias, a_log, d_skip, ssd_norm_g, gla_w_lr, gla_b_lr,
                 gla_norm_g, router_w):
    depth = w_in.shape[0]
    pad = jnp.zeros(w_in.shape[:2] + (D_PACK - D_IN,), w_in.dtype)
    w_pack = jnp.concatenate([w_in[..., :OFF_BDT], w_in[..., OFF_CQ:OFF_CLR], w_in[..., OFF_BDT:OFF_CQ],
                              w_in[..., OFF_CLR:], pad], axis=-1).astype(BF16)
    vec128 = jnp.zeros((depth, 8, 128), F32)
    vec128 = vec128.at[:, 0, :H_B].set(dt_bias).at[:, 1, :H_B].set(a_log).at[:, 2, :].set(gla_b_lr)
    vec256 = jnp.zeros((depth, 8, 256), F32)
    vec256 = (vec256.at[:, 0, :].set(jnp.repeat(d_skip, P_B, axis=1)).at[:, 1, :].set(ssd_norm_g)
              .at[:, 2, :].set(jnp.tile(gla_norm_g, (1, H_C))))
    wlr = jnp.zeros((depth, 128, 128), F32).at[:, MISC_LR:MISC_LR + GATE_RANK, :].set(gla_w_lr).astype(BF16)
    router_pad = jnp.zeros(router_w.shape[:2] + (128,), F32).at[..., :N_EXPERTS].set(router_w)
    lam_init = [0.8 - 0.6 * math.exp(-0.3 * l) for l in range(depth)]
    lam_consts = jnp.asarray(np.array([[li, 1.0 - li] for li in lam_init], np.float32).reshape(-1))
    return dict(w_pack=w_pack, vec128=vec128, vec256=vec256, wlr=wlr, router_pad=router_pad,
                lam_consts=lam_consts, subln_g3=subln_g.reshape(depth, 1, DV_A),
                convb3=conv_b.reshape(depth, 1, XBC_B))


def _pick(n, pref):
    if n <= pref:
        return n
    t = pref
    while n % t:
        t //= 2
    return t


def _channel_mixer(l, depth, h2, x1, mod, pk, wts, tm, moe_blk, final):
    i = l // 2
    lmod = jnp.array([l], I32)
    if l % 2 == 0:
        return dense_ffn(jnp.array([i], I32), lmod, h2, wts['ffn_g'], wts['ffn_u'], wts['ffn_d'], x1, mod,
                         wts['final_g'], tm, _pick(D_FF, 1408), final)
    mod_g2 = mod[l, :, :, 5 * D_MODEL:6 * D_MODEL]
    return moe_ffn(i, h2, x1, mod_g2, pk['router_pad'], wts['moe_g'], wts['moe_u'], wts['moe_d'], wts['final_g'],
                   _pick(x1.shape[0] * x1.shape[1], 256), moe_blk, _pick(D_FF, 1408), final)


def _run_prompt(x, mod, pk, wts, lambda_qk, conv_w, rel_bias):
    b, seq, d = x.shape
    depth = mod.shape[0]
    tm = _pick(seq, 512)
    blk = _pick(seq, 256)
    table = _t5_table(rel_bias, 2 * blk)
    qpos = jnp.arange(blk)[:, None]
    kpos = jnp.arange(blk)[None, :]
    diag = jnp.where(qpos >= kpos, table[:, jnp.maximum(qpos - kpos, 0)], NEG)
    sub = table[:, blk + qpos - kpos]
    bias_tiles = jnp.stack([diag, sub], axis=1)
    ks, vs, convs, ssms, glas = [], [], [], [], []
    for l in range(depth):
        lidx = jnp.array([l], I32)
        q, k, kb, v, vb, rest = in_projection(lidx, x, mod, pk['w_pack'], tm)
        oa = attention_prompt(lidx, pk['lam_consts'], lambda_qk, pk['subln_g3'], q, kb, vb, bias_tiles, blk)
        ob, oc, conv_s, ssm_s, gla_s = bc_mixers_prompt(lidx, rest, conv_w, pk['convb3'], pk['vec128'],
                                                        pk['vec256'], pk['wlr'])
        moe = l % 2 == 1
        x1, h2 = out_projection(lidx, oa, ob, oc, wts['w_out'], x, mod, tm, F32 if moe else BF16)
        x = _channel_mixer(l, depth, h2, x1, mod, pk, wts, tm, _pick(b * seq, 512), l == depth - 1)
        ks.append(k.reshape(b, seq, 2 * H_A, DK_A))
        vs.append(v.reshape(b, seq, H_A, DV_A))
        convs.append(conv_s)
        ssms.append(_unpack_ssm_state(ssm_s))
        glas.append(_unpack_gla_state(gla_s))
    return x, jnp.stack(ks), jnp.stack(vs), jnp.stack(convs), jnp.stack(ssms), jnp.stack(glas)


def _run_decode(x, mod, pk, wts, lambda_qk, conv_w, rel_bias, cache_k, cache_v, page_table,
                state_conv, state_ssm, state_gla, d_skip, ssd_norm_g, gla_norm_g):
    bd = x.shape[0]
    depth = mod.shape[0]
    n_pages = page_table.shape[1]
    past = n_pages * PAGE
    pps = _pick(n_pages, 8)
    xg = x.reshape(1, bd, D_MODEL)
    pool = cache_k.shape[1]
    cache_k4 = cache_k.reshape(depth, pool, PAGE, 2 * H_A * DK_A)
    cache_v4 = cache_v.reshape(depth, pool, PAGE, H_A * DV_A)
    table = _t5_table(rel_bias, PAGE + 1)
    dist = past - ((n_pages - 1) * PAGE + jnp.arange(PAGE))
    tab = jnp.concatenate([jnp.repeat(table[:, dist], 2, axis=0),
                           jnp.broadcast_to(jnp.repeat(table[:, 0], 2)[:, None], (2 * H_A, PAGE))], axis=0)
    conv_t = jnp.swapaxes(state_conv, 1, 2)
    grp = np.arange(H_B) // (H_B // G_B)
    dsk4 = d_skip.reshape(depth, 1, H_B, 1, 1)
    ng4 = ssd_norm_g.reshape(depth, 1, H_B, P_B, 1)
    gng4 = gla_norm_g.reshape(depth, 1, 1, 1, DV_C)
    pt_flat = page_table.reshape(-1).astype(I32)
    ks, vs, convs, ssms, glas = [], [], [], [], []
    for l in range(depth):
        lidx = jnp.array([l], I32)
        q, k, _, v, _, rest = in_projection(lidx, xg, mod, pk['w_pack'], bd)
        sidx = jnp.concatenate([pt_flat, lidx])
        oa = attention_decode(sidx, pk['lam_consts'], lambda_qk, pk['subln_g3'], q.reshape(bd, 1, W_A),
                              k.reshape(bd, 1, W_A), v.reshape(bd, 1, W_A), tab, cache_k4, cache_v4, n_pages, pps)
        rest2 = rest.reshape(bd, W_REST)
        xbc, nbuf, dt, dec, eg = bc_decode_pre(lidx, rest2, conv_t, conv_w, pk['convb3'], pk['vec128'], pk['wlr'])
        x4 = xbc[:, :W_B].reshape(bd, H_B, P_B, 1)
        b4 = xbc[:, W_B:W_B + G_B * N_B].reshape(bd, G_B, 1, N_B)[:, grp]
        c4 = xbc[:, W_B + G_B * N_B:].reshape(bd, G_B, 1, N_B)[:, grp]
        dt4 = dt[:, :H_B].reshape(bd, H_B, 1, 1)
        dec4 = dec[:, :H_B].reshape(bd, H_B, 1, 1)
        bz4 = rest2[:, R_BZ:R_BZ + W_B].reshape(bd, H_B, P_B, 1)
        q4 = (rest2[:, R_CQ:R_CQ + 128] * (DK_C ** -0.5)).reshape(bd, H_C, DK_C, 1)
        k4 = rest2[:, R_CK:R_CK + 128].reshape(bd, H_C, DK_C, 1)
        v4 = rest2[:, R_CV:R_CV + W_C].reshape(bd, H_C, 1, DV_C)
        eg4 = eg.reshape(bd, H_C, DK_C, 1)
        cg4 = rest2[:, R_CG:R_CG + W_C].reshape(bd, H_C, 1, DV_C)
        ssm_s, gla_s, ob4, oc4 = bc_decode_state(lidx, state_ssm, state_gla, x4, b4, c4, dt4, dec4, bz4, dsk4, ng4,
                                                 q4, k4, v4, eg4, cg4, gng4)
        ob = ob4.reshape(1, bd, W_B).astype(BF16)
        oc = oc4.reshape(1, bd, W_C).astype(BF16)
        moe = l % 2 == 1
        x1, h2 = out_projection(lidx, oa.reshape(1, bd, W_A), ob, oc, wts['w_out'], xg, mod, bd, F32 if moe else BF16)
        xg = _channel_mixer(l, depth, h2, x1, mod, pk, wts, bd, 64, l == depth - 1)
        ks.append(k.reshape(bd, 1, 2 * H_A, DK_A))
        vs.append(v.reshape(bd, 1, H_A, DV_A))
        convs.append(jnp.swapaxes(nbuf, 0, 1))
        ssms.append(ssm_s)
        glas.append(gla_s)
    return (xg.reshape(bd, 1, D_MODEL), jnp.stack(ks), jnp.stack(vs), jnp.stack(convs), jnp.stack(ssms),
            jnp.stack(glas))


def kernel(x_prompt, x_sample, c_prompt, c_sample, cache_k, cache_v, page_table, state_conv, state_ssm, state_gla,
           w_ada, b_ada, w_in, w_out, rel_bias, lambda_qk, subln_g, conv_w, conv_b, dt_bias, a_log, d_skip,
           ssd_norm_g, gla_w_lr, gla_b_lr, gla_norm_g, ffn_w_gate, ffn_w_up, ffn_w_down, router_w,
           moe_w_gate, moe_w_up, moe_w_down, final_norm_g):
    depth = w_in.shape[0]
    bp, bd = c_prompt.shape[0], c_sample.shape[0]
    pk = _pack_params(w_in, rel_bias, subln_g, conv_b, dt_bias, a_log, d_skip, ssd_norm_g, gla_w_lr, gla_b_lr,
                      gla_norm_g, router_w)
    wts = dict(w_out=w_out.astype(BF16), ffn_g=ffn_w_gate.astype(BF16), ffn_u=ffn_w_up.astype(BF16),
               ffn_d=ffn_w_down.astype(BF16), moe_g=moe_w_gate.astype(BF16), moe_u=moe_w_up.astype(BF16),
               moe_d=moe_w_down.astype(BF16), final_g=final_norm_g.reshape(1, D_MODEL))
    n_c = bp + bd
    n_cp = -(-n_c // 8) * 8
    c_all = jnp.concatenate([c_prompt, c_sample, jnp.zeros((n_cp - n_c, D_MODEL), F32)], axis=0)
    mod = ada_modulation(c_all, w_ada, b_ada)
    mod_p = mod[:, :bp].reshape(depth, bp, 1, 6 * D_MODEL)
    mod_d = mod[:, bp:n_c].reshape(depth, 1, bd, 6 * D_MODEL)
    yp, kp, vp, convp, ssmp, glap = _run_prompt(x_prompt, mod_p, pk, wts, lambda_qk, conv_w, rel_bias)
    yd, kd, vd, convd, ssmd, glad = _run_decode(x_sample, mod_d, pk, wts, lambda_qk, conv_w, rel_bias, cache_k,
                                                cache_v, page_table, state_conv, state_ssm, state_gla, d_skip,
                                                ssd_norm_g, gla_norm_g)
    return (yp, yd, kp, vp, convp, ssmp, glap, kd, vd, convd, ssmd, glad)
```

```python
import functools
import math

import numpy as np
import jax
import jax.numpy as jnp
from jax import lax
from jax.experimental import pallas as pl
from jax.experimental.pallas import tpu as pltpu

F32 = jnp.float32
BF16 = jnp.bfloat16
I32 = jnp.int32

D_MODEL = 1024
H_A, DK_A, DV_A = 4, 64, 128
W_A = H_A * DV_A
H_B, P_B, G_B, N_B, CONV_W = 4, 64, 2, 64, 4
W_B = H_B * P_B
XBC_B = W_B + 2 * G_B * N_B
H_C, DK_C, DV_C = 4, 32, 64
W_C = H_C * DV_C
GATE_RANK, GATE_NORM = 16, 16.0
T5_BUCKETS, T5_MAX_DIST = 32, 128
D_FF, N_EXPERTS, TOP_K = 2816, 8, 2
EPS = 1e-6
PAGE = 128
CHUNK = 128
OFF_AQ = 0
OFF_AK = OFF_AQ + 2 * H_A * DK_A
OFF_AV = OFF_AK + 2 * H_A * DK_A
OFF_BZ = OFF_AV + W_A
OFF_BX = OFF_BZ + W_B
OFF_BDT = OFF_BX + XBC_B
OFF_CQ = OFF_BDT + H_B
OFF_CK = OFF_CQ + H_C * DK_C
OFF_CV = OFF_CK + H_C * DK_C
OFF_CG = OFF_CV + W_C
OFF_CLR = OFF_CG + W_C
D_IN = OFF_CLR + GATE_RANK
PK_REST = 3 * W_A
R_BZ, R_BX, R_CQ, R_CK, R_CV, R_CG, R_MISC = 0, 256, 768, 896, 1024, 1280, 1536
W_REST = R_MISC + 128
D_PACK = PK_REST + W_REST
MISC_DT, MISC_LR = 0, H_B
NEG = -1e30
VMEM_LIMIT = 56 * 1024 * 1024


def _cparams(sem):
    return pltpu.CompilerParams(dimension_semantics=sem, vmem_limit_bytes=VMEM_LIMIT)


def _rms(x):
    return x * lax.rsqrt(jnp.mean(x * x, axis=-1, keepdims=True) + EPS)


def _silu(x):
    return x * jax.nn.sigmoid(x)


def _softplus(x):
    return jnp.maximum(x, 0.0) + jnp.log1p(jnp.exp(-jnp.abs(x)))


def _split2(x):
    hi = x.astype(BF16)
    return hi, (x - hi.astype(F32)).astype(BF16)


def _split3(x):
    hi = x.astype(BF16)
    r = x - hi.astype(F32)
    mid = r.astype(BF16)
    return hi, mid, (r - mid.astype(F32)).astype(BF16)


def _dotf(a, b):
    return jnp.dot(a, b, preferred_element_type=F32)


def _lmat_exact(mat01, x):
    hi, mid, lo = _split3(x)
    return _dotf(mat01, hi) + _dotf(mat01, mid) + _dotf(mat01, lo)


def _rmat_exact(x, mat01, pieces=3):
    if pieces == 2:
        hi, lo = _split2(x)
        return _dotf(hi, mat01) + _dotf(lo, mat01)
    hi, mid, lo = _split3(x)
    return _dotf(hi, mat01) + _dotf(mid, mat01) + _dotf(lo, mat01)


def _iota(shape, axis):
    return lax.broadcasted_iota(I32, shape, axis)


def _ada_kernel(c_ref, w_ref, b_ref, o_ref):
    c = c_ref[...]
    o_ref[...] = _dotf(_silu(c).astype(BF16), w_ref[...].astype(BF16)) + b_ref[...]


def ada_modulation(c_all, w_ada, b_ada):
    depth, d, n = w_ada.shape
    bc = c_all.shape[0]
    tn = 1536 if n % 1536 == 0 else n
    return pl.pallas_call(
        _ada_kernel,
        out_shape=jax.ShapeDtypeStruct((depth, bc, n), F32),
        grid=(depth, n // tn),
        in_specs=[pl.BlockSpec((bc, d), lambda l, j: (0, 0)),
                  pl.BlockSpec((None, d, tn), lambda l, j: (l, 0, j)),
                  pl.BlockSpec((None, 1, tn), lambda l, j: (l, 0, j))],
        out_specs=pl.BlockSpec((None, bc, tn), lambda l, j: (l, 0, j)),
        compiler_params=_cparams(("arbitrary", "arbitrary")),
        name="ada_modulation",
    )(c_all, w_ada, b_ada.reshape(depth, 1, n))


def _mod_spec(tm, per_row, chunk):
    if per_row:
        return pl.BlockSpec((None, None, tm, D_MODEL), lambda g, i, l: (l[0], g, i, chunk))
    return pl.BlockSpec((None, None, 1, D_MODEL), lambda g, i, l: (l[0], g, 0, chunk))


def _in_proj_kernel(l_ref, x_ref, sh_ref, sc_ref, w_ref, q_ref, k_ref, v_ref, r_ref, *kv_refs):
    del l_ref
    h = (_rms(x_ref[...]) * (1.0 + sc_ref[...]) + sh_ref[...]).astype(BF16)
    q_ref[...] = (_dotf(h, w_ref[:, 0:W_A]) * (DK_A ** -0.5)).astype(BF16)
    k = _dotf(h, w_ref[:, W_A:2 * W_A])
    k_ref[...] = k
    v = _dotf(h, w_ref[:, 2 * W_A:3 * W_A])
    v_ref[...] = v
    r_ref[...] = _dotf(h, w_ref[:, PK_REST:D_PACK])
    if kv_refs:
        kt_ref, vb_ref = kv_refs
        kt_ref[...] = k.T.astype(BF16)
        vb_ref[...] = v.astype(BF16)


def in_projection(lidx, x, mod, w_pack, tm, attn_operands):
    g, r, d = x.shape
    per_row = mod.shape[2] > 1
    row = lambda w, dt: jax.ShapeDtypeStruct((g, r, w), dt)
    ospec = lambda w: pl.BlockSpec((None, tm, w), lambda gi, i, l: (gi, i, 0))
    out_specs = [ospec(W_A), ospec(W_A), ospec(W_A), ospec(W_REST)]
    out_shape = [row(W_A, BF16), row(W_A, F32), row(W_A, F32), row(W_REST, F32)]
    if attn_operands:
        out_specs += [pl.BlockSpec((None, None, W_A, tm), lambda gi, i, l: (gi, i, 0, 0)), ospec(W_A)]
        out_shape += [jax.ShapeDtypeStruct((g, r // tm, W_A, tm), BF16), row(W_A, BF16)]
    grid_spec = pltpu.PrefetchScalarGridSpec(
        num_scalar_prefetch=1, grid=(g, r // tm),
        in_specs=[pl.BlockSpec((None, tm, d), lambda gi, i, l: (gi, i, 0)),
                  _mod_spec(tm, per_row, 0), _mod_spec(tm, per_row, 1),
                  pl.BlockSpec((None, d, D_PACK), lambda gi, i, l: (l[0], 0, 0))],
        out_specs=out_specs)
    return pl.pallas_call(
        _in_proj_kernel,
        out_shape=out_shape,
        grid_spec=grid_spec,
        compiler_params=_cparams(("arbitrary", "arbitrary")),
        name="in_projection",
    )(lidx, x, mod, mod, w_pack)


def _lambda_value(lqk, lam_init):
    s01 = jnp.sum(lqk[0:1, :] * lqk[1:2, :], axis=1, keepdims=True)
    s23 = jnp.sum(lqk[2:3, :] * lqk[3:4, :], axis=1, keepdims=True)
    return jnp.exp(s01) - jnp.exp(s23) + lam_init


def _attn_prompt_kernel(l_ref, lam_ref, lqk_ref, g_ref, q_ref, kt_ref, v_ref, bias_ref, o_ref,
                        vaug_ref, m_ref, acc_ref, *, blk):
    qi = pl.program_id(2)

    @pl.when(qi == 0)
    def _():
        vaug_ref[:, 0:DV_A] = v_ref[...]
        ones_col = _iota((vaug_ref.shape[0], DV_A), 1) == 0
        vaug_ref[:, DV_A:2 * DV_A] = jnp.where(ones_col, 1.0, 0.0).astype(BF16)

    q = q_ref[...]
    lane = _iota(q.shape, 1)
    zero = jnp.zeros_like(q)
    qm = (jnp.where(lane < DK_A, q, zero), jnp.where(lane >= DK_A, q, zero))
    m_ref[...] = jnp.full(m_ref.shape, NEG, F32)
    acc_ref[...] = jnp.zeros(acc_ref.shape, F32)

    def update(j, bias):
        kt = kt_ref[j]
        va = vaug_ref[pl.ds(pl.multiple_of(j * blk, blk), blk), :]
        for m in range(2):
            s = _dotf(qm[m], kt)
            if bias is not None:
                s = s + bias
            m_old = m_ref[m]
            m_new = jnp.maximum(m_old, jnp.broadcast_to(jnp.max(s, axis=1, keepdims=True), m_old.shape))
            p = jnp.exp(s - jnp.tile(m_new, (1, blk // 128))).astype(BF16)
            acc_ref[m] = jnp.tile(jnp.exp(m_old - m_new), (1, 2)) * acc_ref[m] + _dotf(p, va)
            m_ref[m] = m_new

    def far(j, carry):
        update(j, None)
        return carry

    lax.fori_loop(0, jnp.maximum(qi - 1, 0), far, 0)

    @pl.when(qi >= 1)
    def _():
        update(qi - 1, bias_ref[1])

    update(qi, bias_ref[0])

    layer = l_ref[0]
    lam = _lambda_value(lqk_ref[...], lam_ref[2 * layer])
    acc0 = acc_ref[0]
    acc1 = acc_ref[1]
    o = (acc0[:, 0:DV_A] * (1.0 / acc0[:, DV_A:DV_A + 1])
         - lam * (acc1[:, 0:DV_A] * (1.0 / acc1[:, DV_A:DV_A + 1])))
    o_ref[...] = (_rms(o) * g_ref[...] * lam_ref[2 * layer + 1]).astype(BF16)


def _t5_table(rel_bias, n):
    dist = jnp.arange(n)
    max_exact = T5_BUCKETS // 2
    nf = jnp.maximum(dist, 1).astype(F32)
    large = max_exact + (jnp.log(nf / max_exact) / math.log(T5_MAX_DIST / max_exact)
                         * (T5_BUCKETS - max_exact)).astype(I32)
    bucket = jnp.where(dist < max_exact, dist, jnp.minimum(large, T5_BUCKETS - 1))
    return (rel_bias[bucket] - rel_bias[T5_BUCKETS - 1][None, :]).T.astype(F32)


def attention_prompt(lidx, lam_consts, lambda_qk, subln_g3, q, kt, vb, bias_tiles, blk):
    b, seq, _ = q.shape
    assert blk >= T5_MAX_DIST and kt.shape == (b, seq // blk, W_A, blk)
    grid_spec = pltpu.PrefetchScalarGridSpec(
        num_scalar_prefetch=1, grid=(b, H_A, seq // blk),
        in_specs=[pl.BlockSpec(memory_space=pltpu.SMEM),
                  pl.BlockSpec((None, 4, DK_A), lambda bi, h, i, l: (l[0], 0, 0)),
                  pl.BlockSpec((None, 1, DV_A), lambda bi, h, i, l: (l[0], 0, 0)),
                  pl.BlockSpec((None, blk, DV_A), lambda bi, h, i, l: (bi, i, h)),
                  pl.BlockSpec((None, seq // blk, DV_A, blk), lambda bi, h, i, l: (bi, 0, h, 0)),
                  pl.BlockSpec((None, seq, DV_A), lambda bi, h, i, l: (bi, 0, h)),
                  pl.BlockSpec((None, 2, blk, blk), lambda bi, h, i, l: (h, 0, 0, 0))],
        out_specs=pl.BlockSpec((None, blk, DV_A), lambda bi, h, i, l: (bi, i, h)),
        scratch_shapes=[pltpu.VMEM((seq, 2 * DV_A), BF16), pltpu.VMEM((2, blk, 128), F32),
                        pltpu.VMEM((2, blk, 2 * DV_A), F32)])
    return pl.pallas_call(
        functools.partial(_attn_prompt_kernel, blk=blk),
        out_shape=jax.ShapeDtypeStruct((b, seq, W_A), BF16),
        grid_spec=grid_spec,
        compiler_params=_cparams(("arbitrary", "arbitrary", "arbitrary")),
        name="attention_prompt",
    )(lidx, lam_consts, lambda_qk, subln_g3, q, kt, vb, bias_tiles)


def _bc_prompt_kernel(l_ref, rest_ref, convw_ref, convb_ref, v128_ref, v256_ref, wlr_ref,
                      ob_ref, oc_ref, conv_out_ref, ssm_out_ref, gla_out_ref,
                      xpad_ref, sbd_ref, sg_ref, b_ref, oi_ref):
    del l_ref
    c = pl.program_id(1)
    q = CHUNK

    @pl.when(c == 0)
    def _():
        xpad_ref[0:8, :] = jnp.zeros((8, XBC_B), F32)
        sbd_ref[...] = jnp.zeros(sbd_ref.shape, F32)
        sg_ref[...] = jnp.zeros(sg_ref.shape, F32)

    xpad_ref[8:8 + q, :] = rest_ref[:, R_BX:R_BX + XBC_B]
    conv = convb_ref[...]
    for j in range(CONV_W):
        conv = conv + convw_ref[j:j + 1, :] * xpad_ref[5 + j:5 + j + q, :]
    xbc = _silu(conv)
    tail = xpad_ref[5 + q:8 + q, :]
    conv_out_ref[...] = tail
    xpad_ref[5:8, :] = tail

    row = _iota((q, q), 0)
    col = _iota((q, q), 1)
    causal = row >= col
    tril = jnp.where(causal, 1.0, 0.0).astype(BF16)
    lane1 = _iota((1, 128), 1)
    misc = rest_ref[:, R_MISC:R_MISC + 128]

    xs = xbc[:, 0:W_B]
    bmat = xbc[:, W_B:W_B + G_B * N_B].astype(BF16)
    cmat = xbc[:, W_B + G_B * N_B:XBC_B].astype(BF16)
    is_head = lane1 < H_B
    dt = jnp.where(is_head, _softplus(misc + v128_ref[0:1, :]), 0.0)
    a_neg = jnp.where(is_head, -jnp.exp(v128_ref[1:2, :]), 0.0)
    cum = _lmat_exact(tril, dt * a_neg)
    cum_t = cum.T
    ecum = jnp.exp(cum)
    wgt = jnp.exp(cum[q - 1:q, :] - cum) * dt
    expand = jnp.where(_iota((128, W_B), 0) == (_iota((128, W_B), 1) >> 6), 1.0, 0.0).astype(BF16)
    dt_e = _rmat_exact(dt, expand)
    ecum_e = _rmat_exact(ecum, expand)
    wgt_e = _rmat_exact(wgt, expand)
    grp_lane = _iota((q, 128), 1) >> 6
    zero_b = jnp.zeros_like(cmat)
    gmat = [lax.dot_general(jnp.where(grp_lane == g, cmat, zero_b), bmat, (((1,), (1,)), ((), ())),
                            preferred_element_type=F32) for g in range(G_B)]
    mcat = []
    for h in range(H_B):
        diff = cum[:, h:h + 1] - cum_t[h:h + 1, :]
        dec = jnp.exp(jnp.where(causal, diff, NEG))
        mcat.append((gmat[h // (H_B // G_B)] * dec).astype(BF16))
    mcat = jnp.concatenate(mcat, axis=1)
    head_lane = _iota((q, W_B), 1) >> 6
    dtx = xs * dt_e
    xbd = jnp.concatenate([jnp.where(head_lane == h, dtx, 0.0) for h in range(H_B)], axis=0).astype(BF16)
    state = sbd_ref[...]
    y = _dotf(mcat, xbd) + _dotf(cmat, state.astype(BF16)) * ecum_e
    upd = lax.dot_general(bmat, (xs * wgt_e).astype(BF16), (((0,), (0,)), ((), ())), preferred_element_type=F32)
    same_grp = (_iota((128, W_B), 0) >> 6) == (_iota((128, W_B), 1) >> 7)
    state = state * ecum_e[q - 1:q, :] + jnp.where(same_grp, upd, 0.0)
    sbd_ref[...] = state
    ssm_out_ref[...] = state
    y = y + v256_ref[0:1, :] * xs
    ob_ref[...] = (_rms(y * _silu(rest_ref[:, R_BZ:R_BZ + W_B])) * v256_ref[1:2, :]).astype(BF16)

    qg = rest_ref[:, R_CQ:R_CQ + 128] * (DK_C ** -0.5)
    kg = rest_ref[:, R_CK:R_CK + 128]
    vg = rest_ref[:, R_CV:R_CV + W_C]
    gk = _dotf(misc.astype(BF16), wlr_ref[...]) + v128_ref[2:3, :]
    gate = -_softplus(-gk) * (1.0 / GATE_NORM)
    bcum = _lmat_exact(tril, gate)
    b_last = bcum[q - 1:q, :]
    span = jnp.max(-b_last)
    qe = qg * jnp.exp(bcum)
    vhead = _iota((q, W_C), 1) >> 6

    @pl.when(span <= 80.0)
    def _():
        ke = kg * jnp.exp(-bcum)
        kbd = jnp.where((_iota((128, 4 * q), 0) >> 5) == (_iota((128, 4 * q), 1) >> 7),
                        jnp.tile(ke.T, (1, H_C)), 0.0).astype(BF16)
        att = _dotf(qe.astype(BF16), kbd)
        causal4 = _iota((q, H_C * q), 0) >= (_iota((q, H_C * q), 1) & (q - 1))
        att = jnp.where(causal4, att, 0.0).astype(BF16)
        vbd = jnp.concatenate([jnp.where(vhead == h, vg, 0.0) for h in range(H_C)], axis=0).astype(BF16)
        oi_ref[...] = _dotf(att, vbd)

    @pl.when(span > 80.0)
    def _():
        b_ref[...] = bcum
        ind = jnp.where((_iota((128, W_C), 0) >> 5) == (_iota((128, W_C), 1) >> 6), 1.0, 0.0).astype(BF16)
        trow = _iota((q, 128), 0)

        def body(grp, acc):
            base = pl.multiple_of(grp * 8, 8)
            ks8 = rest_ref[pl.ds(base, 8), R_CK:R_CK + 128]
            vs8 = rest_ref[pl.ds(base, 8), R_CV:R_CV + W_C]
            bs8 = b_ref[pl.ds(base, 8), :]
            for j in range(8):
                d = jnp.exp(jnp.minimum(bcum - bs8[j:j + 1, :], 0.0)) * qg * ks8[j:j + 1, :]
                d = jnp.where(trow >= base + j, d, 0.0)
                acc = acc + _dotf(d.astype(BF16), ind) * vs8[j:j + 1, :]
            return acc

        oi_ref[...] = lax.fori_loop(0, q // 8, body, jnp.zeros((q, W_C), F32))

    sg = sg_ref[...]
    o = oi_ref[...] + _dotf(qe.astype(BF16), sg.astype(BF16))
    k2 = (kg * jnp.exp(b_last - bcum)).astype(BF16)
    updg = lax.dot_general(k2, vg.astype(BF16), (((0,), (0,)), ((), ())), preferred_element_type=F32)
    same_head = (_iota((128, W_C), 0) >> 5) == (_iota((128, W_C), 1) >> 6)
    sg = sg * jnp.exp(bcum.T[:, q - 1:q]) + jnp.where(same_head, updg, 0.0)
    sg_ref[...] = sg
    gla_out_ref[...] = sg
    avg = jnp.where((_iota((W_C, W_C), 0) >> 6) == (_iota((W_C, W_C), 1) >> 6), 1.0, 0.0).astype(BF16)
    ms = _rmat_exact(o * o, avg) * (1.0 / DV_C)
    oc = o * lax.rsqrt(ms + EPS) * v256_ref[2:3, :] * _silu(rest_ref[:, R_CG:R_CG + W_C])
    oc_ref[...] = oc.astype(BF16)


def bc_mixers_prompt(lidx, rest, convw, convb3, vec128, vec256, wlr):
    b, seq, _ = rest.shape
    nc = seq // CHUNK
    par = lambda r, w: pl.BlockSpec((None, r, w), lambda bi, c, l: (l[0], 0, 0))
    st = lambda r, w: pl.BlockSpec((None, r, w), lambda bi, c, l: (bi, 0, 0))
    grid_spec = pltpu.PrefetchScalarGridSpec(
        num_scalar_prefetch=1, grid=(b, nc),
        in_specs=[pl.BlockSpec((None, CHUNK, W_REST), lambda bi, c, l: (bi, c, 0)),
                  par(CONV_W, XBC_B), par(1, XBC_B), par(8, 128), par(8, 256), par(128, 128)],
        out_specs=[pl.BlockSpec((None, CHUNK, W_B), lambda bi, c, l: (bi, c, 0)),
                   pl.BlockSpec((None, CHUNK, W_C), lambda bi, c, l: (bi, c, 0)),
                   st(CONV_W - 1, XBC_B), st(128, W_B), st(128, W_C)],
        scratch_shapes=[pltpu.VMEM((8 + CHUNK, XBC_B), F32), pltpu.VMEM((128, W_B), F32),
                        pltpu.VMEM((128, W_C), F32), pltpu.VMEM((CHUNK, 128), F32),
                        pltpu.VMEM((CHUNK, W_C), F32)])
    return pl.pallas_call(
        _bc_prompt_kernel,
        out_shape=[jax.ShapeDtypeStruct((b, seq, W_B), BF16), jax.ShapeDtypeStruct((b, seq, W_C), BF16),
                   jax.ShapeDtypeStruct((b, CONV_W - 1, XBC_B), F32),
                   jax.ShapeDtypeStruct((b, 128, W_B), F32), jax.ShapeDtypeStruct((b, 128, W_C), F32)],
        grid_spec=grid_spec,
        compiler_params=_cparams(("arbitrary", "arbitrary")),
        name="bc_mixers_prompt",
    )(lidx, rest, convw, convb3, vec128, vec256, wlr)


def _unpack_ssm_state(sbd):
    b = sbd.shape[0]
    s = sbd.reshape(b, G_B, N_B, H_B, P_B)
    per_head = [s[:, h // (H_B // G_B), :, h, :] for h in range(H_B)]
    return jnp.swapaxes(jnp.stack(per_head, axis=1), 2, 3)


def _unpack_gla_state(sg):
    b = sg.shape[0]
    s = sg.reshape(b, H_C, DK_C, H_C, DV_C)
    return jnp.stack([s[:, h, :, h, :] for h in range(H_C)], axis=1)


def _out_proj_kernel(l_ref, oa_ref, ob_ref, oc_ref, w_ref, x_ref, g1_ref, sh2_ref, sc2_ref, x1_ref, h2_ref):
    del l_ref
    mix = (_dotf(oa_ref[...], w_ref[0:W_A, :]) + _dotf(ob_ref[...], w_ref[W_A:W_A + W_B, :])
           + _dotf(oc_ref[...], w_ref[W_A + W_B:, :]))
    x1 = x_ref[...] + g1_ref[...] * mix
    x1_ref[...] = x1
    h2_ref[...] = (_rms(x1) * (1.0 + sc2_ref[...]) + sh2_ref[...]).astype(h2_ref.dtype)


def out_projection(lidx, oa, ob, oc, w_out, x, mod, tm, h2_dtype):
    g, r, d = x.shape
    per_row = mod.shape[2] > 1
    act = lambda w: pl.BlockSpec((None, tm, w), lambda gi, i, l: (gi, i, 0))
    grid_spec = pltpu.PrefetchScalarGridSpec(
        num_scalar_prefetch=1, grid=(g, r // tm),
        in_specs=[act(W_A), act(W_B), act(W_C),
                  pl.BlockSpec((None, d, d), lambda gi, i, l: (l[0], 0, 0)),
                  act(d), _mod_spec(tm, per_row, 2), _mod_spec(tm, per_row, 3), _mod_spec(tm, per_row, 4)],
        out_specs=[act(d), act(d)])
    return pl.pallas_call(
        _out_proj_kernel,
        out_shape=[jax.ShapeDtypeStruct((g, r, d), F32), jax.ShapeDtypeStruct((g, r, d), h2_dtype)],
        grid_spec=grid_spec,
        compiler_params=_cparams(("arbitrary", "arbitrary")),
        name="out_projection",
    )(lidx, oa, ob, oc, w_out, x, mod, mod, mod)


def _ffn_kernel(i_ref, h_ref, wg_ref, wu_ref, wd_ref, x1_ref, g2_ref, fg_ref, o_ref, acc_ref, *, final):
    del i_ref
    f = pl.program_id(2)

    @pl.when(f == 0)
    def _():
        acc_ref[...] = jnp.zeros(acc_ref.shape, F32)

    h = h_ref[...]
    a = (_silu(_dotf(h, wg_ref[...])) * _dotf(h, wu_ref[...])).astype(BF16)
    acc_ref[...] += _dotf(a, wd_ref[...])

    @pl.when(f == pl.num_programs(2) - 1)
    def _():
        x2 = x1_ref[...] + g2_ref[...] * acc_ref[...]
        o_ref[...] = _rms(x2) * fg_ref[...] if final else x2


def dense_ffn(iidx, lidx_mod, h2, wg, wu, wd, x1, mod, final_g, tm, tf, final):
    g, r, d = x1.shape
    per_row = mod.shape[2] > 1
    nf = D_FF // tf
    act = lambda w: pl.BlockSpec((None, tm, w), lambda gi, i, f, s: (gi, i, 0))
    if per_row:
        g2 = pl.BlockSpec((None, None, tm, d), lambda gi, i, f, s: (s[1], gi, i, 5))
    else:
        g2 = pl.BlockSpec((None, None, 1, d), lambda gi, i, f, s: (s[1], gi, 0, 5))
    grid_spec = pltpu.PrefetchScalarGridSpec(
        num_scalar_prefetch=1, grid=(g, r // tm, nf),
        in_specs=[act(d),
                  pl.BlockSpec((None, d, tf), lambda gi, i, f, s: (s[0], 0, f)),
                  pl.BlockSpec((None, d, tf), lambda gi, i, f, s: (s[0], 0, f)),
                  pl.BlockSpec((None, tf, d), lambda gi, i, f, s: (s[0], f, 0)),
                  act(d), g2, pl.BlockSpec((1, d), lambda gi, i, f, s: (0, 0))],
        out_specs=act(d),
        scratch_shapes=[pltpu.VMEM((tm, d), F32)])
    sidx = jnp.concatenate([iidx, lidx_mod])
    return pl.pallas_call(
        functools.partial(_ffn_kernel, final=final),
        out_shape=jax.ShapeDtypeStruct((g, r, d), F32),
        grid_spec=grid_spec,
        compiler_params=_cparams(("arbitrary", "arbitrary", "arbitrary")),
        name="dense_ffn",
    )(sidx, h2, wg, wu, wd, x1, mod, final_g)


def _route_kernel(i_ref, h_ref, rw_ref, tri_ref, idx_ref, gate_ref, cnt_ref, carry_ref):
    del i_ref
    i = pl.program_id(0)

    @pl.when(i == 0)
    def _():
        carry_ref[...] = jnp.zeros(carry_ref.shape, F32)

    hh, hl = _split2(h_ref[...])
    wh, wl = _split2(rw_ref[...])
    logits = _dotf(hh, wh) + _dotf(hl, wh) + _dotf(hh, wl)
    lane = _iota(logits.shape, 1).astype(F32)
    logits = jnp.where(lane < N_EXPERTS, logits, NEG)
    m1 = jnp.max(logits, axis=1, keepdims=True)
    i1 = jnp.min(jnp.where(logits == m1, lane, 128.0), axis=1, keepdims=True)
    rest = jnp.where(lane == i1, NEG, logits)
    m2 = jnp.max(rest, axis=1, keepdims=True)
    i2 = jnp.min(jnp.where(rest == m2, lane, 128.0), axis=1, keepdims=True)
    e = jnp.exp(m2 - m1)
    g1 = 1.0 / (1.0 + e)
    sel = jnp.where(lane == i1, 1.0, 0.0) + jnp.where(lane == i2, 1.0, 0.0)
    before = _dotf(tri_ref[...], sel.astype(BF16)) + carry_ref[...]
    r1 = jnp.sum(jnp.where(lane == i1, before, 0.0), axis=1, keepdims=True)
    r2 = jnp.sum(jnp.where(lane == i2, before, 0.0), axis=1, keepdims=True)
    carry_ref[...] = carry_ref[...] + jnp.sum(sel, axis=0, keepdims=True)
    lane8 = _iota(idx_ref.shape, 1).astype(F32)
    idx_ref[...] = jnp.where(lane8 == 0.0, i1, jnp.where(lane8 == 1.0, i2, jnp.where(lane8 == 2.0, r1, r2))).astype(I32)
    gate_ref[...] = jnp.where(lane8 == 0.0, g1, e * g1)
    cnt_ref[...] = jnp.broadcast_to(carry_ref[...], cnt_ref.shape)


def moe_route(iidx, h2, router_pad, tm):
    t, d = h2.shape
    tri = jnp.asarray(np.tril(np.ones((tm, tm), np.float32), -1), BF16)
    grid_spec = pltpu.PrefetchScalarGridSpec(
        num_scalar_prefetch=1, grid=(t // tm,),
        in_specs=[pl.BlockSpec((tm, d), lambda i, s: (i, 0)),
                  pl.BlockSpec((None, d, 128), lambda i, s: (s[0], 0, 0)),
                  pl.BlockSpec((tm, tm), lambda i, s: (0, 0))],
        out_specs=[pl.BlockSpec((tm, 8), lambda i, s: (i, 0)), pl.BlockSpec((tm, 8), lambda i, s: (i, 0)),
                   pl.BlockSpec((8, 128), lambda i, s: (0, 0))],
        scratch_shapes=[pltpu.VMEM((1, 128), F32)])
    return pl.pallas_call(
        _route_kernel,
        out_shape=[jax.ShapeDtypeStruct((t, 8), I32), jax.ShapeDtypeStruct((t, 8), F32),
                   jax.ShapeDtypeStruct((8, 128), F32)],
        grid_spec=grid_spec,
        compiler_params=_cparams(("arbitrary",)),
        name="moe_route",
    )(iidx, h2, router_pad, tri)


def _row_copy(src_ref, src_row, dst_ref, dst_row, sem):
    return pltpu.make_async_copy(src_ref.at[pl.ds(src_row, 1), :], dst_ref.at[pl.ds(dst_row, 1), :], sem)


def _dispatch_kernel(dest_ref, h_ref, xb_in_ref, xb_ref, sem):
    del xb_in_ref
    tm = h_ref.shape[0]

    def start(r, carry):
        for k in range(TOP_K):
            _row_copy(h_ref, r, xb_ref, dest_ref[0, TOP_K * r + k], sem).start()
        return carry

    def wait(r, carry):
        for k in range(TOP_K):
            _row_copy(h_ref, r, xb_ref, dest_ref[0, TOP_K * r + k], sem).wait()
        return carry

    lax.fori_loop(0, tm, start, 0)
    lax.fori_loop(0, tm, wait, 0)


def moe_dispatch(dest3, h2, n_rows, tm):
    t, d = h2.shape
    zeros = jnp.zeros((n_rows, d), F32)
    return pl.pallas_call(
        _dispatch_kernel,
        out_shape=jax.ShapeDtypeStruct((n_rows, d), F32),
        grid=(t // tm,),
        in_specs=[pl.BlockSpec((None, 1, TOP_K * tm), lambda i: (i, 0, 0), memory_space=pltpu.SMEM),
                  pl.BlockSpec((tm, d), lambda i: (i, 0)),
                  pl.BlockSpec(memory_space=pl.ANY)],
        out_specs=pl.BlockSpec(memory_space=pl.ANY),
        scratch_shapes=[pltpu.SemaphoreType.DMA(())],
        input_output_aliases={2: 0},
        compiler_params=pltpu.CompilerParams(dimension_semantics=("arbitrary",), vmem_limit_bytes=VMEM_LIMIT,
                                             has_side_effects=True),
        name="moe_dispatch",
    )(dest3, h2, zeros)


def _expert_kernel(s_ref, x_ref, wg_ref, wu_ref, wd_ref, y_ref, xb_ref, acc_ref):
    i = pl.program_id(0)
    f = pl.program_id(1)
    nb = pl.num_programs(0)

    @pl.when(i < s_ref[nb + 1])
    def _():
        @pl.when(f == 0)
        def _():
            xb_ref[...] = x_ref[...].astype(BF16)
            acc_ref[...] = jnp.zeros(acc_ref.shape, F32)

        x = xb_ref[...]
        a = (_silu(_dotf(x, wg_ref[...])) * _dotf(x, wu_ref[...])).astype(BF16)
        acc_ref[...] += _dotf(a, wd_ref[...])

        @pl.when(f == pl.num_programs(1) - 1)
        def _():
            y_ref[...] = acc_ref[...]

    @pl.when(jnp.logical_and(i >= s_ref[nb + 1], f == pl.num_programs(1) - 1))
    def _():
        y_ref[...] = jnp.zeros(y_ref.shape, F32)


def moe_experts(sidx, xb, wg, wu, wd, blk, tf):
    n_rows, d = xb.shape
    nb = n_rows // blk
    nf = D_FF // tf

    def row_map(i, f, s):
        return (jnp.minimum(i, s[nb + 1] - 1), 0)

    def f_of(i, f, s):
        return jnp.where(i < s[nb + 1], f, nf - 1)

    grid_spec = pltpu.PrefetchScalarGridSpec(
        num_scalar_prefetch=1, grid=(nb, nf),
        in_specs=[pl.BlockSpec((blk, d), row_map),
                  pl.BlockSpec((None, None, d, tf), lambda i, f, s: (s[nb], s[i], 0, f_of(i, f, s))),
                  pl.BlockSpec((None, None, d, tf), lambda i, f, s: (s[nb], s[i], 0, f_of(i, f, s))),
                  pl.BlockSpec((None, None, tf, d), lambda i, f, s: (s[nb], s[i], f_of(i, f, s), 0))],
        out_specs=pl.BlockSpec((blk, d), lambda i, f, s: (i, 0)),
        scratch_shapes=[pltpu.VMEM((blk, d), BF16), pltpu.VMEM((blk, d), F32)])
    return pl.pallas_call(
        _expert_kernel,
        out_shape=jax.ShapeDtypeStruct((n_rows, d), F32),
        grid_spec=grid_spec,
        compiler_params=_cparams(("arbitrary", "arbitrary")),
        name="moe_experts",
    )(sidx, xb, wg, wu, wd)


def _combine_kernel(dest_ref, yb_ref, gate_ref, x1_ref, g2_ref, fg_ref, o_ref, buf_ref, sem, *, final):
    tm = x1_ref.shape[0]

    def start(r, carry):
        for k in range(TOP_K):
            _row_copy(yb_ref, dest_ref[0, TOP_K * r + k], buf_ref.at[k], r, sem).start()
        return carry

    def wait(r, carry):
        for k in range(TOP_K):
            _row_copy(yb_ref, dest_ref[0, TOP_K * r + k], buf_ref.at[k], r, sem).wait()
        return carry

    lax.fori_loop(0, tm, start, 0)
    lax.fori_loop(0, tm, wait, 0)
    gate = gate_ref[...]
    f = gate[:, 0:1] * buf_ref[0] + gate[:, 1:2] * buf_ref[1]
    x2 = x1_ref[...] + g2_ref[...] * f
    o_ref[...] = _rms(x2) * fg_ref[...] if final else x2


def moe_combine(dest3, yb, gates, x1, mod, lidx_mod, final_g, tm, final):
    g, r, d = x1.shape
    per_row = mod.shape[1] > 1
    nt = r // tm
    del lidx_mod
    act = pl.BlockSpec((None, tm, d), lambda gi, i: (gi, i, 0))
    return pl.pallas_call(
        functools.partial(_combine_kernel, final=final),
        out_shape=jax.ShapeDtypeStruct((g, r, d), F32),
        grid=(g, nt),
        in_specs=[pl.BlockSpec((None, 1, TOP_K * tm), lambda gi, i: (gi * nt + i, 0, 0), memory_space=pltpu.SMEM),
                  pl.BlockSpec(memory_space=pl.ANY),
                  pl.BlockSpec((tm, 8), lambda gi, i: (gi * nt + i, 0)),
                  act,
                  (pl.BlockSpec((None, tm, d), lambda gi, i: (gi, i, 0)) if per_row
                   else pl.BlockSpec((None, 1, d), lambda gi, i: (gi, 0, 0))),
                  pl.BlockSpec((1, d), lambda gi, i: (0, 0))],
        out_specs=act,
        scratch_shapes=[pltpu.VMEM((TOP_K, tm, d), F32), pltpu.SemaphoreType.DMA(())],
        compiler_params=_cparams(("arbitrary", "arbitrary")),
        name="moe_combine",
    )(dest3, yb, gates, x1, mod, final_g)


def moe_ffn(layer_slot, h2, x1, mod_g2, router_pad, wg, wu, wd, final_g, tm, blk, tf, final):
    g, r, d = x1.shape
    t = g * r
    h2f = h2.reshape(t, d)
    iidx = jnp.array([layer_slot], I32)
    idx, gates, counts = moe_route(iidx, h2f, router_pad, tm)
    counts = counts[0, :N_EXPERTS].astype(I32)
    padded = (counts + blk - 1) // blk * blk
    pad_end = jnp.cumsum(padded)
    pad_start = pad_end - padded
    first = sum(jnp.where(idx[:, 0:TOP_K] == e, pad_start[e], 0) for e in range(N_EXPERTS))
    dest = first + idx[:, TOP_K:2 * TOP_K]
    nb = -(-(t * TOP_K) // blk) + N_EXPERTS
    n_used = pad_end[-1] // blk
    blk_e = jnp.minimum(jnp.searchsorted(pad_end, jnp.arange(nb, dtype=I32) * blk, side='right'), N_EXPERTS - 1)
    blk_e = jnp.where(jnp.arange(nb) < n_used, blk_e, blk_e[jnp.maximum(n_used - 1, 0)]).astype(I32)
    dest3 = dest.astype(I32).reshape(t // tm, 1, TOP_K * tm)
    xb = moe_dispatch(dest3, h2f, nb * blk, tm)
    sidx = jnp.concatenate([blk_e, iidx, n_used.astype(I32)[None]])
    yb = moe_experts(sidx, xb, wg, wu, wd, blk, tf)
    return moe_combine(dest3, yb, gates, x1, mod_g2, None, final_g, tm, final)


def _attn_decode_kernel(pt_ref, lam_ref, lqk_ref, g_ref, q_ref, kn_ref, vn_ref, tab_ref, *refs, pps, n_pages):
    k_refs = refs[:pps]
    v_refs = refs[pps:2 * pps]
    o_ref, m_ref, s_ref, acc_ref = refs[2 * pps:]
    step = pl.program_id(1)
    layer = pt_ref[pt_ref.shape[0] - 1]
    rows = 2 * H_A

    @pl.when(step == 0)
    def _():
        m_ref[...] = jnp.full(m_ref.shape, NEG, F32)
        s_ref[...] = jnp.zeros(s_ref.shape, F32)
        acc_ref[...] = jnp.zeros(acc_ref.shape, F32)

    qrow = jnp.broadcast_to(q_ref[...].astype(F32), (rows, W_A))
    own = (_iota((rows, W_A), 1) >> 6) == _iota((rows, W_A), 0)
    q8 = jnp.where(own, qrow, 0.0).astype(BF16)

    def accumulate(s, pv):
        m_old = m_ref[...]
        m_new = jnp.maximum(m_old, jnp.max(s, axis=1, keepdims=True))
        alpha = jnp.exp(m_old - m_new)
        p = jnp.exp(s - m_new)
        s_ref[...] = alpha * s_ref[...] + jnp.sum(p, axis=1, keepdims=True)
        acc_ref[...] = alpha * acc_ref[...] + pv(p)
        m_ref[...] = m_new

    scores = [_dotf(q8, k_refs[i][...].astype(BF16)) for i in range(pps)]
    scores[pps - 1] = jnp.where(step == pl.num_programs(1) - 1, scores[pps - 1] + tab_ref[0:rows, :], scores[pps - 1])
    s_all = jnp.concatenate(scores, axis=1)

    def value_page(i):
        heads = [v_refs[i][pl.ds(h, PAGE, stride=H_A), :] for h in range(H_A)]
        return jnp.concatenate(heads, axis=1).astype(BF16)

    def pv_pages(p):
        out = _dotf(p[:, 0:PAGE].astype(BF16), value_page(0))
        for i in range(1, pps):
            out = out + _dotf(p[:, i * PAGE:(i + 1) * PAGE].astype(BF16), value_page(i))
        return out

    accumulate(s_all, pv_pages)

    @pl.when(step == pl.num_programs(1) - 1)
    def _():
        kn = kn_ref[...].astype(BF16).astype(F32)
        vn = vn_ref[...].astype(BF16).astype(F32)
        s_new = jnp.sum(q8.astype(F32) * kn, axis=1, keepdims=True) + tab_ref[rows:2 * rows, 0:1]
        accumulate(s_new, lambda p: p.astype(BF16).astype(F32) * vn)
        o = acc_ref[...] * (1.0 / s_ref[...])
        lam = _lambda_value(lqk_ref[...], lam_ref[2 * layer])
        outs = []
        for h in range(H_A):
            blkh = o[:, h * DV_A:(h + 1) * DV_A]
            oh = blkh[2 * h:2 * h + 1, :] - lam * blkh[2 * h + 1:2 * h + 2, :]
            outs.append(_rms(oh) * g_ref[...] * lam_ref[2 * layer + 1])
        o_ref[...] = jnp.concatenate(outs, axis=1).astype(BF16)


def attention_decode(sidx, lam_consts, lambda_qk, subln_g3, q, k_new, v_new, tab, cache_k4, cache_v4, n_pages, pps):
    bd = q.shape[0]
    nl = sidx.shape[0] - 1
    steps = n_pages // pps
    row = pl.BlockSpec((None, 1, W_A), lambda b, s, pt: (b, 0, 0))

    def page_spec(i):
        return pl.BlockSpec((None, None, W_A, PAGE),
                            lambda b, s, pt: (pt[nl], pt[b * n_pages + s * pps + i], 0, 0))

    grid_spec = pltpu.PrefetchScalarGridSpec(
        num_scalar_prefetch=1, grid=(bd, steps),
        in_specs=[pl.BlockSpec(memory_space=pltpu.SMEM),
                  pl.BlockSpec((None, 4, DK_A), lambda b, s, pt: (pt[nl], 0, 0)),
                  pl.BlockSpec((None, 1, DV_A), lambda b, s, pt: (pt[nl], 0, 0)),
                  row, row, row,
                  pl.BlockSpec((16, PAGE), lambda b, s, pt: (0, 0))]
                 + [page_spec(i) for i in range(pps)] + [page_spec(i) for i in range(pps)],
        out_specs=row,
        scratch_shapes=[pltpu.VMEM((2 * H_A, 1), F32), pltpu.VMEM((2 * H_A, 1), F32),
                        pltpu.VMEM((2 * H_A, W_A), F32)])
    return pl.pallas_call(
        functools.partial(_attn_decode_kernel, pps=pps, n_pages=n_pages),
        out_shape=jax.ShapeDtypeStruct((bd, 1, W_A), BF16),
        grid_spec=grid_spec,
        compiler_params=_cparams(("arbitrary", "arbitrary")),
        name="attention_decode",
    )(sidx, lam_consts, lambda_qk, subln_g3, q, k_new, v_new, tab, *([cache_k4] * pps), *([cache_v4] * pps))


def _bc_decode_pre_kernel(l_ref, rest_ref, buf_ref, convw_ref, convb_ref, v128_ref, wlr_ref,
                          xbc_ref, nbuf_ref, dt_ref, dec_ref, eg_ref):
    del l_ref
    u = rest_ref[:, R_BX:R_BX + XBC_B]
    conv = convb_ref[...] + convw_ref[CONV_W - 1:CONV_W, :] * u
    for j in range(CONV_W - 1):
        conv = conv + convw_ref[j:j + 1, :] * buf_ref[j]
    xbc_ref[...] = _silu(conv)
    for j in range(CONV_W - 2):
        nbuf_ref[j] = buf_ref[j + 1]
    nbuf_ref[CONV_W - 2] = u
    misc = rest_ref[:, R_MISC:R_MISC + 128]
    dt = _softplus(misc + v128_ref[0:1, :])
    dt_ref[...] = dt
    dec_ref[...] = jnp.exp(dt * (-jnp.exp(v128_ref[1:2, :])))
    gk = _dotf(misc.astype(BF16), wlr_ref[...]) + v128_ref[2:3, :]
    eg_ref[...] = jnp.exp(-_softplus(-gk) * (1.0 / GATE_NORM))


def bc_decode_pre(lidx, rest, conv_state_t, convw, convb3, vec128, wlr):
    bd = rest.shape[0]
    par = lambda r, w: pl.BlockSpec((None, r, w), lambda i, l: (l[0], 0, 0))
    full = lambda w: pl.BlockSpec((bd, w), lambda i, l: (0, 0))
    grid_spec = pltpu.PrefetchScalarGridSpec(
        num_scalar_prefetch=1, grid=(1,),
        in_specs=[full(W_REST),
                  pl.BlockSpec((None, CONV_W - 1, bd, XBC_B), lambda i, l: (l[0], 0, 0, 0)),
                  par(CONV_W, XBC_B), par(1, XBC_B), par(8, 128), par(128, 128)],
        out_specs=[full(XBC_B), pl.BlockSpec((CONV_W - 1, bd, XBC_B), lambda i, l: (0, 0, 0)),
                   full(128), full(128), full(128)])
    return pl.pallas_call(
        _bc_decode_pre_kernel,
        out_shape=[jax.ShapeDtypeStruct((bd, XBC_B), F32), jax.ShapeDtypeStruct((CONV_W - 1, bd, XBC_B), F32),
                   jax.ShapeDtypeStruct((bd, 128), F32), jax.ShapeDtypeStruct((bd, 128), F32),
                   jax.ShapeDtypeStruct((bd, 128), F32)],
        grid_spec=grid_spec,
        compiler_params=_cparams(("arbitrary",)),
        name="bc_decode_pre",
    )(lidx, rest, conv_state_t, convw, convb3, vec128, wlr)


def _bc_decode_state_kernel(l_ref, ssm_ref, gla_ref, x4_ref, b4_ref, c4_ref, dt4_ref, dec4_ref, bz4_ref,
                            dsk_ref, ng_ref, q4_ref, k4_ref, v4_ref, eg4_ref, cg4_ref, gng_ref,
                            ssm_out_ref, gla_out_ref, ob4_ref, oc4_ref):
    del l_ref
    x4 = x4_ref[...]
    s = ssm_ref[...] * dec4_ref[...] + (dt4_ref[...] * x4) * b4_ref[...]
    ssm_out_ref[...] = s
    y = jnp.sum(c4_ref[...] * s, axis=3, keepdims=True) + dsk_ref[...] * x4
    yg = y * _silu(bz4_ref[...])
    ms = jnp.sum(jnp.sum(yg * yg, axis=2, keepdims=True), axis=1, keepdims=True) * (1.0 / W_B)
    ob4_ref[...] = yg * lax.rsqrt(ms + EPS) * ng_ref[...]
    sg = gla_ref[...] * eg4_ref[...] + k4_ref[...] * v4_ref[...]
    gla_out_ref[...] = sg
    o = jnp.sum(q4_ref[...] * sg, axis=2, keepdims=True)
    oc4_ref[...] = _rms(o) * gng_ref[...] * _silu(cg4_ref[...])


def bc_decode_state(lidx, state_ssm, state_gla, x4, b4, c4, dt4, dec4, bz4, dsk4, ng4, q4, k4, v4, eg4, cg4, gng4):
    bd = x4.shape[0]

    def full(shape):
        n = len(shape)
        return pl.BlockSpec(shape, lambda i, l: (0,) * n)

    def layer(shape):
        n = len(shape)
        return pl.BlockSpec((None,) + shape, lambda i, l: (l[0],) + (0,) * n)

    s_ssm = (bd, H_B, P_B, N_B)
    s_gla = (bd, H_C, DK_C, DV_C)
    col_b = (bd, H_B, P_B, 1)
    row_b = (bd, H_B, 1, N_B)
    one_b = (bd, H_B, 1, 1)
    col_c = (bd, H_C, DK_C, 1)
    row_c = (bd, H_C, 1, DV_C)
    grid_spec = pltpu.PrefetchScalarGridSpec(
        num_scalar_prefetch=1, grid=(1,),
        in_specs=[layer(s_ssm), layer(s_gla), full(col_b), full(row_b), full(row_b), full(one_b), full(one_b),
                  full(col_b), layer((1, H_B, 1, 1)), layer((1, H_B, P_B, 1)),
                  full(col_c), full(col_c), full(row_c), full(col_c), full(row_c), layer((1, 1, 1, DV_C))],
        out_specs=[full(s_ssm), full(s_gla), full(col_b), full(row_c)])
    return pl.pallas_call(
        _bc_decode_state_kernel,
        out_shape=[jax.ShapeDtypeStruct(s_ssm, F32), jax.ShapeDtypeStruct(s_gla, F32),
                   jax.ShapeDtypeStruct(col_b, F32), jax.ShapeDtypeStruct(row_c, F32)],
        grid_spec=grid_spec,
        compiler_params=_cparams(("arbitrary",)),
        name="bc_decode_state",
    )(lidx, state_ssm, state_gla, x4, b4, c4, dt4, dec4, bz4, dsk4, ng4, q4, k4, v4, eg4, cg4, gng4)


def _pack_params(w_in, rel_bias, subln_g, conv_b, dt_bias, a_log, d_skip, ssd_norm_g, gla_w_lr, gla_b_lr,
                 gla_norm_g, router_w):
    depth = w_in.shape[0]
    pad = jnp.zeros(w_in.shape[:2] + (D_PACK - D_IN,), w_in.dtype)
    w_pack = jnp.concatenate([w_in[..., :OFF_BDT], w_in[..., OFF_CQ:OFF_CLR], w_in[..., OFF_BDT:OFF_CQ],
                              w_in[..., OFF_CLR:], pad], axis=-1).astype(BF16)
    vec128 = jnp.zeros((depth, 8, 128), F32)
    vec128 = vec128.at[:, 0, :H_B].set(dt_bias).at[:, 1, :H_B].set(a_log).at[:, 2, :].set(gla_b_lr)
    vec256 = jnp.zeros((depth, 8, 256), F32)
    vec256 = (vec256.at[:, 0, :].set(jnp.repeat(d_skip, P_B, axis=1)).at[:, 1, :].set(ssd_norm_g)
              .at[:, 2, :].set(jnp.tile(gla_norm_g, (1, H_C))))
    wlr = jnp.zeros((depth, 128, 128), F32).at[:, MISC_LR:MISC_LR + GATE_RANK, :].set(gla_w_lr).astype(BF16)
    router_pad = jnp.zeros(router_w.shape[:2] + (128,), F32).at[..., :N_EXPERTS].set(router_w)
    lam_init = [0.8 - 0.6 * math.exp(-0.3 * l) for l in range(depth)]
    lam_consts = jnp.asarray(np.array([[li, 1.0 - li] for li in lam_init], np.float32).reshape(-1))
    return dict(w_pack=w_pack, vec128=vec128, vec256=vec256, wlr=wlr, router_pad=router_pad,
                lam_consts=lam_consts, subln_g3=subln_g.reshape(depth, 1, DV_A),
                convb3=conv_b.reshape(depth, 1, XBC_B))


def _pick(n, pref):
    if n <= pref:
        return n
    t = pref
    while n % t:
        t //= 2
    return t


def _channel_mixer(l, depth, h2, x1, mod, pk, wts, tm, moe_blk, final):
    i = l // 2
    lmod = jnp.array([l], I32)
    if l % 2 == 0:
        return dense_ffn(jnp.array([i], I32), lmod, h2, wts['ffn_g'], wts['ffn_u'], wts['ffn_d'], x1, mod,
                         wts['final_g'], tm, _pick(D_FF, 1408), final)
    mod_g2 = mod[l, :, :, 5 * D_MODEL:6 * D_MODEL]
    return moe_ffn(i, h2, x1, mod_g2, pk['router_pad'], wts['moe_g'], wts['moe_u'], wts['moe_d'], wts['final_g'],
                   _pick(x1.shape[0] * x1.shape[1], 256), moe_blk, _pick(D_FF, 1408), final)


def _run_prompt(x, mod, pk, wts, lambda_qk, conv_w, rel_bias):
    b, seq, d = x.shape
    depth = mod.shape[0]
    tm = _pick(seq, 512)
    blk = tm
    table = _t5_table(rel_bias, 2 * blk)
    qpos = jnp.arange(blk)[:, None]
    kpos = jnp.arange(blk)[None, :]
    diag = jnp.where(qpos >= kpos, table[:, jnp.maximum(qpos - kpos, 0)], NEG)
    sub = table[:, blk + qpos - kpos]
    bias_tiles = jnp.stack([diag, sub], axis=1)
    ks, vs, convs, ssms, glas = [], [], [], [], []
    for l in range(depth):
        lidx = jnp.array([l], I32)
        q, k, v, rest, kt, vb = in_projection(lidx, x, mod, pk['w_pack'], tm, True)
        oa = attention_prompt(lidx, pk['lam_consts'], lambda_qk, pk['subln_g3'], q, kt, vb, bias_tiles, blk)
        ob, oc, conv_s, ssm_s, gla_s = bc_mixers_prompt(lidx, rest, conv_w, pk['convb3'], pk['vec128'],
                                                        pk['vec256'], pk['wlr'])
        moe = l % 2 == 1
        x1, h2 = out_projection(lidx, oa, ob, oc, wts['w_out'], x, mod, tm, F32 if moe else BF16)
        x = _channel_mixer(l, depth, h2, x1, mod, pk, wts, tm, _pick(b * seq, 512), l == depth - 1)
        ks.append(k.reshape(b, seq, 2 * H_A, DK_A))
        vs.append(v.reshape(b, seq, H_A, DV_A))
        convs.append(conv_s)
        ssms.append(_unpack_ssm_state(ssm_s))
        glas.append(_unpack_gla_state(gla_s))
    return x, jnp.stack(ks), jnp.stack(vs), jnp.stack(convs), jnp.stack(ssms), jnp.stack(glas)


def _run_decode(x, mod, pk, wts, lambda_qk, conv_w, rel_bias, cache_k, cache_v, page_table,
                state_conv, state_ssm, state_gla, d_skip, ssd_norm_g, gla_norm_g):
    bd = x.shape[0]
    depth = mod.shape[0]
    n_pages = page_table.shape[1]
    past = n_pages * PAGE
    pps = _pick(n_pages, 8)
    xg = x.reshape(1, bd, D_MODEL)
    pool = cache_k.shape[1]
    assert PAGE >= T5_MAX_DIST
    cache_k4 = jnp.transpose(cache_k, (0, 1, 3, 4, 2)).reshape(depth, pool, 2 * H_A * DK_A, PAGE)
    cache_v4 = cache_v.reshape(depth, pool, PAGE * H_A, DV_A)
    table = _t5_table(rel_bias, PAGE + 1)
    dist = past - ((n_pages - 1) * PAGE + jnp.arange(PAGE))
    tab = jnp.concatenate([jnp.repeat(table[:, dist], 2, axis=0),
                           jnp.broadcast_to(jnp.repeat(table[:, 0], 2)[:, None], (2 * H_A, PAGE))], axis=0)
    conv_t = jnp.swapaxes(state_conv, 1, 2)
    grp = np.arange(H_B) // (H_B // G_B)
    dsk4 = d_skip.reshape(depth, 1, H_B, 1, 1)
    ng4 = ssd_norm_g.reshape(depth, 1, H_B, P_B, 1)
    gng4 = gla_norm_g.reshape(depth, 1, 1, 1, DV_C)
    pt_flat = page_table.reshape(-1).astype(I32)
    ks, vs, convs, ssms, glas = [], [], [], [], []
    for l in range(depth):
        lidx = jnp.array([l], I32)
        q, k, v, rest = in_projection(lidx, xg, mod, pk['w_pack'], bd, False)
        sidx = jnp.concatenate([pt_flat, lidx])
        oa = attention_decode(sidx, pk['lam_consts'], lambda_qk, pk['subln_g3'], q.reshape(bd, 1, W_A),
                              k.reshape(bd, 1, W_A), v.reshape(bd, 1, W_A), tab, cache_k4, cache_v4, n_pages, pps)
        rest2 = rest.reshape(bd, W_REST)
        xbc, nbuf, dt, dec, eg = bc_decode_pre(lidx, rest2, conv_t, conv_w, pk['convb3'], pk['vec128'], pk['wlr'])
        x4 = xbc[:, :W_B].reshape(bd, H_B, P_B, 1)
        b4 = xbc[:, W_B:W_B + G_B * N_B].reshape(bd, G_B, 1, N_B)[:, grp]
        c4 = xbc[:, W_B + G_B * N_B:].reshape(bd, G_B, 1, N_B)[:, grp]
        dt4 = dt[:, :H_B].reshape(bd, H_B, 1, 1)
        dec4 = dec[:, :H_B].reshape(bd, H_B, 1, 1)
        bz4 = rest2[:, R_BZ:R_BZ + W_B].reshape(bd, H_B, P_B, 1)
        q4 = (rest2[:, R_CQ:R_CQ + 128] * (DK_C ** -0.5)).reshape(bd, H_C, DK_C, 1)
        k4 = rest2[:, R_CK:R_CK + 128].reshape(bd, H_C, DK_C, 1)
        v4 = rest2[:, R_CV:R_CV + W_C].reshape(bd, H_C, 1, DV_C)
        eg4 = eg.reshape(bd, H_C, DK_C, 1)
        cg4 = rest2[:, R_CG:R_CG + W_C].reshape(bd, H_C, 1, DV_C)
        ssm_s, gla_s, ob4, oc4 = bc_decode_state(lidx, state_ssm, state_gla, x4, b4, c4, dt4, dec4, bz4, dsk4, ng4,
                                                 q4, k4, v4, eg4, cg4, gng4)
        ob = ob4.reshape(1, bd, W_B).astype(BF16)
        oc = oc4.reshape(1, bd, W_C).astype(BF16)
        moe = l % 2 == 1
        x1, h2 = out_projection(lidx, oa.reshape(1, bd, W_A), ob, oc, wts['w_out'], xg, mod, bd, F32 if moe else BF16)
        xg = _channel_mixer(l, depth, h2, x1, mod, pk, wts, bd, 64, l == depth - 1)
        ks.append(k.reshape(bd, 1, 2 * H_A, DK_A))
        vs.append(v.reshape(bd, 1, H_A, DV_A))
        convs.append(jnp.swapaxes(nbuf, 0, 1))
        ssms.append(ssm_s)
        glas.append(gla_s)
    return (xg.reshape(bd, 1, D_MODEL), jnp.stack(ks), jnp.stack(vs), jnp.stack(convs), jnp.stack(ssms),
            jnp.stack(glas))


def kernel(x_prompt, x_sample, c_prompt, c_sample, cache_k, cache_v, page_table, state_conv, state_ssm, state_gla,
           w_ada, b_ada, w_in, w_out, rel_bias, lambda_qk, subln_g, conv_w, conv_b, dt_bias, a_log, d_skip,
           ssd_norm_g, gla_w_lr, gla_b_lr, gla_norm_g, ffn_w_gate, ffn_w_up, ffn_w_down, router_w,
           moe_w_gate, moe_w_up, moe_w_down, final_norm_g):
    depth = w_in.shape[0]
    bp, bd = c_prompt.shape[0], c_sample.shape[0]
    pk = _pack_params(w_in, rel_bias, subln_g, conv_b, dt_bias, a_log, d_skip, ssd_norm_g, gla_w_lr, gla_b_lr,
                      gla_norm_g, router_w)
    wts = dict(w_out=w_out.astype(BF16), ffn_g=ffn_w_gate.astype(BF16), ffn_u=ffn_w_up.astype(BF16),
               ffn_d=ffn_w_down.astype(BF16), moe_g=moe_w_gate.astype(BF16), moe_u=moe_w_up.astype(BF16),
               moe_d=moe_w_down.astype(BF16), final_g=final_norm_g.reshape(1, D_MODEL))
    n_c = bp + bd
    n_cp = -(-n_c // 8) * 8
    c_all = jnp.concatenate([c_prompt, c_sample, jnp.zeros((n_cp - n_c, D_MODEL), F32)], axis=0)
    mod = ada_modulation(c_all, w_ada, b_ada)
    mod_p = mod[:, :bp].reshape(depth, bp, 1, 6 * D_MODEL)
    mod_d = mod[:, bp:n_c].reshape(depth, 1, bd, 6 * D_MODEL)
    yp, kp, vp, convp, ssmp, glap = _run_prompt(x_prompt, mod_p, pk, wts, lambda_qk, conv_w, rel_bias)
    yd, kd, vd, convd, ssmd, glad = _run_decode(x_sample, mod_d, pk, wts, lambda_qk, conv_w, rel_bias, cache_k,
                                                cache_v, page_table, state_conv, state_ssm, state_gla, d_skip,
                                                ssd_norm_g, gla_norm_g)
    return (yp, yd, kp, vp, convp, ssmp, glap, kd, vd, convd, ssmd, glad)
```

```python
import functools
import math

import numpy as np
import jax
import jax.numpy as jnp
from jax import lax
from jax.experimental import pallas as pl
from jax.experimental.pallas import tpu as pltpu

F32 = jnp.float32
BF16 = jnp.bfloat16
I32 = jnp.int32

D_MODEL = 1024
H_A, DK_A, DV_A = 4, 64, 128
W_A = H_A * DV_A
H_B, P_B, G_B, N_B, CONV_W = 4, 64, 2, 64, 4
W_B = H_B * P_B
XBC_B = W_B + 2 * G_B * N_B
H_C, DK_C, DV_C = 4, 32, 64
W_C = H_C * DV_C
GATE_RANK, GATE_NORM = 16, 16.0
T5_BUCKETS, T5_MAX_DIST = 32, 128
D_FF, N_EXPERTS, TOP_K = 2816, 8, 2
EPS = 1e-6
PAGE = 128
CHUNK = 128
OFF_AQ = 0
OFF_AK = OFF_AQ + 2 * H_A * DK_A
OFF_AV = OFF_AK + 2 * H_A * DK_A
OFF_BZ = OFF_AV + W_A
OFF_BX = OFF_BZ + W_B
OFF_BDT = OFF_BX + XBC_B
OFF_CQ = OFF_BDT + H_B
OFF_CK = OFF_CQ + H_C * DK_C
OFF_CV = OFF_CK + H_C * DK_C
OFF_CG = OFF_CV + W_C
OFF_CLR = OFF_CG + W_C
D_IN = OFF_CLR + GATE_RANK
PK_REST = 3 * W_A
R_BZ, R_BX, R_CQ, R_CK, R_CV, R_CG, R_MISC = 0, 256, 768, 896, 1024, 1280, 1536
W_REST = R_MISC + 128
D_PACK = PK_REST + W_REST
MISC_DT, MISC_LR = 0, H_B
NEG = -1e30
VMEM_LIMIT = 56 * 1024 * 1024


def _cparams(sem):
    return pltpu.CompilerParams(dimension_semantics=sem, vmem_limit_bytes=VMEM_LIMIT)


def _rms(x):
    return x * lax.rsqrt(jnp.mean(x * x, axis=-1, keepdims=True) + EPS)


def _silu(x):
    return x * jax.nn.sigmoid(x)


def _softplus(x):
    return jnp.maximum(x, 0.0) + jnp.log1p(jnp.exp(-jnp.abs(x)))


def _split2(x):
    hi = x.astype(BF16)
    return hi, (x - hi.astype(F32)).astype(BF16)


def _split3(x):
    hi = x.astype(BF16)
    r = x - hi.astype(F32)
    mid = r.astype(BF16)
    return hi, mid, (r - mid.astype(F32)).astype(BF16)


def _dotf(a, b):
    return jnp.dot(a, b, preferred_element_type=F32)


def _lmat_exact(mat01, x):
    hi, mid, lo = _split3(x)
    return _dotf(mat01, hi) + _dotf(mat01, mid) + _dotf(mat01, lo)


def _rmat_exact(x, mat01, pieces=3):
    if pieces == 2:
        hi, lo = _split2(x)
        return _dotf(hi, mat01) + _dotf(lo, mat01)
    hi, mid, lo = _split3(x)
    return _dotf(hi, mat01) + _dotf(mid, mat01) + _dotf(lo, mat01)


def _iota(shape, axis):
    return lax.broadcasted_iota(I32, shape, axis)


def _ada_kernel(c_ref, w_ref, b_ref, o_ref):
    c = c_ref[...]
    o_ref[...] = _dotf(_silu(c).astype(BF16), w_ref[...].astype(BF16)) + b_ref[...]


def ada_modulation(c_all, w_ada, b_ada):
    depth, d, n = w_ada.shape
    bc = c_all.shape[0]
    tn = 1536 if n % 1536 == 0 else n
    return pl.pallas_call(
        _ada_kernel,
        out_shape=jax.ShapeDtypeStruct((depth, bc, n), F32),
        grid=(depth, n // tn),
        in_specs=[pl.BlockSpec((bc, d), lambda l, j: (0, 0)),
                  pl.BlockSpec((None, d, tn), lambda l, j: (l, 0, j)),
                  pl.BlockSpec((None, 1, tn), lambda l, j: (l, 0, j))],
        out_specs=pl.BlockSpec((None, bc, tn), lambda l, j: (l, 0, j)),
        compiler_params=_cparams(("arbitrary", "arbitrary")),
        name="ada_modulation",
    )(c_all, w_ada, b_ada.reshape(depth, 1, n))


def _mod_spec(tm, per_row, chunk):
    if per_row:
        return pl.BlockSpec((None, None, tm, D_MODEL), lambda g, i, l: (l[0], g, i, chunk))
    return pl.BlockSpec((None, None, 1, D_MODEL), lambda g, i, l: (l[0], g, 0, chunk))


def _in_proj_kernel(l_ref, x_ref, sh_ref, sc_ref, w_ref, q_ref, k_ref, v_ref, r_ref, *kv_refs):
    del l_ref
    h = (_rms(x_ref[...]) * (1.0 + sc_ref[...]) + sh_ref[...]).astype(BF16)
    q_ref[...] = (_dotf(h, w_ref[:, 0:W_A]) * (DK_A ** -0.5)).astype(BF16)
    k = _dotf(h, w_ref[:, W_A:2 * W_A])
    k_ref[...] = k
    v = _dotf(h, w_ref[:, 2 * W_A:3 * W_A])
    v_ref[...] = v
    r_ref[...] = _dotf(h, w_ref[:, PK_REST:D_PACK])
    if kv_refs:
        kt_ref, vb_ref = kv_refs
        kt_ref[...] = k.T.astype(BF16)
        vb_ref[...] = v.astype(BF16)


def in_projection(lidx, x, mod, w_pack, tm, attn_operands):
    g, r, d = x.shape
    per_row = mod.shape[2] > 1
    row = lambda w, dt: jax.ShapeDtypeStruct((g, r, w), dt)
    ospec = lambda w: pl.BlockSpec((None, tm, w), lambda gi, i, l: (gi, i, 0))
    out_specs = [ospec(W_A), ospec(W_A), ospec(W_A), ospec(W_REST)]
    out_shape = [row(W_A, BF16), row(W_A, F32), row(W_A, F32), row(W_REST, F32)]
    if attn_operands:
        out_specs += [pl.BlockSpec((None, None, W_A, tm), lambda gi, i, l: (gi, i, 0, 0)), ospec(W_A)]
        out_shape += [jax.ShapeDtypeStruct((g, r // tm, W_A, tm), BF16), row(W_A, BF16)]
    grid_spec = pltpu.PrefetchScalarGridSpec(
        num_scalar_prefetch=1, grid=(g, r // tm),
        in_specs=[pl.BlockSpec((None, tm, d), lambda gi, i, l: (gi, i, 0)),
                  _mod_spec(tm, per_row, 0), _mod_spec(tm, per_row, 1),
                  pl.BlockSpec((None, d, D_PACK), lambda gi, i, l: (l[0], 0, 0))],
        out_specs=out_specs)
    return pl.pallas_call(
        _in_proj_kernel,
        out_shape=out_shape,
        grid_spec=grid_spec,
        compiler_params=_cparams(("arbitrary", "arbitrary")),
        name="in_projection",
    )(lidx, x, mod, mod, w_pack)


def _lambda_value(lqk, lam_init):
    s01 = jnp.sum(lqk[0:1, :] * lqk[1:2, :], axis=1, keepdims=True)
    s23 = jnp.sum(lqk[2:3, :] * lqk[3:4, :], axis=1, keepdims=True)
    return jnp.exp(s01) - jnp.exp(s23) + lam_init


def _attn_prompt_kernel(l_ref, lam_ref, lqk_ref, g_ref, q_ref, kt_ref, v_ref, bias_ref, o_ref,
                        vaug_ref, m_ref, acc_ref, *, blk):
    qi = pl.program_id(2)

    @pl.when(qi == 0)
    def _():
        vaug_ref[:, 0:DV_A] = v_ref[...]
        ones_col = _iota((vaug_ref.shape[0], DV_A), 1) == 0
        vaug_ref[:, DV_A:2 * DV_A] = jnp.where(ones_col, 1.0, 0.0).astype(BF16)

    q = q_ref[...]
    lane = _iota(q.shape, 1)
    zero = jnp.zeros_like(q)
    qm = (jnp.where(lane < DK_A, q, zero), jnp.where(lane >= DK_A, q, zero))
    m_ref[...] = jnp.full(m_ref.shape, NEG, F32)
    acc_ref[...] = jnp.zeros(acc_ref.shape, F32)

    def update(j, bias):
        kt = kt_ref[j]
        va = vaug_ref[pl.ds(pl.multiple_of(j * blk, blk), blk), :]
        for m in range(2):
            s = _dotf(qm[m], kt)
            if bias is not None:
                s = s + bias
            m_old = m_ref[m]
            m_new = jnp.maximum(m_old, jnp.broadcast_to(jnp.max(s, axis=1, keepdims=True), m_old.shape))
            p = jnp.exp(s - jnp.tile(m_new, (1, blk // 128))).astype(BF16)
            acc_ref[m] = jnp.tile(jnp.exp(m_old - m_new), (1, 2)) * acc_ref[m] + _dotf(p, va)
            m_ref[m] = m_new

    def far(j, carry):
        update(j, None)
        return carry

    lax.fori_loop(0, jnp.maximum(qi - 1, 0), far, 0)

    @pl.when(qi >= 1)
    def _():
        update(qi - 1, bias_ref[1])

    update(qi, bias_ref[0])

    layer = l_ref[0]
    lam = _lambda_value(lqk_ref[...], lam_ref[2 * layer])
    acc0 = acc_ref[0]
    acc1 = acc_ref[1]
    o = (acc0[:, 0:DV_A] * (1.0 / acc0[:, DV_A:DV_A + 1])
         - lam * (acc1[:, 0:DV_A] * (1.0 / acc1[:, DV_A:DV_A + 1])))
    o_ref[...] = (_rms(o) * g_ref[...] * lam_ref[2 * layer + 1]).astype(BF16)


def _t5_table(rel_bias, n):
    dist = jnp.arange(n)
    max_exact = T5_BUCKETS // 2
    nf = jnp.maximum(dist, 1).astype(F32)
    large = max_exact + (jnp.log(nf / max_exact) / math.log(T5_MAX_DIST / max_exact)
                         * (T5_BUCKETS - max_exact)).astype(I32)
    bucket = jnp.where(dist < max_exact, dist, jnp.minimum(large, T5_BUCKETS - 1))
    return (rel_bias[bucket] - rel_bias[T5_BUCKETS - 1][None, :]).T.astype(F32)


def _bias_tiles_kernel(rel_ref, o_ref):
    h = pl.program_id(0)
    blk = o_ref.shape[0]
    d = _iota((blk, blk), 0) - _iota((blk, blk), 1) + pl.program_id(1) * blk
    dist = jnp.maximum(d, 0)
    max_exact = T5_BUCKETS // 2
    nf = jnp.maximum(dist, 1).astype(F32)
    large = max_exact + (jnp.log(nf / max_exact) / math.log(T5_MAX_DIST / max_exact)
                         * (T5_BUCKETS - max_exact)).astype(I32)
    bucket = jnp.where(dist < max_exact, dist, jnp.minimum(large, T5_BUCKETS - 1))
    sat = rel_ref[(T5_BUCKETS - 1) * H_A + h]
    val = jnp.zeros((blk, blk), F32)
    for b in range(T5_BUCKETS - 1):
        val = jnp.where(bucket == b, rel_ref[b * H_A + h] - sat, val)
    o_ref[...] = jnp.where(d >= 0, val, NEG)


def attention_bias_tiles(rel_bias, blk):
    return pl.pallas_call(
        _bias_tiles_kernel,
        out_shape=jax.ShapeDtypeStruct((H_A, 2, blk, blk), F32),
        grid=(H_A, 2),
        in_specs=[pl.BlockSpec(memory_space=pltpu.SMEM)],
        out_specs=pl.BlockSpec((None, None, blk, blk), lambda h, t: (h, t, 0, 0)),
        compiler_params=_cparams(("arbitrary", "arbitrary")),
        name="attention_bias_tiles",
    )(rel_bias.reshape(-1))


def attention_prompt(lidx, lam_consts, lambda_qk, subln_g3, q, kt, vb, bias_tiles, blk):
    b, seq, _ = q.shape
    assert blk >= T5_MAX_DIST and kt.shape == (b, seq // blk, W_A, blk)
    grid_spec = pltpu.PrefetchScalarGridSpec(
        num_scalar_prefetch=1, grid=(b, H_A, seq // blk),
        in_specs=[pl.BlockSpec(memory_space=pltpu.SMEM),
                  pl.BlockSpec((None, 4, DK_A), lambda bi, h, i, l: (l[0], 0, 0)),
                  pl.BlockSpec((None, 1, DV_A), lambda bi, h, i, l: (l[0], 0, 0)),
                  pl.BlockSpec((None, blk, DV_A), lambda bi, h, i, l: (bi, i, h)),
                  pl.BlockSpec((None, seq // blk, DV_A, blk), lambda bi, h, i, l: (bi, 0, h, 0)),
                  pl.BlockSpec((None, seq, DV_A), lambda bi, h, i, l: (bi, 0, h)),
                  pl.BlockSpec((None, 2, blk, blk), lambda bi, h, i, l: (h, 0, 0, 0))],
        out_specs=pl.BlockSpec((None, blk, DV_A), lambda bi, h, i, l: (bi, i, h)),
        scratch_shapes=[pltpu.VMEM((seq, 2 * DV_A), BF16), pltpu.VMEM((2, blk, 128), F32),
                        pltpu.VMEM((2, blk, 2 * DV_A), F32)])
    return pl.pallas_call(
        functools.partial(_attn_prompt_kernel, blk=blk),
        out_shape=jax.ShapeDtypeStruct((b, seq, W_A), BF16),
        grid_spec=grid_spec,
        compiler_params=_cparams(("arbitrary", "arbitrary", "arbitrary")),
        name="attention_prompt",
    )(lidx, lam_consts, lambda_qk, subln_g3, q, kt, vb, bias_tiles)


def _bc_prompt_kernel(l_ref, rest_ref, convw_ref, convb_ref, v128_ref, v256_ref, wlr_ref,
                      ob_ref, oc_ref, conv_out_ref, ssm_out_ref, gla_out_ref,
                      xpad_ref, sbd_ref, sg_ref, b_ref, oi_ref):
    del l_ref
    c = pl.program_id(1)
    q = CHUNK

    @pl.when(c == 0)
    def _():
        xpad_ref[0:8, :] = jnp.zeros((8, XBC_B), F32)
        sbd_ref[...] = jnp.zeros(sbd_ref.shape, F32)
        sg_ref[...] = jnp.zeros(sg_ref.shape, F32)

    xpad_ref[8:8 + q, :] = rest_ref[:, R_BX:R_BX + XBC_B]
    conv = convb_ref[...]
    for j in range(CONV_W):
        conv = conv + convw_ref[j:j + 1, :] * xpad_ref[5 + j:5 + j + q, :]
    xbc = _silu(conv)
    tail = xpad_ref[5 + q:8 + q, :]
    conv_out_ref[...] = tail
    xpad_ref[5:8, :] = tail

    row = _iota((q, q), 0)
    col = _iota((q, q), 1)
    causal = row >= col
    tril = jnp.where(causal, 1.0, 0.0).astype(BF16)
    lane1 = _iota((1, 128), 1)
    misc = rest_ref[:, R_MISC:R_MISC + 128]

    xs = xbc[:, 0:W_B]
    bmat = xbc[:, W_B:W_B + G_B * N_B].astype(BF16)
    cmat = xbc[:, W_B + G_B * N_B:XBC_B].astype(BF16)
    is_head = lane1 < H_B
    dt = jnp.where(is_head, _softplus(misc + v128_ref[0:1, :]), 0.0)
    a_neg = jnp.where(is_head, -jnp.exp(v128_ref[1:2, :]), 0.0)
    cum = _lmat_exact(tril, dt * a_neg)
    cum_t = cum.T
    ecum = jnp.exp(cum)
    wgt = jnp.exp(cum[q - 1:q, :] - cum) * dt
    expand = jnp.where(_iota((128, W_B), 0) == (_iota((128, W_B), 1) >> 6), 1.0, 0.0).astype(BF16)
    dt_e = _rmat_exact(dt, expand)
    ecum_e = _rmat_exact(ecum, expand)
    wgt_e = _rmat_exact(wgt, expand)
    grp_lane = _iota((q, 128), 1) >> 6
    zero_b = jnp.zeros_like(cmat)
    gmat = [lax.dot_general(jnp.where(grp_lane == g, cmat, zero_b), bmat, (((1,), (1,)), ((), ())),
                            preferred_element_type=F32) for g in range(G_B)]
    mcat = []
    for h in range(H_B):
        diff = cum[:, h:h + 1] - cum_t[h:h + 1, :]
        dec = jnp.exp(jnp.where(causal, diff, NEG))
        mcat.append((gmat[h // (H_B // G_B)] * dec).astype(BF16))
    mcat = jnp.concatenate(mcat, axis=1)
    head_lane = _iota((q, W_B), 1) >> 6
    dtx = xs * dt_e
    xbd = jnp.concatenate([jnp.where(head_lane == h, dtx, 0.0) for h in range(H_B)], axis=0).astype(BF16)
    state = sbd_ref[...]
    y = _dotf(mcat, xbd) + _dotf(cmat, state.astype(BF16)) * ecum_e
    upd = lax.dot_general(bmat, (xs * wgt_e).astype(BF16), (((0,), (0,)), ((), ())), preferred_element_type=F32)
    same_grp = (_iota((128, W_B), 0) >> 6) == (_iota((128, W_B), 1) >> 7)
    state = state * ecum_e[q - 1:q, :] + jnp.where(same_grp, upd, 0.0)
    sbd_ref[...] = state
    ssm_out_ref[...] = state
    y = y + v256_ref[0:1, :] * xs
    ob_ref[...] = (_rms(y * _silu(rest_ref[:, R_BZ:R_BZ + W_B])) * v256_ref[1:2, :]).astype(BF16)

    qg = rest_ref[:, R_CQ:R_CQ + 128] * (DK_C ** -0.5)
    kg = rest_ref[:, R_CK:R_CK + 128]
    vg = rest_ref[:, R_CV:R_CV + W_C]
    gk = _dotf(misc.astype(BF16), wlr_ref[...]) + v128_ref[2:3, :]
    gate = -_softplus(-gk) * (1.0 / GATE_NORM)
    bcum = _lmat_exact(tril, gate)
    b_last = bcum[q - 1:q, :]
    span = jnp.max(-b_last)
    qe = qg * jnp.exp(bcum)
    vhead = _iota((q, W_C), 1) >> 6

    @pl.when(span <= 80.0)
    def _():
        ke = kg * jnp.exp(-bcum)
        kbd = jnp.where((_iota((128, 4 * q), 0) >> 5) == (_iota((128, 4 * q), 1) >> 7),
                        jnp.tile(ke.T, (1, H_C)), 0.0).astype(BF16)
        att = _dotf(qe.astype(BF16), kbd)
        causal4 = _iota((q, H_C * q), 0) >= (_iota((q, H_C * q), 1) & (q - 1))
        att = jnp.where(causal4, att, 0.0).astype(BF16)
        vbd = jnp.concatenate([jnp.where(vhead == h, vg, 0.0) for h in range(H_C)], axis=0).astype(BF16)
        oi_ref[...] = _dotf(att, vbd)

    @pl.when(span > 80.0)
    def _():
        b_ref[...] = bcum
        ind = jnp.where((_iota((128, W_C), 0) >> 5) == (_iota((128, W_C), 1) >> 6), 1.0, 0.0).astype(BF16)
        trow = _iota((q, 128), 0)

        def body(grp, acc):
            base = pl.multiple_of(grp * 8, 8)
            ks8 = rest_ref[pl.ds(base, 8), R_CK:R_CK + 128]
            vs8 = rest_ref[pl.ds(base, 8), R_CV:R_CV + W_C]
            bs8 = b_ref[pl.ds(base, 8), :]
            for j in range(8):
                d = jnp.exp(jnp.minimum(bcum - bs8[j:j + 1, :], 0.0)) * qg * ks8[j:j + 1, :]
                d = jnp.where(trow >= base + j, d, 0.0)
                acc = acc + _dotf(d.astype(BF16), ind) * vs8[j:j + 1, :]
            return acc

        oi_ref[...] = lax.fori_loop(0, q // 8, body, jnp.zeros((q, W_C), F32))

    sg = sg_ref[...]
    o = oi_ref[...] + _dotf(qe.astype(BF16), sg.astype(BF16))
    k2 = (kg * jnp.exp(b_last - bcum)).astype(BF16)
    updg = lax.dot_general(k2, vg.astype(BF16), (((0,), (0,)), ((), ())), preferred_element_type=F32)
    same_head = (_iota((128, W_C), 0) >> 5) == (_iota((128, W_C), 1) >> 6)
    sg = sg * jnp.exp(bcum.T[:, q - 1:q]) + jnp.where(same_head, updg, 0.0)
    sg_ref[...] = sg
    gla_out_ref[...] = sg
    avg = jnp.where((_iota((W_C, W_C), 0) >> 6) == (_iota((W_C, W_C), 1) >> 6), 1.0, 0.0).astype(BF16)
    ms = _rmat_exact(o * o, avg) * (1.0 / DV_C)
    oc = o * lax.rsqrt(ms + EPS) * v256_ref[2:3, :] * _silu(rest_ref[:, R_CG:R_CG + W_C])
    oc_ref[...] = oc.astype(BF16)


def bc_mixers_prompt(lidx, rest, convw, convb3, vec128, vec256, wlr):
    b, seq, _ = rest.shape
    nc = seq // CHUNK
    par = lambda r, w: pl.BlockSpec((None, r, w), lambda bi, c, l: (l[0], 0, 0))
    st = lambda r, w: pl.BlockSpec((None, r, w), lambda bi, c, l: (bi, 0, 0))
    grid_spec = pltpu.PrefetchScalarGridSpec(
        num_scalar_prefetch=1, grid=(b, nc),
        in_specs=[pl.BlockSpec((None, CHUNK, W_REST), lambda bi, c, l: (bi, c, 0)),
                  par(CONV_W, XBC_B), par(1, XBC_B), par(8, 128), par(8, 256), par(128, 128)],
        out_specs=[pl.BlockSpec((None, CHUNK, W_B), lambda bi, c, l: (bi, c, 0)),
                   pl.BlockSpec((None, CHUNK, W_C), lambda bi, c, l: (bi, c, 0)),
                   st(CONV_W - 1, XBC_B), st(128, W_B), st(128, W_C)],
        scratch_shapes=[pltpu.VMEM((8 + CHUNK, XBC_B), F32), pltpu.VMEM((128, W_B), F32),
                        pltpu.VMEM((128, W_C), F32), pltpu.VMEM((CHUNK, 128), F32),
                        pltpu.VMEM((CHUNK, W_C), F32)])
    return pl.pallas_call(
        _bc_prompt_kernel,
        out_shape=[jax.ShapeDtypeStruct((b, seq, W_B), BF16), jax.ShapeDtypeStruct((b, seq, W_C), BF16),
                   jax.ShapeDtypeStruct((b, CONV_W - 1, XBC_B), F32),
                   jax.ShapeDtypeStruct((b, 128, W_B), F32), jax.ShapeDtypeStruct((b, 128, W_C), F32)],
        grid_spec=grid_spec,
        compiler_params=_cparams(("arbitrary", "arbitrary")),
        name="bc_mixers_prompt",
    )(lidx, rest, convw, convb3, vec128, vec256, wlr)


def _unpack_ssm_state(sbd):
    b = sbd.shape[0]
    s = sbd.reshape(b, G_B, N_B, H_B, P_B)
    per_head = [s[:, h // (H_B // G_B), :, h, :] for h in range(H_B)]
    return jnp.swapaxes(jnp.stack(per_head, axis=1), 2, 3)


def _unpack_gla_state(sg):
    b = sg.shape[0]
    s = sg.reshape(b, H_C, DK_C, H_C, DV_C)
    return jnp.stack([s[:, h, :, h, :] for h in range(H_C)], axis=1)


def _out_proj_kernel(l_ref, oa_ref, ob_ref, oc_ref, w_ref, x_ref, g1_ref, sh2_ref, sc2_ref, x1_ref, h2_ref):
    del l_ref
    mix = (_dotf(oa_ref[...], w_ref[0:W_A, :]) + _dotf(ob_ref[...], w_ref[W_A:W_A + W_B, :])
           + _dotf(oc_ref[...], w_ref[W_A + W_B:, :]))
    x1 = x_ref[...] + g1_ref[...] * mix
    x1_ref[...] = x1
    h2_ref[...] = (_rms(x1) * (1.0 + sc2_ref[...]) + sh2_ref[...]).astype(h2_ref.dtype)


def out_projection(lidx, oa, ob, oc, w_out, x, mod, tm, h2_dtype):
    g, r, d = x.shape
    per_row = mod.shape[2] > 1
    act = lambda w: pl.BlockSpec((None, tm, w), lambda gi, i, l: (gi, i, 0))
    grid_spec = pltpu.PrefetchScalarGridSpec(
        num_scalar_prefetch=1, grid=(g, r // tm),
        in_specs=[act(W_A), act(W_B), act(W_C),
                  pl.BlockSpec((None, d, d), lambda gi, i, l: (l[0], 0, 0)),
                  act(d), _mod_spec(tm, per_row, 2), _mod_spec(tm, per_row, 3), _mod_spec(tm, per_row, 4)],
        out_specs=[act(d), act(d)])
    return pl.pallas_call(
        _out_proj_kernel,
        out_shape=[jax.ShapeDtypeStruct((g, r, d), F32), jax.ShapeDtypeStruct((g, r, d), h2_dtype)],
        grid_spec=grid_spec,
        compiler_params=_cparams(("arbitrary", "arbitrary")),
        name="out_projection",
    )(lidx, oa, ob, oc, w_out, x, mod, mod, mod)


def _ffn_kernel(i_ref, h_ref, wg_ref, wu_ref, wd_ref, x1_ref, g2_ref, fg_ref, o_ref, acc_ref, *, final):
    del i_ref
    f = pl.program_id(2)

    @pl.when(f == 0)
    def _():
        acc_ref[...] = jnp.zeros(acc_ref.shape, F32)

    h = h_ref[...]
    a = (_silu(_dotf(h, wg_ref[...])) * _dotf(h, wu_ref[...])).astype(BF16)
    acc_ref[...] += _dotf(a, wd_ref[...])

    @pl.when(f == pl.num_programs(2) - 1)
    def _():
        x2 = x1_ref[...] + g2_ref[...] * acc_ref[...]
        o_ref[...] = _rms(x2) * fg_ref[...] if final else x2


def dense_ffn(iidx, lidx_mod, h2, wg, wu, wd, x1, mod, final_g, tm, tf, final):
    g, r, d = x1.shape
    per_row = mod.shape[2] > 1
    nf = D_FF // tf
    act = lambda w: pl.BlockSpec((None, tm, w), lambda gi, i, f, s: (gi, i, 0))
    if per_row:
        g2 = pl.BlockSpec((None, None, tm, d), lambda gi, i, f, s: (s[1], gi, i, 5))
    else:
        g2 = pl.BlockSpec((None, None, 1, d), lambda gi, i, f, s: (s[1], gi, 0, 5))
    grid_spec = pltpu.PrefetchScalarGridSpec(
        num_scalar_prefetch=1, grid=(g, r // tm, nf),
        in_specs=[act(d),
                  pl.BlockSpec((None, d, tf), lambda gi, i, f, s: (s[0], 0, f)),
                  pl.BlockSpec((None, d, tf), lambda gi, i, f, s: (s[0], 0, f)),
                  pl.BlockSpec((None, tf, d), lambda gi, i, f, s: (s[0], f, 0)),
                  act(d), g2, pl.BlockSpec((1, d), lambda gi, i, f, s: (0, 0))],
        out_specs=act(d),
        scratch_shapes=[pltpu.VMEM((tm, d), F32)])
    sidx = jnp.concatenate([iidx, lidx_mod])
    return pl.pallas_call(
        functools.partial(_ffn_kernel, final=final),
        out_shape=jax.ShapeDtypeStruct((g, r, d), F32),
        grid_spec=grid_spec,
        compiler_params=_cparams(("arbitrary", "arbitrary", "arbitrary")),
        name="dense_ffn",
    )(sidx, h2, wg, wu, wd, x1, mod, final_g)


def _route_kernel(i_ref, h_ref, rw_ref, tri_ref, idx_ref, gate_ref, cnt_ref, carry_ref):
    del i_ref
    i = pl.program_id(0)

    @pl.when(i == 0)
    def _():
        carry_ref[...] = jnp.zeros(carry_ref.shape, F32)

    hh, hl = _split2(h_ref[...])
    wh, wl = _split2(rw_ref[...])
    logits = _dotf(hh, wh) + _dotf(hl, wh) + _dotf(hh, wl)
    lane = _iota(logits.shape, 1).astype(F32)
    logits = jnp.where(lane < N_EXPERTS, logits, NEG)
    m1 = jnp.max(logits, axis=1, keepdims=True)
    i1 = jnp.min(jnp.where(logits == m1, lane, 128.0), axis=1, keepdims=True)
    rest = jnp.where(lane == i1, NEG, logits)
    m2 = jnp.max(rest, axis=1, keepdims=True)
    i2 = jnp.min(jnp.where(rest == m2, lane, 128.0), axis=1, keepdims=True)
    e = jnp.exp(m2 - m1)
    g1 = 1.0 / (1.0 + e)
    sel = jnp.where(lane == i1, 1.0, 0.0) + jnp.where(lane == i2, 1.0, 0.0)
    before = _dotf(tri_ref[...], sel.astype(BF16)) + carry_ref[...]
    r1 = jnp.sum(jnp.where(lane == i1, before, 0.0), axis=1, keepdims=True)
    r2 = jnp.sum(jnp.where(lane == i2, before, 0.0), axis=1, keepdims=True)
    carry_ref[...] = carry_ref[...] + jnp.sum(sel, axis=0, keepdims=True)
    lane8 = _iota(idx_ref.shape, 1).astype(F32)
    idx_ref[...] = jnp.where(lane8 == 0.0, i1, jnp.where(lane8 == 1.0, i2, jnp.where(lane8 == 2.0, r1, r2))).astype(I32)
    gate_ref[...] = jnp.where(lane8 == 0.0, g1, e * g1)
    cnt_ref[...] = jnp.broadcast_to(carry_ref[...], cnt_ref.shape)


def moe_route(iidx, h2, router_pad, tm):
    t, d = h2.shape
    tri = jnp.asarray(np.tril(np.ones((tm, tm), np.float32), -1), BF16)
    grid_spec = pltpu.PrefetchScalarGridSpec(
        num_scalar_prefetch=1, grid=(t // tm,),
        in_specs=[pl.BlockSpec((tm, d), lambda i, s: (i, 0)),
                  pl.BlockSpec((None, d, 128), lambda i, s: (s[0], 0, 0)),
                  pl.BlockSpec((tm, tm), lambda i, s: (0, 0))],
        out_specs=[pl.BlockSpec((tm, 8), lambda i, s: (i, 0)), pl.BlockSpec((tm, 8), lambda i, s: (i, 0)),
                   pl.BlockSpec((8, 128), lambda i, s: (0, 0))],
        scratch_shapes=[pltpu.VMEM((1, 128), F32)])
    return pl.pallas_call(
        _route_kernel,
        out_shape=[jax.ShapeDtypeStruct((t, 8), I32), jax.ShapeDtypeStruct((t, 8), F32),
                   jax.ShapeDtypeStruct((8, 128), F32)],
        grid_spec=grid_spec,
        compiler_params=_cparams(("arbitrary",)),
        name="moe_route",
    )(iidx, h2, router_pad, tri)


def _row_copy(src_ref, src_row, dst_ref, dst_row, sem):
    return pltpu.make_async_copy(src_ref.at[pl.ds(src_row, 1), :], dst_ref.at[pl.ds(dst_row, 1), :], sem)


def _dispatch_kernel(dest_ref, h_ref, xb_in_ref, xb_ref, sem):
    del xb_in_ref
    tm = h_ref.shape[0]

    def start(r, carry):
        for k in range(TOP_K):
            _row_copy(h_ref, r, xb_ref, dest_ref[0, TOP_K * r + k], sem).start()
        return carry

    lax.fori_loop(0, tm, start, 0, unroll=8)
    for k in range(TOP_K):
        pltpu.make_async_copy(h_ref, xb_ref.at[pl.ds(0, tm), :], sem).wait()


def moe_dispatch(dest3, h2, n_rows, tm):
    t, d = h2.shape
    zeros = jnp.zeros((n_rows, d), F32)
    return pl.pallas_call(
        _dispatch_kernel,
        out_shape=jax.ShapeDtypeStruct((n_rows, d), F32),
        grid=(t // tm,),
        in_specs=[pl.BlockSpec((None, 1, TOP_K * tm), lambda i: (i, 0, 0), memory_space=pltpu.SMEM),
                  pl.BlockSpec((tm, d), lambda i: (i, 0)),
                  pl.BlockSpec(memory_space=pl.ANY)],
        out_specs=pl.BlockSpec(memory_space=pl.ANY),
        scratch_shapes=[pltpu.SemaphoreType.DMA(())],
        input_output_aliases={2: 0},
        compiler_params=pltpu.CompilerParams(dimension_semantics=("arbitrary",), vmem_limit_bytes=VMEM_LIMIT,
                                             has_side_effects=True),
        name="moe_dispatch",
    )(dest3, h2, zeros)


def _expert_kernel(s_ref, x_ref, wg_ref, wu_ref, wd_ref, y_ref, xb_ref, acc_ref):
    i = pl.program_id(0)
    f = pl.program_id(1)
    nb = pl.num_programs(0)

    @pl.when(i < s_ref[nb + 1])
    def _():
        @pl.when(f == 0)
        def _():
            xb_ref[...] = x_ref[...].astype(BF16)
            acc_ref[...] = jnp.zeros(acc_ref.shape, F32)

        x = xb_ref[...]
        a = (_silu(_dotf(x, wg_ref[...])) * _dotf(x, wu_ref[...])).astype(BF16)
        acc_ref[...] += _dotf(a, wd_ref[...])

        @pl.when(f == pl.num_programs(1) - 1)
        def _():
            y_ref[...] = acc_ref[...]

    @pl.when(jnp.logical_and(i >= s_ref[nb + 1], f == pl.num_programs(1) - 1))
    def _():
        y_ref[...] = jnp.zeros(y_ref.shape, F32)


def moe_experts(sidx, xb, wg, wu, wd, blk, tf):
    n_rows, d = xb.shape
    nb = n_rows // blk
    nf = D_FF // tf

    def row_map(i, f, s):
        return (jnp.minimum(i, s[nb + 1] - 1), 0)

    def f_of(i, f, s):
        return jnp.where(i < s[nb + 1], f, nf - 1)

    grid_spec = pltpu.PrefetchScalarGridSpec(
        num_scalar_prefetch=1, grid=(nb, nf),
        in_specs=[pl.BlockSpec((blk, d), row_map),
                  pl.BlockSpec((None, None, d, tf), lambda i, f, s: (s[nb], s[i], 0, f_of(i, f, s))),
                  pl.BlockSpec((None, None, d, tf), lambda i, f, s: (s[nb], s[i], 0, f_of(i, f, s))),
                  pl.BlockSpec((None, None, tf, d), lambda i, f, s: (s[nb], s[i], f_of(i, f, s), 0))],
        out_specs=pl.BlockSpec((blk, d), lambda i, f, s: (i, 0)),
        scratch_shapes=[pltpu.VMEM((blk, d), BF16), pltpu.VMEM((blk, d), F32)])
    return pl.pallas_call(
        _expert_kernel,
        out_shape=jax.ShapeDtypeStruct((n_rows, d), F32),
        grid_spec=grid_spec,
        compiler_params=_cparams(("arbitrary", "arbitrary")),
        name="moe_experts",
    )(sidx, xb, wg, wu, wd)


def _combine_kernel(dest_ref, yb_ref, gate_ref, x1_ref, g2_ref, fg_ref, o_ref, buf_ref, sem, *, final):
    tm = x1_ref.shape[0]

    def start(r, carry):
        for k in range(TOP_K):
            _row_copy(yb_ref, dest_ref[0, TOP_K * r + k], buf_ref.at[k], r, sem).start()
        return carry

    lax.fori_loop(0, tm, start, 0, unroll=8)
    for k in range(TOP_K):
        pltpu.make_async_copy(yb_ref.at[pl.ds(0, tm), :], buf_ref.at[k], sem).wait()
    gate = gate_ref[...]
    f = gate[:, 0:1] * buf_ref[0] + gate[:, 1:2] * buf_ref[1]
    x2 = x1_ref[...] + g2_ref[...] * f
    o_ref[...] = _rms(x2) * fg_ref[...] if final else x2


def moe_combine(dest3, yb, gates, x1, mod, lidx_mod, final_g, tm, final):
    g, r, d = x1.shape
    per_row = mod.shape[1] > 1
    nt = r // tm
    del lidx_mod
    act = pl.BlockSpec((None, tm, d), lambda gi, i: (gi, i, 0))
    return pl.pallas_call(
        functools.partial(_combine_kernel, final=final),
        out_shape=jax.ShapeDtypeStruct((g, r, d), F32),
        grid=(g, nt),
        in_specs=[pl.BlockSpec((None, 1, TOP_K * tm), lambda gi, i: (gi * nt + i, 0, 0), memory_space=pltpu.SMEM),
                  pl.BlockSpec(memory_space=pl.ANY),
                  pl.BlockSpec((tm, 8), lambda gi, i: (gi * nt + i, 0)),
                  act,
                  (pl.BlockSpec((None, tm, d), lambda gi, i: (gi, i, 0)) if per_row
                   else pl.BlockSpec((None, 1, d), lambda gi, i: (gi, 0, 0))),
                  pl.BlockSpec((1, d), lambda gi, i: (0, 0))],
        out_specs=act,
        scratch_shapes=[pltpu.VMEM((TOP_K, tm, d), F32), pltpu.SemaphoreType.DMA(())],
        compiler_params=_cparams(("arbitrary", "arbitrary")),
        name="moe_combine",
    )(dest3, yb, gates, x1, mod, final_g)


def moe_ffn(layer_slot, h2, x1, mod_g2, router_pad, wg, wu, wd, final_g, tm, blk, tf, final):
    g, r, d = x1.shape
    t = g * r
    h2f = h2.reshape(t, d)
    iidx = jnp.array([layer_slot], I32)
    idx, gates, counts = moe_route(iidx, h2f, router_pad, tm)
    counts = counts[0, :N_EXPERTS].astype(I32)
    padded = (counts + blk - 1) // blk * blk
    pad_end = jnp.cumsum(padded)
    pad_start = pad_end - padded
    first = sum(jnp.where(idx[:, 0:TOP_K] == e, pad_start[e], 0) for e in range(N_EXPERTS))
    dest = first + idx[:, TOP_K:2 * TOP_K]
    nb = -(-(t * TOP_K) // blk) + N_EXPERTS
    n_used = pad_end[-1] // blk
    blk_e = jnp.minimum(jnp.searchsorted(pad_end, jnp.arange(nb, dtype=I32) * blk, side='right'), N_EXPERTS - 1)
    blk_e = jnp.where(jnp.arange(nb) < n_used, blk_e, blk_e[jnp.maximum(n_used - 1, 0)]).astype(I32)
    dest3 = dest.astype(I32).reshape(t // tm, 1, TOP_K * tm)
    xb = moe_dispatch(dest3, h2f, nb * blk, tm)
    sidx = jnp.concatenate([blk_e, iidx, n_used.astype(I32)[None]])
    yb = moe_experts(sidx, xb, wg, wu, wd, blk, tf)
    return moe_combine(dest3, yb, gates, x1, mod_g2, None, final_g, tm, final)


def _attn_decode_kernel(pt_ref, lam_ref, lqk_ref, g_ref, q_ref, kn_ref, vn_ref, tab_ref, *refs, pps, n_pages):
    k_refs = refs[:pps]
    v_refs = refs[pps:2 * pps]
    o_ref, m_ref, s_ref, acc_ref = refs[2 * pps:]
    step = pl.program_id(1)
    layer = pt_ref[pt_ref.shape[0] - 1]
    rows = 2 * H_A

    @pl.when(step == 0)
    def _():
        m_ref[...] = jnp.full(m_ref.shape, NEG, F32)
        s_ref[...] = jnp.zeros(s_ref.shape, F32)
        acc_ref[...] = jnp.zeros(acc_ref.shape, F32)

    qrow = jnp.broadcast_to(q_ref[...].astype(F32), (rows, W_A))
    own = (_iota((rows, W_A), 1) >> 6) == _iota((rows, W_A), 0)
    q8 = jnp.where(own, qrow, 0.0).astype(BF16)

    def accumulate(s, pv):
        m_old = m_ref[...]
        m_new = jnp.maximum(m_old, jnp.max(s, axis=1, keepdims=True))
        alpha = jnp.exp(m_old - m_new)
        p = jnp.exp(s - m_new)
        s_ref[...] = alpha * s_ref[...] + jnp.sum(p, axis=1, keepdims=True)
        acc_ref[...] = alpha * acc_ref[...] + pv(p)
        m_ref[...] = m_new

    scores = [_dotf(q8, k_refs[i][...].astype(BF16)) for i in range(pps)]
    scores[pps - 1] = jnp.where(step == pl.num_programs(1) - 1, scores[pps - 1] + tab_ref[0:rows, :], scores[pps - 1])
    s_all = jnp.concatenate(scores, axis=1)

    def value_page(i):
        heads = [v_refs[i][pl.ds(h, PAGE, stride=H_A), :] for h in range(H_A)]
        return jnp.concatenate(heads, axis=1).astype(BF16)

    def pv_pages(p):
        out = _dotf(p[:, 0:PAGE].astype(BF16), value_page(0))
        for i in range(1, pps):
            out = out + _dotf(p[:, i * PAGE:(i + 1) * PAGE].astype(BF16), value_page(i))
        return out

    accumulate(s_all, pv_pages)

    @pl.when(step == pl.num_programs(1) - 1)
    def _():
        kn = kn_ref[...].astype(BF16).astype(F32)
        vn = vn_ref[...].astype(BF16).astype(F32)
        s_new = jnp.sum(q8.astype(F32) * kn, axis=1, keepdims=True) + tab_ref[rows:2 * rows, 0:1]
        accumulate(s_new, lambda p: p.astype(BF16).astype(F32) * vn)
        o = acc_ref[...] * (1.0 / s_ref[...])
        lam = _lambda_value(lqk_ref[...], lam_ref[2 * layer])
        outs = []
        for h in range(H_A):
            blkh = o[:, h * DV_A:(h + 1) * DV_A]
            oh = blkh[2 * h:2 * h + 1, :] - lam * blkh[2 * h + 1:2 * h + 2, :]
            outs.append(_rms(oh) * g_ref[...] * lam_ref[2 * layer + 1])
        o_ref[...] = jnp.concatenate(outs, axis=1).astype(BF16)


def attention_decode(sidx, lam_consts, lambda_qk, subln_g3, q, k_new, v_new, tab, cache_k4, cache_v4, n_pages, pps):
    bd = q.shape[0]
    nl = sidx.shape[0] - 1
    steps = n_pages // pps
    row = pl.BlockSpec((None, 1, W_A), lambda b, s, pt: (b, 0, 0))

    def page_spec(i):
        return pl.BlockSpec((None, None, W_A, PAGE),
                            lambda b, s, pt: (pt[nl], pt[b * n_pages + s * pps + i], 0, 0))

    grid_spec = pltpu.PrefetchScalarGridSpec(
        num_scalar_prefetch=1, grid=(bd, steps),
        in_specs=[pl.BlockSpec(memory_space=pltpu.SMEM),
                  pl.BlockSpec((None, 4, DK_A), lambda b, s, pt: (pt[nl], 0, 0)),
                  pl.BlockSpec((None, 1, DV_A), lambda b, s, pt: (pt[nl], 0, 0)),
                  row, row, row,
                  pl.BlockSpec((16, PAGE), lambda b, s, pt: (0, 0))]
                 + [page_spec(i) for i in range(pps)] + [page_spec(i) for i in range(pps)],
        out_specs=row,
        scratch_shapes=[pltpu.VMEM((2 * H_A, 1), F32), pltpu.VMEM((2 * H_A, 1), F32),
                        pltpu.VMEM((2 * H_A, W_A), F32)])
    return pl.pallas_call(
        functools.partial(_attn_decode_kernel, pps=pps, n_pages=n_pages),
        out_shape=jax.ShapeDtypeStruct((bd, 1, W_A), BF16),
        grid_spec=grid_spec,
        compiler_params=_cparams(("arbitrary", "arbitrary")),
        name="attention_decode",
    )(sidx, lam_consts, lambda_qk, subln_g3, q, k_new, v_new, tab, *([cache_k4] * pps), *([cache_v4] * pps))


def _bc_decode_pre_kernel(l_ref, rest_ref, buf_ref, convw_ref, convb_ref, v128_ref, wlr_ref,
                          xbc_ref, nbuf_ref, dt_ref, dec_ref, eg_ref):
    del l_ref
    u = rest_ref[:, R_BX:R_BX + XBC_B]
    conv = convb_ref[...] + convw_ref[CONV_W - 1:CONV_W, :] * u
    for j in range(CONV_W - 1):
        conv = conv + convw_ref[j:j + 1, :] * buf_ref[j]
    xbc_ref[...] = _silu(conv)
    for j in range(CONV_W - 2):
        nbuf_ref[j] = buf_ref[j + 1]
    nbuf_ref[CONV_W - 2] = u
    misc = rest_ref[:, R_MISC:R_MISC + 128]
    dt = _softplus(misc + v128_ref[0:1, :])
    dt_ref[...] = dt
    dec_ref[...] = jnp.exp(dt * (-jnp.exp(v128_ref[1:2, :])))
    gk = _dotf(misc.astype(BF16), wlr_ref[...]) + v128_ref[2:3, :]
    eg_ref[...] = jnp.exp(-_softplus(-gk) * (1.0 / GATE_NORM))


def bc_decode_pre(lidx, rest, conv_state_t, convw, convb3, vec128, wlr):
    bd = rest.shape[0]
    par = lambda r, w: pl.BlockSpec((None, r, w), lambda i, l: (l[0], 0, 0))
    full = lambda w: pl.BlockSpec((bd, w), lambda i, l: (0, 0))
    grid_spec = pltpu.PrefetchScalarGridSpec(
        num_scalar_prefetch=1, grid=(1,),
        in_specs=[full(W_REST),
                  pl.BlockSpec((None, CONV_W - 1, bd, XBC_B), lambda i, l: (l[0], 0, 0, 0)),
                  par(CONV_W, XBC_B), par(1, XBC_B), par(8, 128), par(128, 128)],
        out_specs=[full(XBC_B), pl.BlockSpec((CONV_W - 1, bd, XBC_B), lambda i, l: (0, 0, 0)),
                   full(128), full(128), full(128)])
    return pl.pallas_call(
        _bc_decode_pre_kernel,
        out_shape=[jax.ShapeDtypeStruct((bd, XBC_B), F32), jax.ShapeDtypeStruct((CONV_W - 1, bd, XBC_B), F32),
                   jax.ShapeDtypeStruct((bd, 128), F32), jax.ShapeDtypeStruct((bd, 128), F32),
                   jax.ShapeDtypeStruct((bd, 128), F32)],
        grid_spec=grid_spec,
        compiler_params=_cparams(("arbitrary",)),
        name="bc_decode_pre",
    )(lidx, rest, conv_state_t, convw, convb3, vec128, wlr)


def _bc_decode_state_kernel(l_ref, ssm_ref, gla_ref, x4_ref, b4_ref, c4_ref, dt4_ref, dec4_ref, bz4_ref,
                            dsk_ref, ng_ref, q4_ref, k4_ref, v4_ref, eg4_ref, cg4_ref, gng_ref,
                            ssm_out_ref, gla_out_ref, ob4_ref, oc4_ref):
    del l_ref
    x4 = x4_ref[...]
    s = ssm_ref[...] * dec4_ref[...] + (dt4_ref[...] * x4) * b4_ref[...]
    ssm_out_ref[...] = s
    y = jnp.sum(c4_ref[...] * s, axis=3, keepdims=True) + dsk_ref[...] * x4
    yg = y * _silu(bz4_ref[...])
    ms = jnp.sum(jnp.sum(yg * yg, axis=2, keepdims=True), axis=1, keepdims=True) * (1.0 / W_B)
    ob4_ref[...] = yg * lax.rsqrt(ms + EPS) * ng_ref[...]
    sg = gla_ref[...] * eg4_ref[...] + k4_ref[...] * v4_ref[...]
    gla_out_ref[...] = sg
    o = jnp.sum(q4_ref[...] * sg, axis=2, keepdims=True)
    oc4_ref[...] = _rms(o) * gng_ref[...] * _silu(cg4_ref[...])


def bc_decode_state(lidx, state_ssm, state_gla, x4, b4, c4, dt4, dec4, bz4, dsk4, ng4, q4, k4, v4, eg4, cg4, gng4):
    bd = x4.shape[0]

    def full(shape):
        n = len(shape)
        return pl.BlockSpec(shape, lambda i, l: (0,) * n)

    def layer(shape):
        n = len(shape)
        return pl.BlockSpec((None,) + shape, lambda i, l: (l[0],) + (0,) * n)

    s_ssm = (bd, H_B, P_B, N_B)
    s_gla = (bd, H_C, DK_C, DV_C)
    col_b = (bd, H_B, P_B, 1)
    row_b = (bd, H_B, 1, N_B)
    one_b = (bd, H_B, 1, 1)
    col_c = (bd, H_C, DK_C, 1)
    row_c = (bd, H_C, 1, DV_C)
    grid_spec = pltpu.PrefetchScalarGridSpec(
        num_scalar_prefetch=1, grid=(1,),
        in_specs=[layer(s_ssm), layer(s_gla), full(col_b), full(row_b), full(row_b), full(one_b), full(one_b),
                  full(col_b), layer((1, H_B, 1, 1)), layer((1, H_B, P_B, 1)),
                  full(col_c), full(col_c), full(row_c), full(col_c), full(row_c), layer((1, 1, 1, DV_C))],
        out_specs=[full(s_ssm), full(s_gla), full(col_b), full(row_c)])
    return pl.pallas_call(
        _bc_decode_state_kernel,
        out_shape=[jax.ShapeDtypeStruct(s_ssm, F32), jax.ShapeDtypeStruct(s_gla, F32),
                   jax.ShapeDtypeStruct(col_b, F32), jax.ShapeDtypeStruct(row_c, F32)],
        grid_spec=grid_spec,
        compiler_params=_cparams(("arbitrary",)),
        name="bc_decode_state",
    )(lidx, state_ssm, state_gla, x4, b4, c4, dt4, dec4, bz4, dsk4, ng4, q4, k4, v4, eg4, cg4, gng4)


def _pack_params(w_in, rel_bias, subln_g, conv_b, dt_bias, a_log, d_skip, ssd_norm_g, gla_w_lr, gla_b_lr,
                 gla_norm_g, router_w):
    depth = w_in.shape[0]
    pad = jnp.zeros(w_in.shape[:2] + (D_PACK - D_IN,), w_in.dtype)
    w_pack = jnp.concatenate([w_in[..., :OFF_BDT], w_in[..., OFF_CQ:OFF_CLR], w_in[..., OFF_BDT:OFF_CQ],
                              w_in[..., OFF_CLR:], pad], axis=-1).astype(BF16)
    vec128 = jnp.zeros((depth, 8, 128), F32)
    vec128 = vec128.at[:, 0, :H_B].set(dt_bias).at[:, 1, :H_B].set(a_log).at[:, 2, :].set(gla_b_lr)
    vec256 = jnp.zeros((depth, 8, 256), F32)
    vec256 = (vec256.at[:, 0, :].set(jnp.repeat(d_skip, P_B, axis=1)).at[:, 1, :].set(ssd_norm_g)
              .at[:, 2, :].set(jnp.tile(gla_norm_g, (1, H_C))))
    wlr = jnp.zeros((depth, 128, 128), F32).at[:, MISC_LR:MISC_LR + GATE_RANK, :].set(gla_w_lr).astype(BF16)
    router_pad = jnp.zeros(router_w.shape[:2] + (128,), F32).at[..., :N_EXPERTS].set(router_w)
    lam_init = [0.8 - 0.6 * math.exp(-0.3 * l) for l in range(depth)]
    lam_consts = jnp.asarray(np.array([[li, 1.0 - li] for li in lam_init], np.float32).reshape(-1))
    return dict(w_pack=w_pack, vec128=vec128, vec256=vec256, wlr=wlr, router_pad=router_pad,
                lam_consts=lam_consts, subln_g3=subln_g.reshape(depth, 1, DV_A),
                convb3=conv_b.reshape(depth, 1, XBC_B))


def _pick(n, pref):
    if n <= pref:
        return n
    t = pref
    while n % t:
        t //= 2
    return t


def _channel_mixer(l, depth, h2, x1, mod, pk, wts, tm, moe_blk, final):
    i = l // 2
    lmod = jnp.array([l], I32)
    if l % 2 == 0:
        return dense_ffn(jnp.array([i], I32), lmod, h2, wts['ffn_g'], wts['ffn_u'], wts['ffn_d'], x1, mod,
                         wts['final_g'], tm, _pick(D_FF, 1408), final)
    mod_g2 = mod[l, :, :, 5 * D_MODEL:6 * D_MODEL]
    return moe_ffn(i, h2, x1, mod_g2, pk['router_pad'], wts['moe_g'], wts['moe_u'], wts['moe_d'], wts['final_g'],
                   _pick(x1.shape[0] * x1.shape[1], 256), moe_blk, _pick(D_FF, 1408), final)


def _run_prompt(x, mod, pk, wts, lambda_qk, conv_w, rel_bias):
    b, seq, d = x.shape
    depth = mod.shape[0]
    tm = _pick(seq, 512)
    blk = tm
    bias_tiles = attention_bias_tiles(rel_bias, blk)
    ks, vs, convs, ssms, glas = [], [], [], [], []
    for l in range(depth):
        lidx = jnp.array([l], I32)
        q, k, v, rest, kt, vb = in_projection(lidx, x, mod, pk['w_pack'], tm, True)
        oa = attention_prompt(lidx, pk['lam_consts'], lambda_qk, pk['subln_g3'], q, kt, vb, bias_tiles, blk)
        ob, oc, conv_s, ssm_s, gla_s = bc_mixers_prompt(lidx, rest, conv_w, pk['convb3'], pk['vec128'],
                                                        pk['vec256'], pk['wlr'])
        moe = l % 2 == 1
        x1, h2 = out_projection(lidx, oa, ob, oc, wts['w_out'], x, mod, tm, F32 if moe else BF16)
        x = _channel_mixer(l, depth, h2, x1, mod, pk, wts, tm, _pick(b * seq, 512), l == depth - 1)
        ks.append(k.reshape(b, seq, 2 * H_A, DK_A))
        vs.append(v.reshape(b, seq, H_A, DV_A))
        convs.append(conv_s)
        ssms.append(_unpack_ssm_state(ssm_s))
        glas.append(_unpack_gla_state(gla_s))
    return x, jnp.stack(ks), jnp.stack(vs), jnp.stack(convs), jnp.stack(ssms), jnp.stack(glas)


def _run_decode(x, mod, pk, wts, lambda_qk, conv_w, rel_bias, cache_k, cache_v, page_table,
                state_conv, state_ssm, state_gla, d_skip, ssd_norm_g, gla_norm_g):
    bd = x.shape[0]
    depth = mod.shape[0]
    n_pages = page_table.shape[1]
    past = n_pages * PAGE
    pps = _pick(n_pages, 16)
    xg = x.reshape(1, bd, D_MODEL)
    pool = cache_k.shape[1]
    assert PAGE >= T5_MAX_DIST
    cache_k4 = jnp.transpose(cache_k, (0, 1, 3, 4, 2)).reshape(depth, pool, 2 * H_A * DK_A, PAGE)
    cache_v4 = cache_v.reshape(depth, pool, PAGE * H_A, DV_A)
    table = _t5_table(rel_bias, PAGE + 1)
    dist = past - ((n_pages - 1) * PAGE + jnp.arange(PAGE))
    tab = jnp.concatenate([jnp.repeat(table[:, dist], 2, axis=0),
                           jnp.broadcast_to(jnp.repeat(table[:, 0], 2)[:, None], (2 * H_A, PAGE))], axis=0)
    conv_t = jnp.swapaxes(state_conv, 1, 2)
    grp = np.arange(H_B) // (H_B // G_B)
    dsk4 = d_skip.reshape(depth, 1, H_B, 1, 1)
    ng4 = ssd_norm_g.reshape(depth, 1, H_B, P_B, 1)
    gng4 = gla_norm_g.reshape(depth, 1, 1, 1, DV_C)
    pt_flat = page_table.reshape(-1).astype(I32)
    ks, vs, convs, ssms, glas = [], [], [], [], []
    for l in range(depth):
        lidx = jnp.array([l], I32)
        q, k, v, rest = in_projection(lidx, xg, mod, pk['w_pack'], bd, False)
        sidx = jnp.concatenate([pt_flat, lidx])
        oa = attention_decode(sidx, pk['lam_consts'], lambda_qk, pk['subln_g3'], q.reshape(bd, 1, W_A),
                              k.reshape(bd, 1, W_A), v.reshape(bd, 1, W_A), tab, cache_k4, cache_v4, n_pages, pps)
        rest2 = rest.reshape(bd, W_REST)
        xbc, nbuf, dt, dec, eg = bc_decode_pre(lidx, rest2, conv_t, conv_w, pk['convb3'], pk['vec128'], pk['wlr'])
        x4 = xbc[:, :W_B].reshape(bd, H_B, P_B, 1)
        b4 = xbc[:, W_B:W_B + G_B * N_B].reshape(bd, G_B, 1, N_B)[:, grp]
        c4 = xbc[:, W_B + G_B * N_B:].reshape(bd, G_B, 1, N_B)[:, grp]
        dt4 = dt[:, :H_B].reshape(bd, H_B, 1, 1)
        dec4 = dec[:, :H_B].reshape(bd, H_B, 1, 1)
        bz4 = rest2[:, R_BZ:R_BZ + W_B].reshape(bd, H_B, P_B, 1)
        q4 = (rest2[:, R_CQ:R_CQ + 128] * (DK_C ** -0.5)).reshape(bd, H_C, DK_C, 1)
        k4 = rest2[:, R_CK:R_CK + 128].reshape(bd, H_C, DK_C, 1)
        v4 = rest2[:, R_CV:R_CV + W_C].reshape(bd, H_C, 1, DV_C)
        eg4 = eg.reshape(bd, H_C, DK_C, 1)
        cg4 = rest2[:, R_CG:R_CG + W_C].reshape(bd, H_C, 1, DV_C)
        ssm_s, gla_s, ob4, oc4 = bc_decode_state(lidx, state_ssm, state_gla, x4, b4, c4, dt4, dec4, bz4, dsk4, ng4,
                                                 q4, k4, v4, eg4, cg4, gng4)
        ob = ob4.reshape(1, bd, W_B).astype(BF16)
        oc = oc4.reshape(1, bd, W_C).astype(BF16)
        moe = l % 2 == 1
        x1, h2 = out_projection(lidx, oa.reshape(1, bd, W_A), ob, oc, wts['w_out'], xg, mod, bd, F32 if moe else BF16)
        xg = _channel_mixer(l, depth, h2, x1, mod, pk, wts, bd, 64, l == depth - 1)
        ks.append(k.reshape(bd, 1, 2 * H_A, DK_A))
        vs.append(v.reshape(bd, 1, H_A, DV_A))
        convs.append(jnp.swapaxes(nbuf, 0, 1))
        ssms.append(ssm_s)
        glas.append(gla_s)
    return (xg.reshape(bd, 1, D_MODEL), jnp.stack(ks), jnp.stack(vs), jnp.stack(convs), jnp.stack(ssms),
            jnp.stack(glas))


def kernel(x_prompt, x_sample, c_prompt, c_sample, cache_k, cache_v, page_table, state_conv, state_ssm, state_gla,
           w_ada, b_ada, w_in, w_out, rel_bias, lambda_qk, subln_g, conv_w, conv_b, dt_bias, a_log, d_skip,
           ssd_norm_g, gla_w_lr, gla_b_lr, gla_norm_g, ffn_w_gate, ffn_w_up, ffn_w_down, router_w,
           moe_w_gate, moe_w_up, moe_w_down, final_norm_g):
    depth = w_in.shape[0]
    bp, bd = c_prompt.shape[0], c_sample.shape[0]
    pk = _pack_params(w_in, rel_bias, subln_g, conv_b, dt_bias, a_log, d_skip, ssd_norm_g, gla_w_lr, gla_b_lr,
                      gla_norm_g, router_w)
    wts = dict(w_out=w_out.astype(BF16), ffn_g=ffn_w_gate.astype(BF16), ffn_u=ffn_w_up.astype(BF16),
               ffn_d=ffn_w_down.astype(BF16), moe_g=moe_w_gate.astype(BF16), moe_u=moe_w_up.astype(BF16),
               moe_d=moe_w_down.astype(BF16), final_g=final_norm_g.reshape(1, D_MODEL))
    n_c = bp + bd
    n_cp = -(-n_c // 8) * 8
    c_all = jnp.concatenate([c_prompt, c_sample, jnp.zeros((n_cp - n_c, D_MODEL), F32)], axis=0)
    mod = ada_modulation(c_all, w_ada, b_ada)
    mod_p = mod[:, :bp].reshape(depth, bp, 1, 6 * D_MODEL)
    mod_d = mod[:, bp:n_c].reshape(depth, 1, bd, 6 * D_MODEL)
    yp, kp, vp, convp, ssmp, glap = _run_prompt(x_prompt, mod_p, pk, wts, lambda_qk, conv_w, rel_bias)
    yd, kd, vd, convd, ssmd, glad = _run_decode(x_sample, mod_d, pk, wts, lambda_qk, conv_w, rel_bias, cache_k,
                                                cache_v, page_table, state_conv, state_ssm, state_gla, d_skip,
                                                ssd_norm_g, gla_norm_g)
    return (yp, yd, kp, vp, convp, ssmp, glap, kd, vd, convd, ssmd, glad)
```

```python
import functools
import math

import numpy as np
import jax
import jax.numpy as jnp
from jax import lax
from jax.experimental import pallas as pl
from jax.experimental.pallas import tpu as pltpu

F32 = jnp.float32
BF16 = jnp.bfloat16
I32 = jnp.int32

D_MODEL = 1024
H_A, DK_A, DV_A = 4, 64, 128
W_A = H_A * DV_A
H_B, P_B, G_B, N_B, CONV_W = 4, 64, 2, 64, 4
W_B = H_B * P_B
XBC_B = W_B + 2 * G_B * N_B
H_C, DK_C, DV_C = 4, 32, 64
W_C = H_C * DV_C
GATE_RANK, GATE_NORM = 16, 16.0
T5_BUCKETS, T5_MAX_DIST = 32, 128
D_FF, N_EXPERTS, TOP_K = 2816, 8, 2
EPS = 1e-6
PAGE = 128
CHUNK = 128
OFF_AQ = 0
OFF_AK = OFF_AQ + 2 * H_A * DK_A
OFF_AV = OFF_AK + 2 * H_A * DK_A
OFF_BZ = OFF_AV + W_A
OFF_BX = OFF_BZ + W_B
OFF_BDT = OFF_BX + XBC_B
OFF_CQ = OFF_BDT + H_B
OFF_CK = OFF_CQ + H_C * DK_C
OFF_CV = OFF_CK + H_C * DK_C
OFF_CG = OFF_CV + W_C
OFF_CLR = OFF_CG + W_C
D_IN = OFF_CLR + GATE_RANK
PK_REST = 3 * W_A
R_BZ, R_BX, R_CQ, R_CK, R_CV, R_CG, R_MISC = 0, 256, 768, 896, 1024, 1280, 1536
W_REST = R_MISC + 128
D_PACK = PK_REST + W_REST
MISC_DT, MISC_LR = 0, H_B
NEG = -1e30
VMEM_LIMIT = 56 * 1024 * 1024


def _cparams(sem):
    return pltpu.CompilerParams(dimension_semantics=sem, vmem_limit_bytes=VMEM_LIMIT)


def _rms(x):
    return x * lax.rsqrt(jnp.mean(x * x, axis=-1, keepdims=True) + EPS)


def _silu(x):
    return x * jax.nn.sigmoid(x)


def _softplus(x):
    return jnp.maximum(x, 0.0) + jnp.log1p(jnp.exp(-jnp.abs(x)))


def _split2(x):
    hi = x.astype(BF16)
    return hi, (x - hi.astype(F32)).astype(BF16)


def _split3(x):
    hi = x.astype(BF16)
    r = x - hi.astype(F32)
    mid = r.astype(BF16)
    return hi, mid, (r - mid.astype(F32)).astype(BF16)


def _dotf(a, b):
    return jnp.dot(a, b, preferred_element_type=F32)


def _lmat_exact(mat01, x):
    hi, mid, lo = _split3(x)
    return _dotf(mat01, hi) + _dotf(mat01, mid) + _dotf(mat01, lo)


def _rmat_exact(x, mat01, pieces=3):
    if pieces == 2:
        hi, lo = _split2(x)
        return _dotf(hi, mat01) + _dotf(lo, mat01)
    hi, mid, lo = _split3(x)
    return _dotf(hi, mat01) + _dotf(mid, mat01) + _dotf(lo, mat01)


def _iota(shape, axis):
    return lax.broadcasted_iota(I32, shape, axis)


def _ada_kernel(c_ref, w_ref, b_ref, o_ref):
    c = c_ref[...]
    o_ref[...] = _dotf(_silu(c).astype(BF16), w_ref[...].astype(BF16)) + b_ref[...]


def ada_modulation(c_all, w_ada, b_ada):
    depth, d, n = w_ada.shape
    bc = c_all.shape[0]
    tn = 1536 if n % 1536 == 0 else n
    return pl.pallas_call(
        _ada_kernel,
        out_shape=jax.ShapeDtypeStruct((depth, bc, n), F32),
        grid=(depth, n // tn),
        in_specs=[pl.BlockSpec((bc, d), lambda l, j: (0, 0)),
                  pl.BlockSpec((None, d, tn), lambda l, j: (l, 0, j)),
                  pl.BlockSpec((None, 1, tn), lambda l, j: (l, 0, j))],
        out_specs=pl.BlockSpec((None, bc, tn), lambda l, j: (l, 0, j)),
        compiler_params=_cparams(("arbitrary", "arbitrary")),
        name="ada_modulation",
    )(c_all, w_ada, b_ada.reshape(depth, 1, n))


def _mod_spec(tm, per_row, chunk):
    if per_row:
        return pl.BlockSpec((None, None, tm, D_MODEL), lambda g, i, l: (l[0], g, i, chunk))
    return pl.BlockSpec((None, None, 1, D_MODEL), lambda g, i, l: (l[0], g, 0, chunk))


def _in_proj_kernel(l_ref, x_ref, sh_ref, sc_ref, w_ref, q_ref, k_ref, v_ref, r_ref):
    del l_ref
    h = (_rms(x_ref[...]) * (1.0 + sc_ref[...]) + sh_ref[...]).astype(BF16)
    q_ref[...] = (_dotf(h, w_ref[:, 0:W_A]) * (DK_A ** -0.5)).astype(BF16)
    k_ref[...] = _dotf(h, w_ref[:, W_A:2 * W_A])
    v_ref[...] = _dotf(h, w_ref[:, 2 * W_A:3 * W_A])
    r_ref[...] = _dotf(h, w_ref[:, PK_REST:D_PACK])


def in_projection(lidx, x, mod, w_pack, tm):
    g, r, d = x.shape
    per_row = mod.shape[2] > 1
    row = lambda w, dt: jax.ShapeDtypeStruct((g, r, w), dt)
    ospec = lambda w: pl.BlockSpec((None, tm, w), lambda gi, i, l: (gi, i, 0))
    out_specs = [ospec(W_A), ospec(W_A), ospec(W_A), ospec(W_REST)]
    out_shape = [row(W_A, BF16), row(W_A, F32), row(W_A, F32), row(W_REST, F32)]
    grid_spec = pltpu.PrefetchScalarGridSpec(
        num_scalar_prefetch=1, grid=(g, r // tm),
        in_specs=[pl.BlockSpec((None, tm, d), lambda gi, i, l: (gi, i, 0)),
                  _mod_spec(tm, per_row, 0), _mod_spec(tm, per_row, 1),
                  pl.BlockSpec((None, d, D_PACK), lambda gi, i, l: (l[0], 0, 0))],
        out_specs=out_specs)
    return pl.pallas_call(
        _in_proj_kernel,
        out_shape=out_shape,
        grid_spec=grid_spec,
        compiler_params=_cparams(("arbitrary", "arbitrary")),
        name="in_projection",
    )(lidx, x, mod, mod, w_pack)


def _in_proj_prompt_kernel(l_ref, x_ref, sh_ref, sc_ref, w_ref, kall_in_ref, vall_in_ref,
                           q_ref, r_ref, kt_ref, vb_ref, kall_ref, vall_ref, k_scr):
    del l_ref, kall_in_ref, vall_in_ref
    tm = x_ref.shape[0]
    h = (_rms(x_ref[...]) * (1.0 + sc_ref[...]) + sh_ref[...]).astype(BF16)
    q_ref[...] = (_dotf(h, w_ref[:, 0:W_A]) * (DK_A ** -0.5)).astype(BF16)
    k_scr[...] = _dotf(h, w_ref[:, W_A:2 * W_A])
    kt = k_scr[...].T
    kall_ref[...] = kt
    kt_ref[...] = kt.astype(BF16)
    v = _dotf(h, w_ref[:, 2 * W_A:3 * W_A])
    vb_ref[...] = v.astype(BF16)
    for head in range(H_A):
        vall_ref[pl.ds(head, tm, stride=H_A), :] = v[:, head * DV_A:(head + 1) * DV_A]
    r_ref[...] = _dotf(h, w_ref[:, PK_REST:D_PACK])


def in_projection_prompt(lidx, x, mod, w_pack, k_all, v_all, tm, kblk):
    b, seq, d = x.shape
    sub = kblk // tm
    act = lambda w: pl.BlockSpec((None, tm, w), lambda bi, i, l: (bi, i, 0))
    grid_spec = pltpu.PrefetchScalarGridSpec(
        num_scalar_prefetch=1, grid=(b, seq // tm),
        in_specs=[act(d), _mod_spec(tm, False, 0), _mod_spec(tm, False, 1),
                  pl.BlockSpec((None, d, D_PACK), lambda bi, i, l: (l[0], 0, 0)),
                  pl.BlockSpec(memory_space=pl.ANY), pl.BlockSpec(memory_space=pl.ANY)],
        out_specs=[act(W_A), act(W_REST),
                   pl.BlockSpec((None, None, W_A, tm), lambda bi, i, l: (bi, i // sub, 0, i % sub)),
                   act(W_A),
                   pl.BlockSpec((None, None, W_A, tm), lambda bi, i, l: (l[0], bi, 0, i)),
                   pl.BlockSpec((None, None, H_A * tm, DV_A), lambda bi, i, l: (l[0], bi, i, 0))],
        scratch_shapes=[pltpu.VMEM((tm, W_A), F32)])
    return pl.pallas_call(
        _in_proj_prompt_kernel,
        out_shape=[jax.ShapeDtypeStruct((b, seq, W_A), BF16), jax.ShapeDtypeStruct((b, seq, W_REST), F32),
                   jax.ShapeDtypeStruct((b, seq // kblk, W_A, kblk), BF16), jax.ShapeDtypeStruct((b, seq, W_A), BF16),
                   jax.ShapeDtypeStruct(k_all.shape, F32), jax.ShapeDtypeStruct(v_all.shape, F32)],
        grid_spec=grid_spec,
        input_output_aliases={5: 4, 6: 5},
        compiler_params=_cparams(("arbitrary", "arbitrary")),
        name="in_projection_prompt",
    )(lidx, x, mod, mod, w_pack, k_all, v_all)


def _lambda_value(lqk, lam_init):
    s01 = jnp.sum(lqk[0:1, :] * lqk[1:2, :], axis=1, keepdims=True)
    s23 = jnp.sum(lqk[2:3, :] * lqk[3:4, :], axis=1, keepdims=True)
    return jnp.exp(s01) - jnp.exp(s23) + lam_init


def _attn_prompt_kernel(l_ref, lam_ref, lqk_ref, g_ref, q_ref, kt_ref, v_ref, bias_ref, o_ref,
                        vaug_ref, m_ref, acc_ref, *, blk):
    qi = pl.program_id(2)

    @pl.when(qi == 0)
    def _():
        vaug_ref[:, 0:DV_A] = v_ref[...]
        ones_col = _iota((vaug_ref.shape[0], DV_A), 1) == 0
        vaug_ref[:, DV_A:2 * DV_A] = jnp.where(ones_col, 1.0, 0.0).astype(BF16)

    q = q_ref[...]
    lane = _iota(q.shape, 1)
    zero = jnp.zeros_like(q)
    qm = (jnp.where(lane < DK_A, q, zero), jnp.where(lane >= DK_A, q, zero))
    m_ref[...] = jnp.full(m_ref.shape, NEG, F32)
    acc_ref[...] = jnp.zeros(acc_ref.shape, F32)

    def update(kt, va, bias):
        for m in range(2):
            s = _dotf(qm[m], kt)
            if bias is not None:
                pieces = [s[:, i * blk:(i + 1) * blk] if b is None else s[:, i * blk:(i + 1) * blk] + b
                          for i, b in enumerate(bias)]
                s = pieces[0] if len(pieces) == 1 else jnp.concatenate(pieces, axis=1)
            m_old = m_ref[m]
            m_new = jnp.maximum(m_old, jnp.broadcast_to(jnp.max(s, axis=1, keepdims=True), m_old.shape))
            p = jnp.exp(s - jnp.tile(m_new, (1, s.shape[1] // 128))).astype(BF16)
            acc_ref[m] = jnp.tile(jnp.exp(m_old - m_new), (1, 2)) * acc_ref[m] + _dotf(p, va)
            m_ref[m] = m_new

    def key_block(j):
        return kt_ref[j], vaug_ref[pl.ds(pl.multiple_of(j * 2 * blk, 2 * blk), 2 * blk), :]

    def far(j, carry):
        update(*key_block(j), None)
        return carry

    a = qi >> 1
    odd = (qi & 1) == 1
    lax.fori_loop(0, jnp.where(odd, a, jnp.maximum(a - 1, 0)), far, 0)

    @pl.when(odd)
    def _():
        update(*key_block(a), (bias_ref[1], bias_ref[0]))

    @pl.when(jnp.logical_and(jnp.logical_not(odd), a >= 1))
    def _():
        update(*key_block(a - 1), (None, bias_ref[1]))

    @pl.when(jnp.logical_not(odd))
    def _():
        start = pl.multiple_of(a * 2 * blk, 2 * blk)
        update(kt_ref[a, :, 0:blk], vaug_ref[pl.ds(start, blk), :], (bias_ref[0],))

    layer = l_ref[0]
    lam = _lambda_value(lqk_ref[...], lam_ref[2 * layer])
    acc0 = acc_ref[0]
    acc1 = acc_ref[1]
    o = (acc0[:, 0:DV_A] * (1.0 / acc0[:, DV_A:DV_A + 1])
         - lam * (acc1[:, 0:DV_A] * (1.0 / acc1[:, DV_A:DV_A + 1])))
    o_ref[...] = (_rms(o) * g_ref[...] * lam_ref[2 * layer + 1]).astype(BF16)


def _t5_table(rel_bias, n):
    dist = jnp.arange(n)
    max_exact = T5_BUCKETS // 2
    nf = jnp.maximum(dist, 1).astype(F32)
    large = max_exact + (jnp.log(nf / max_exact) / math.log(T5_MAX_DIST / max_exact)
                         * (T5_BUCKETS - max_exact)).astype(I32)
    bucket = jnp.where(dist < max_exact, dist, jnp.minimum(large, T5_BUCKETS - 1))
    return (rel_bias[bucket] - rel_bias[T5_BUCKETS - 1][None, :]).T.astype(F32)


def _bias_tiles_kernel(rel_ref, o_ref):
    h = pl.program_id(0)
    blk = o_ref.shape[0]
    d = _iota((blk, blk), 0) - _iota((blk, blk), 1) + pl.program_id(1) * blk
    dist = jnp.maximum(d, 0)
    max_exact = T5_BUCKETS // 2
    nf = jnp.maximum(dist, 1).astype(F32)
    large = max_exact + (jnp.log(nf / max_exact) / math.log(T5_MAX_DIST / max_exact)
                         * (T5_BUCKETS - max_exact)).astype(I32)
    bucket = jnp.where(dist < max_exact, dist, jnp.minimum(large, T5_BUCKETS - 1))
    sat = rel_ref[(T5_BUCKETS - 1) * H_A + h]
    val = jnp.zeros((blk, blk), F32)
    for b in range(T5_BUCKETS - 1):
        val = jnp.where(bucket == b, rel_ref[b * H_A + h] - sat, val)
    o_ref[...] = jnp.where(d >= 0, val, NEG)


def attention_bias_tiles(rel_bias, blk):
    return pl.pallas_call(
        _bias_tiles_kernel,
        out_shape=jax.ShapeDtypeStruct((H_A, 2, blk, blk), F32),
        grid=(H_A, 2),
        in_specs=[pl.BlockSpec(memory_space=pltpu.SMEM)],
        out_specs=pl.BlockSpec((None, None, blk, blk), lambda h, t: (h, t, 0, 0)),
        compiler_params=_cparams(("arbitrary", "arbitrary")),
        name="attention_bias_tiles",
    )(rel_bias.reshape(-1))


def attention_prompt(lidx, lam_consts, lambda_qk, subln_g3, q, kt, vb, bias_tiles, blk):
    b, seq, _ = q.shape
    assert blk >= T5_MAX_DIST and blk % 128 == 0 and kt.shape == (b, seq // (2 * blk), W_A, 2 * blk)
    grid_spec = pltpu.PrefetchScalarGridSpec(
        num_scalar_prefetch=1, grid=(b, H_A, seq // blk),
        in_specs=[pl.BlockSpec(memory_space=pltpu.SMEM),
                  pl.BlockSpec((None, 4, DK_A), lambda bi, h, i, l: (l[0], 0, 0)),
                  pl.BlockSpec((None, 1, DV_A), lambda bi, h, i, l: (l[0], 0, 0)),
                  pl.BlockSpec((None, blk, DV_A), lambda bi, h, i, l: (bi, i, h)),
                  pl.BlockSpec((None, seq // (2 * blk), DV_A, 2 * blk), lambda bi, h, i, l: (bi, 0, h, 0)),
                  pl.BlockSpec((None, seq, DV_A), lambda bi, h, i, l: (bi, 0, h)),
                  pl.BlockSpec((None, 2, blk, blk), lambda bi, h, i, l: (h, 0, 0, 0))],
        out_specs=pl.BlockSpec((None, blk, DV_A), lambda bi, h, i, l: (bi, i, h)),
        scratch_shapes=[pltpu.VMEM((seq, 2 * DV_A), BF16), pltpu.VMEM((2, blk, 128), F32),
                        pltpu.VMEM((2, blk, 2 * DV_A), F32)])
    return pl.pallas_call(
        functools.partial(_attn_prompt_kernel, blk=blk),
        out_shape=jax.ShapeDtypeStruct((b, seq, W_A), BF16),
        grid_spec=grid_spec,
        compiler_params=_cparams(("arbitrary", "arbitrary", "arbitrary")),
        name="attention_prompt",
    )(lidx, lam_consts, lambda_qk, subln_g3, q, kt, vb, bias_tiles)


def _bc_prompt_kernel(l_ref, rest_ref, convw_ref, convb_ref, v128_ref, v256_ref, wlr_ref,
                      ob_ref, oc_ref, conv_out_ref, ssm_out_ref, gla_out_ref,
                      xpad_ref, sbd_ref, sg_ref, b_ref, oi_ref):
    del l_ref
    c = pl.program_id(1)
    q = CHUNK

    @pl.when(c == 0)
    def _():
        xpad_ref[0:8, :] = jnp.zeros((8, XBC_B), F32)
        sbd_ref[...] = jnp.zeros(sbd_ref.shape, F32)
        sg_ref[...] = jnp.zeros(sg_ref.shape, F32)

    xpad_ref[8:8 + q, :] = rest_ref[:, R_BX:R_BX + XBC_B]
    conv = convb_ref[...]
    for j in range(CONV_W):
        conv = conv + convw_ref[j:j + 1, :] * xpad_ref[5 + j:5 + j + q, :]
    xbc = _silu(conv)
    tail = xpad_ref[5 + q:8 + q, :]
    conv_out_ref[...] = tail
    xpad_ref[5:8, :] = tail

    row = _iota((q, q), 0)
    col = _iota((q, q), 1)
    causal = row >= col
    tril = jnp.where(causal, 1.0, 0.0).astype(BF16)
    lane1 = _iota((1, 128), 1)
    misc = rest_ref[:, R_MISC:R_MISC + 128]

    xs = xbc[:, 0:W_B]
    bmat = xbc[:, W_B:W_B + G_B * N_B].astype(BF16)
    cmat = xbc[:, W_B + G_B * N_B:XBC_B].astype(BF16)
    is_head = lane1 < H_B
    dt = jnp.where(is_head, _softplus(misc + v128_ref[0:1, :]), 0.0)
    a_neg = jnp.where(is_head, -jnp.exp(v128_ref[1:2, :]), 0.0)
    cum = _lmat_exact(tril, dt * a_neg)
    cum_t = cum.T
    ecum = jnp.exp(cum)
    wgt = jnp.exp(cum[q - 1:q, :] - cum) * dt
    expand = jnp.where(_iota((128, W_B), 0) == (_iota((128, W_B), 1) >> 6), 1.0, 0.0).astype(BF16)
    dt_e = _rmat_exact(dt, expand)
    ecum_e = _rmat_exact(ecum, expand)
    wgt_e = _rmat_exact(wgt, expand)
    grp_lane = _iota((q, 128), 1) >> 6
    zero_b = jnp.zeros_like(cmat)
    gmat = [lax.dot_general(jnp.where(grp_lane == g, cmat, zero_b), bmat, (((1,), (1,)), ((), ())),
                            preferred_element_type=F32) for g in range(G_B)]
    mcat = []
    for h in range(H_B):
        diff = cum[:, h:h + 1] - cum_t[h:h + 1, :]
        dec = jnp.exp(jnp.where(causal, diff, NEG))
        mcat.append((gmat[h // (H_B // G_B)] * dec).astype(BF16))
    mcat = jnp.concatenate(mcat, axis=1)
    head_lane = _iota((q, W_B), 1) >> 6
    dtx = xs * dt_e
    xbd = jnp.concatenate([jnp.where(head_lane == h, dtx, 0.0) for h in range(H_B)], axis=0).astype(BF16)
    state = sbd_ref[...]
    y = _dotf(mcat, xbd) + _dotf(cmat, state.astype(BF16)) * ecum_e
    upd = lax.dot_general(bmat, (xs * wgt_e).astype(BF16), (((0,), (0,)), ((), ())), preferred_element_type=F32)
    same_grp = (_iota((128, W_B), 0) >> 6) == (_iota((128, W_B), 1) >> 7)
    state = state * ecum_e[q - 1:q, :] + jnp.where(same_grp, upd, 0.0)
    sbd_ref[...] = state
    ssm_out_ref[...] = state
    y = y + v256_ref[0:1, :] * xs
    ob_ref[...] = (_rms(y * _silu(rest_ref[:, R_BZ:R_BZ + W_B])) * v256_ref[1:2, :]).astype(BF16)

    qg = rest_ref[:, R_CQ:R_CQ + 128] * (DK_C ** -0.5)
    kg = rest_ref[:, R_CK:R_CK + 128]
    vg = rest_ref[:, R_CV:R_CV + W_C]
    gk = _dotf(misc.astype(BF16), wlr_ref[...]) + v128_ref[2:3, :]
    gate = -_softplus(-gk) * (1.0 / GATE_NORM)
    bcum = _lmat_exact(tril, gate)
    b_last = bcum[q - 1:q, :]
    span = jnp.max(-b_last)
    qe = qg * jnp.exp(bcum)
    vhead = _iota((q, W_C), 1) >> 6

    @pl.when(span <= 80.0)
    def _():
        ke = kg * jnp.exp(-bcum)
        kbd = jnp.where((_iota((128, 4 * q), 0) >> 5) == (_iota((128, 4 * q), 1) >> 7),
                        jnp.tile(ke.T, (1, H_C)), 0.0).astype(BF16)
        att = _dotf(qe.astype(BF16), kbd)
        causal4 = _iota((q, H_C * q), 0) >= (_iota((q, H_C * q), 1) & (q - 1))
        att = jnp.where(causal4, att, 0.0).astype(BF16)
        vbd = jnp.concatenate([jnp.where(vhead == h, vg, 0.0) for h in range(H_C)], axis=0).astype(BF16)
        oi_ref[...] = _dotf(att, vbd)

    @pl.when(span > 80.0)
    def _():
        b_ref[...] = bcum
        ind = jnp.where((_iota((128, W_C), 0) >> 5) == (_iota((128, W_C), 1) >> 6), 1.0, 0.0).astype(BF16)
        trow = _iota((q, 128), 0)

        def body(grp, acc):
            base = pl.multiple_of(grp * 8, 8)
            ks8 = rest_ref[pl.ds(base, 8), R_CK:R_CK + 128]
            vs8 = rest_ref[pl.ds(base, 8), R_CV:R_CV + W_C]
            bs8 = b_ref[pl.ds(base, 8), :]
            for j in range(8):
                d = jnp.exp(jnp.minimum(bcum - bs8[j:j + 1, :], 0.0)) * qg * ks8[j:j + 1, :]
                d = jnp.where(trow >= base + j, d, 0.0)
                acc = acc + _dotf(d.astype(BF16), ind) * vs8[j:j + 1, :]
            return acc

        oi_ref[...] = lax.fori_loop(0, q // 8, body, jnp.zeros((q, W_C), F32))

    sg = sg_ref[...]
    o = oi_ref[...] + _dotf(qe.astype(BF16), sg.astype(BF16))
    k2 = (kg * jnp.exp(b_last - bcum)).astype(BF16)
    updg = lax.dot_general(k2, vg.astype(BF16), (((0,), (0,)), ((), ())), preferred_element_type=F32)
    same_head = (_iota((128, W_C), 0) >> 5) == (_iota((128, W_C), 1) >> 6)
    sg = sg * jnp.exp(bcum.T[:, q - 1:q]) + jnp.where(same_head, updg, 0.0)
    sg_ref[...] = sg
    gla_out_ref[...] = sg
    avg = jnp.where((_iota((W_C, W_C), 0) >> 6) == (_iota((W_C, W_C), 1) >> 6), 1.0, 0.0).astype(BF16)
    ms = _rmat_exact(o * o, avg) * (1.0 / DV_C)
    oc = o * lax.rsqrt(ms + EPS) * v256_ref[2:3, :] * _silu(rest_ref[:, R_CG:R_CG + W_C])
    oc_ref[...] = oc.astype(BF16)


def bc_mixers_prompt(lidx, rest, convw, convb3, vec128, vec256, wlr):
    b, seq, _ = rest.shape
    nc = seq // CHUNK
    par = lambda r, w: pl.BlockSpec((None, r, w), lambda bi, c, l: (l[0], 0, 0))
    st = lambda r, w: pl.BlockSpec((None, r, w), lambda bi, c, l: (bi, 0, 0))
    grid_spec = pltpu.PrefetchScalarGridSpec(
        num_scalar_prefetch=1, grid=(b, nc),
        in_specs=[pl.BlockSpec((None, CHUNK, W_REST), lambda bi, c, l: (bi, c, 0)),
                  par(CONV_W, XBC_B), par(1, XBC_B), par(8, 128), par(8, 256), par(128, 128)],
        out_specs=[pl.BlockSpec((None, CHUNK, W_B), lambda bi, c, l: (bi, c, 0)),
                   pl.BlockSpec((None, CHUNK, W_C), lambda bi, c, l: (bi, c, 0)),
                   st(CONV_W - 1, XBC_B), st(128, W_B), st(128, W_C)],
        scratch_shapes=[pltpu.VMEM((8 + CHUNK, XBC_B), F32), pltpu.VMEM((128, W_B), F32),
                        pltpu.VMEM((128, W_C), F32), pltpu.VMEM((CHUNK, 128), F32),
                        pltpu.VMEM((CHUNK, W_C), F32)])
    return pl.pallas_call(
        _bc_prompt_kernel,
        out_shape=[jax.ShapeDtypeStruct((b, seq, W_B), BF16), jax.ShapeDtypeStruct((b, seq, W_C), BF16),
                   jax.ShapeDtypeStruct((b, CONV_W - 1, XBC_B), F32),
                   jax.ShapeDtypeStruct((b, 128, W_B), F32), jax.ShapeDtypeStruct((b, 128, W_C), F32)],
        grid_spec=grid_spec,
        compiler_params=_cparams(("arbitrary", "arbitrary")),
        name="bc_mixers_prompt",
    )(lidx, rest, convw, convb3, vec128, vec256, wlr)


def _unpack_ssm_state(sbd):
    b = sbd.shape[0]
    s = sbd.reshape(b, G_B, N_B, H_B, P_B)
    per_head = [s[:, h // (H_B // G_B), :, h, :] for h in range(H_B)]
    return jnp.swapaxes(jnp.stack(per_head, axis=1), 2, 3)


def _unpack_gla_state(sg):
    b = sg.shape[0]
    s = sg.reshape(b, H_C, DK_C, H_C, DV_C)
    return jnp.stack([s[:, h, :, h, :] for h in range(H_C)], axis=1)


def _out_proj_kernel(l_ref, oa_ref, ob_ref, oc_ref, w_ref, x_ref, g1_ref, sh2_ref, sc2_ref, x1_ref, h2_ref):
    del l_ref
    mix = (_dotf(oa_ref[...], w_ref[0:W_A, :]) + _dotf(ob_ref[...], w_ref[W_A:W_A + W_B, :])
           + _dotf(oc_ref[...], w_ref[W_A + W_B:, :]))
    x1 = x_ref[...] + g1_ref[...] * mix
    x1_ref[...] = x1
    h2_ref[...] = (_rms(x1) * (1.0 + sc2_ref[...]) + sh2_ref[...]).astype(h2_ref.dtype)


def out_projection(lidx, oa, ob, oc, w_out, x, mod, tm, h2_dtype):
    g, r, d = x.shape
    per_row = mod.shape[2] > 1
    act = lambda w: pl.BlockSpec((None, tm, w), lambda gi, i, l: (gi, i, 0))
    grid_spec = pltpu.PrefetchScalarGridSpec(
        num_scalar_prefetch=1, grid=(g, r // tm),
        in_specs=[act(W_A), act(W_B), act(W_C),
                  pl.BlockSpec((None, d, d), lambda gi, i, l: (l[0], 0, 0)),
                  act(d), _mod_spec(tm, per_row, 2), _mod_spec(tm, per_row, 3), _mod_spec(tm, per_row, 4)],
        out_specs=[act(d), act(d)])
    return pl.pallas_call(
        _out_proj_kernel,
        out_shape=[jax.ShapeDtypeStruct((g, r, d), F32), jax.ShapeDtypeStruct((g, r, d), h2_dtype)],
        grid_spec=grid_spec,
        compiler_params=_cparams(("arbitrary", "arbitrary")),
        name="out_projection",
    )(lidx, oa, ob, oc, w_out, x, mod, mod, mod)


def _ffn_kernel(i_ref, h_ref, wg_ref, wu_ref, wd_ref, x1_ref, g2_ref, fg_ref, o_ref, acc_ref, *, final):
    del i_ref
    f = pl.program_id(2)

    @pl.when(f == 0)
    def _():
        acc_ref[...] = jnp.zeros(acc_ref.shape, F32)

    h = h_ref[...]
    a = (_silu(_dotf(h, wg_ref[...])) * _dotf(h, wu_ref[...])).astype(BF16)
    acc_ref[...] += _dotf(a, wd_ref[...])

    @pl.when(f == pl.num_programs(2) - 1)
    def _():
        x2 = x1_ref[...] + g2_ref[...] * acc_ref[...]
        o_ref[...] = _rms(x2) * fg_ref[...] if final else x2


def dense_ffn(iidx, lidx_mod, h2, wg, wu, wd, x1, mod, final_g, tm, tf, final):
    g, r, d = x1.shape
    per_row = mod.shape[2] > 1
    nf = D_FF // tf
    act = lambda w: pl.BlockSpec((None, tm, w), lambda gi, i, f, s: (gi, i, 0))
    if per_row:
        g2 = pl.BlockSpec((None, None, tm, d), lambda gi, i, f, s: (s[1], gi, i, 5))
    else:
        g2 = pl.BlockSpec((None, None, 1, d), lambda gi, i, f, s: (s[1], gi, 0, 5))
    grid_spec = pltpu.PrefetchScalarGridSpec(
        num_scalar_prefetch=1, grid=(g, r // tm, nf),
        in_specs=[act(d),
                  pl.BlockSpec((None, d, tf), lambda gi, i, f, s: (s[0], 0, f)),
                  pl.BlockSpec((None, d, tf), lambda gi, i, f, s: (s[0], 0, f)),
                  pl.BlockSpec((None, tf, d), lambda gi, i, f, s: (s[0], f, 0)),
                  act(d), g2, pl.BlockSpec((1, d), lambda gi, i, f, s: (0, 0))],
        out_specs=act(d),
        scratch_shapes=[pltpu.VMEM((tm, d), F32)])
    sidx = jnp.concatenate([iidx, lidx_mod])
    return pl.pallas_call(
        functools.partial(_ffn_kernel, final=final),
        out_shape=jax.ShapeDtypeStruct((g, r, d), F32),
        grid_spec=grid_spec,
        compiler_params=_cparams(("arbitrary", "arbitrary", "arbitrary")),
        name="dense_ffn",
    )(sidx, h2, wg, wu, wd, x1, mod, final_g)


def _route_kernel(i_ref, h_ref, rw_ref, tri_ref, idx_ref, gate_ref, cnt_ref, carry_ref):
    del i_ref
    i = pl.program_id(0)

    @pl.when(i == 0)
    def _():
        carry_ref[...] = jnp.zeros(carry_ref.shape, F32)

    hh, hl = _split2(h_ref[...])
    wh, wl = _split2(rw_ref[...])
    logits = _dotf(hh, wh) + _dotf(hl, wh) + _dotf(hh, wl)
    lane = _iota(logits.shape, 1).astype(F32)
    logits = jnp.where(lane < N_EXPERTS, logits, NEG)
    m1 = jnp.max(logits, axis=1, keepdims=True)
    i1 = jnp.min(jnp.where(logits == m1, lane, 128.0), axis=1, keepdims=True)
    rest = jnp.where(lane == i1, NEG, logits)
    m2 = jnp.max(rest, axis=1, keepdims=True)
    i2 = jnp.min(jnp.where(rest == m2, lane, 128.0), axis=1, keepdims=True)
    e = jnp.exp(m2 - m1)
    g1 = 1.0 / (1.0 + e)
    sel = jnp.where(lane == i1, 1.0, 0.0) + jnp.where(lane == i2, 1.0, 0.0)
    before = _dotf(tri_ref[...], sel.astype(BF16)) + carry_ref[...]
    r1 = jnp.sum(jnp.where(lane == i1, before, 0.0), axis=1, keepdims=True)
    r2 = jnp.sum(jnp.where(lane == i2, before, 0.0), axis=1, keepdims=True)
    carry_ref[...] = carry_ref[...] + jnp.sum(sel, axis=0, keepdims=True)
    lane8 = _iota(idx_ref.shape, 1).astype(F32)
    idx_ref[...] = jnp.where(lane8 == 0.0, i1, jnp.where(lane8 == 1.0, i2, jnp.where(lane8 == 2.0, r1, r2))).astype(I32)
    gate_ref[...] = jnp.where(lane8 == 0.0, g1, e * g1)
    cnt_ref[...] = jnp.broadcast_to(carry_ref[...], cnt_ref.shape)


def moe_route(iidx, h2, router_pad, tm):
    t, d = h2.shape
    tri = jnp.asarray(np.tril(np.ones((tm, tm), np.float32), -1), BF16)
    grid_spec = pltpu.PrefetchScalarGridSpec(
        num_scalar_prefetch=1, grid=(t // tm,),
        in_specs=[pl.BlockSpec((tm, d), lambda i, s: (i, 0)),
                  pl.BlockSpec((None, d, 128), lambda i, s: (s[0], 0, 0)),
                  pl.BlockSpec((tm, tm), lambda i, s: (0, 0))],
        out_specs=[pl.BlockSpec((tm, 8), lambda i, s: (i, 0)), pl.BlockSpec((tm, 8), lambda i, s: (i, 0)),
                   pl.BlockSpec((8, 128), lambda i, s: (0, 0))],
        scratch_shapes=[pltpu.VMEM((1, 128), F32)])
    return pl.pallas_call(
        _route_kernel,
        out_shape=[jax.ShapeDtypeStruct((t, 8), I32), jax.ShapeDtypeStruct((t, 8), F32),
                   jax.ShapeDtypeStruct((8, 128), F32)],
        grid_spec=grid_spec,
        compiler_params=_cparams(("arbitrary",)),
        name="moe_route",
    )(iidx, h2, router_pad, tri)


def _row_copy(src_ref, src_row, dst_ref, dst_row, sem):
    return pltpu.make_async_copy(src_ref.at[pl.ds(src_row, 1), :], dst_ref.at[pl.ds(dst_row, 1), :], sem)


def _dispatch_kernel(dest_ref, h_ref, xb_in_ref, xb_ref, sem):
    del xb_in_ref
    tm = h_ref.shape[0]

    def start(r, carry):
        for k in range(TOP_K):
            _row_copy(h_ref, r, xb_ref, dest_ref[0, TOP_K * r + k], sem).start()
        return carry

    lax.fori_loop(0, tm, start, 0, unroll=8)
    for k in range(TOP_K):
        pltpu.make_async_copy(h_ref, xb_ref.at[pl.ds(0, tm), :], sem).wait()


def moe_dispatch(dest3, h2, n_rows, tm):
    t, d = h2.shape
    zeros = jnp.zeros((n_rows, d), F32)
    return pl.pallas_call(
        _dispatch_kernel,
        out_shape=jax.ShapeDtypeStruct((n_rows, d), F32),
        grid=(t // tm,),
        in_specs=[pl.BlockSpec((None, 1, TOP_K * tm), lambda i: (i, 0, 0), memory_space=pltpu.SMEM),
                  pl.BlockSpec((tm, d), lambda i: (i, 0)),
                  pl.BlockSpec(memory_space=pl.ANY)],
        out_specs=pl.BlockSpec(memory_space=pl.ANY),
        scratch_shapes=[pltpu.SemaphoreType.DMA(())],
        input_output_aliases={2: 0},
        compiler_params=pltpu.CompilerParams(dimension_semantics=("arbitrary",), vmem_limit_bytes=VMEM_LIMIT,
                                             has_side_effects=True),
        name="moe_dispatch",
    )(dest3, h2, zeros)


def _expert_kernel(s_ref, x_ref, wg_ref, wu_ref, wd_ref, y_ref, xb_ref, acc_ref):
    i = pl.program_id(0)
    f = pl.program_id(1)
    nb = pl.num_programs(0)

    @pl.when(i < s_ref[nb + 1])
    def _():
        @pl.when(f == 0)
        def _():
            xb_ref[...] = x_ref[...].astype(BF16)
            acc_ref[...] = jnp.zeros(acc_ref.shape, F32)

        x = xb_ref[...]
        a = (_silu(_dotf(x, wg_ref[...])) * _dotf(x, wu_ref[...])).astype(BF16)
        acc_ref[...] += _dotf(a, wd_ref[...])

        @pl.when(f == pl.num_programs(1) - 1)
        def _():
            y_ref[...] = acc_ref[...]

    @pl.when(jnp.logical_and(i >= s_ref[nb + 1], f == pl.num_programs(1) - 1))
    def _():
        y_ref[...] = jnp.zeros(y_ref.shape, F32)


def moe_experts(sidx, xb, wg, wu, wd, blk, tf):
    n_rows, d = xb.shape
    nb = n_rows // blk
    nf = D_FF // tf

    def row_map(i, f, s):
        return (jnp.minimum(i, s[nb + 1] - 1), 0)

    def f_of(i, f, s):
        return jnp.where(i < s[nb + 1], f, nf - 1)

    grid_spec = pltpu.PrefetchScalarGridSpec(
        num_scalar_prefetch=1, grid=(nb, nf),
        in_specs=[pl.BlockSpec((blk, d), row_map),
                  pl.BlockSpec((None, None, d, tf), lambda i, f, s: (s[nb], s[i], 0, f_of(i, f, s))),
                  pl.BlockSpec((None, None, d, tf), lambda i, f, s: (s[nb], s[i], 0, f_of(i, f, s))),
                  pl.BlockSpec((None, None, tf, d), lambda i, f, s: (s[nb], s[i], f_of(i, f, s), 0))],
        out_specs=pl.BlockSpec((blk, d), lambda i, f, s: (i, 0)),
        scratch_shapes=[pltpu.VMEM((blk, d), BF16), pltpu.VMEM((blk, d), F32)])
    return pl.pallas_call(
        _expert_kernel,
        out_shape=jax.ShapeDtypeStruct((n_rows, d), F32),
        grid_spec=grid_spec,
        compiler_params=_cparams(("arbitrary", "arbitrary")),
        name="moe_experts",
    )(sidx, xb, wg, wu, wd)


def _combine_kernel(dest_ref, yb_ref, gate_ref, x1_ref, g2_ref, fg_ref, o_ref, buf_ref, sem, *, final):
    tm = x1_ref.shape[0]

    def start(r, carry):
        for k in range(TOP_K):
            _row_copy(yb_ref, dest_ref[0, TOP_K * r + k], buf_ref.at[k], r, sem).start()
        return carry

    lax.fori_loop(0, tm, start, 0, unroll=8)
    for k in range(TOP_K):
        pltpu.make_async_copy(yb_ref.at[pl.ds(0, tm), :], buf_ref.at[k], sem).wait()
    gate = gate_ref[...]
    f = gate[:, 0:1] * buf_ref[0] + gate[:, 1:2] * buf_ref[1]
    x2 = x1_ref[...] + g2_ref[...] * f
    o_ref[...] = _rms(x2) * fg_ref[...] if final else x2


def moe_combine(dest3, yb, gates, x1, mod, lidx_mod, final_g, tm, final):
    g, r, d = x1.shape
    per_row = mod.shape[1] > 1
    nt = r // tm
    del lidx_mod
    act = pl.BlockSpec((None, tm, d), lambda gi, i: (gi, i, 0))
    return pl.pallas_call(
        functools.partial(_combine_kernel, final=final),
        out_shape=jax.ShapeDtypeStruct((g, r, d), F32),
        grid=(g, nt),
        in_specs=[pl.BlockSpec((None, 1, TOP_K * tm), lambda gi, i: (gi * nt + i, 0, 0), memory_space=pltpu.SMEM),
                  pl.BlockSpec(memory_space=pl.ANY),
                  pl.BlockSpec((tm, 8), lambda gi, i: (gi * nt + i, 0)),
                  act,
                  (pl.BlockSpec((None, tm, d), lambda gi, i: (gi, i, 0)) if per_row
                   else pl.BlockSpec((None, 1, d), lambda gi, i: (gi, 0, 0))),
                  pl.BlockSpec((1, d), lambda gi, i: (0, 0))],
        out_specs=act,
        scratch_shapes=[pltpu.VMEM((TOP_K, tm, d), F32), pltpu.SemaphoreType.DMA(())],
        compiler_params=_cparams(("arbitrary", "arbitrary")),
        name="moe_combine",
    )(dest3, yb, gates, x1, mod, final_g)


def moe_ffn(layer_slot, h2, x1, mod_g2, router_pad, wg, wu, wd, final_g, tm, blk, tf, final):
    g, r, d = x1.shape
    t = g * r
    h2f = h2.reshape(t, d)
    iidx = jnp.array([layer_slot], I32)
    idx, gates, counts = moe_route(iidx, h2f, router_pad, _pick(t, 1024))
    counts = counts[0, :N_EXPERTS].astype(I32)
    padded = (counts + blk - 1) // blk * blk
    pad_end = jnp.cumsum(padded)
    pad_start = pad_end - padded
    first = sum(jnp.where(idx[:, 0:TOP_K] == e, pad_start[e], 0) for e in range(N_EXPERTS))
    dest = first + idx[:, TOP_K:2 * TOP_K]
    nb = -(-(t * TOP_K) // blk) + N_EXPERTS
    n_used = pad_end[-1] // blk
    blk_e = jnp.minimum(jnp.searchsorted(pad_end, jnp.arange(nb, dtype=I32) * blk, side='right'), N_EXPERTS - 1)
    blk_e = jnp.where(jnp.arange(nb) < n_used, blk_e, blk_e[jnp.maximum(n_used - 1, 0)]).astype(I32)
    dest3 = dest.astype(I32).reshape(t // tm, 1, TOP_K * tm)
    xb = moe_dispatch(dest3, h2f, nb * blk, tm)
    sidx = jnp.concatenate([blk_e, iidx, n_used.astype(I32)[None]])
    yb = moe_experts(sidx, xb, wg, wu, wd, blk, tf)
    return moe_combine(dest3, yb, gates, x1, mod_g2, None, final_g, tm, final)


def _attn_decode_kernel(pt_ref, lam_ref, lqk_ref, g_ref, q_ref, kn_ref, vn_ref, tab_ref, ck_ref, cv_ref, o_ref,
                        kbuf_ref, vbuf_ref, sem, m_ref, s_ref, acc_ref, *, pps):
    step = pl.program_id(1)
    n_steps = pl.num_programs(1)
    layer = pt_ref[pt_ref.shape[0] - 1]
    rows = 2 * H_A
    t = pl.program_id(0) * n_steps + step
    slot = lax.rem(t, 2)

    def start_pages(tt, sl):
        for i in range(pps):
            page = pt_ref[tt * pps + i]
            pltpu.make_async_copy(ck_ref.at[layer, page], kbuf_ref.at[sl, i], sem.at[0, sl]).start()
            pltpu.make_async_copy(cv_ref.at[layer, page], vbuf_ref.at[sl, i], sem.at[1, sl]).start()

    @pl.when(t == 0)
    def _():
        start_pages(0, 0)

    @pl.when(t + 1 < pl.num_programs(0) * n_steps)
    def _():
        start_pages(t + 1, 1 - slot)

    pltpu.make_async_copy(ck_ref.at[layer, pl.ds(0, pps)], kbuf_ref.at[slot], sem.at[0, slot]).wait()
    pltpu.make_async_copy(cv_ref.at[layer, pl.ds(0, pps)], vbuf_ref.at[slot], sem.at[1, slot]).wait()

    @pl.when(step == 0)
    def _():
        m_ref[...] = jnp.full(m_ref.shape, NEG, F32)
        s_ref[...] = jnp.zeros(s_ref.shape, F32)
        acc_ref[...] = jnp.zeros(acc_ref.shape, F32)

    qrow = jnp.broadcast_to(q_ref[...].astype(F32), (rows, W_A))
    own = (_iota((rows, W_A), 1) >> 6) == _iota((rows, W_A), 0)
    q8 = jnp.where(own, qrow, 0.0).astype(BF16)

    def accumulate(s, pv):
        m_old = m_ref[...]
        m_new = jnp.maximum(m_old, jnp.max(s, axis=1, keepdims=True))
        alpha = jnp.exp(m_old - m_new)
        p = jnp.exp(s - m_new)
        s_ref[...] = alpha * s_ref[...] + jnp.sum(p, axis=1, keepdims=True)
        acc_ref[...] = alpha * acc_ref[...] + pv(p)
        m_ref[...] = m_new

    scores = [_dotf(q8, kbuf_ref[slot, i].astype(BF16)) for i in range(pps)]
    scores[pps - 1] = jnp.where(step == n_steps - 1, scores[pps - 1] + tab_ref[0:rows, :], scores[pps - 1])
    s_all = jnp.concatenate(scores, axis=1)

    def value_page(i):
        heads = [vbuf_ref[slot, i, pl.ds(h, PAGE, stride=H_A), :] for h in range(H_A)]
        return jnp.concatenate(heads, axis=1).astype(BF16)

    def pv_pages(p):
        out = _dotf(p[:, 0:PAGE].astype(BF16), value_page(0))
        for i in range(1, pps):
            out = out + _dotf(p[:, i * PAGE:(i + 1) * PAGE].astype(BF16), value_page(i))
        return out

    accumulate(s_all, pv_pages)

    @pl.when(step == n_steps - 1)
    def _():
        kn = kn_ref[...].astype(BF16).astype(F32)
        vn = vn_ref[...].astype(BF16).astype(F32)
        s_new = jnp.sum(q8.astype(F32) * kn, axis=1, keepdims=True) + tab_ref[rows:2 * rows, 0:1]
        accumulate(s_new, lambda p: p.astype(BF16).astype(F32) * vn)
        o = acc_ref[...] * (1.0 / s_ref[...])
        lam = _lambda_value(lqk_ref[...], lam_ref[2 * layer])
        outs = []
        for h in range(H_A):
            blkh = o[:, h * DV_A:(h + 1) * DV_A]
            oh = blkh[2 * h:2 * h + 1, :] - lam * blkh[2 * h + 1:2 * h + 2, :]
            outs.append(_rms(oh) * g_ref[...] * lam_ref[2 * layer + 1])
        o_ref[...] = jnp.concatenate(outs, axis=1).astype(BF16)


def attention_decode(sidx, lam_consts, lambda_qk, subln_g3, q, k_new, v_new, tab, cache_k4, cache_v4, n_pages, pps):
    bd = q.shape[0]
    nl = sidx.shape[0] - 1
    assert n_pages % pps == 0
    row = pl.BlockSpec((None, 1, W_A), lambda b, s, pt: (b, 0, 0))
    grid_spec = pltpu.PrefetchScalarGridSpec(
        num_scalar_prefetch=1, grid=(bd, n_pages // pps),
        in_specs=[pl.BlockSpec(memory_space=pltpu.SMEM),
                  pl.BlockSpec((None, 4, DK_A), lambda b, s, pt: (pt[nl], 0, 0)),
                  pl.BlockSpec((None, 1, DV_A), lambda b, s, pt: (pt[nl], 0, 0)),
                  row, row, row,
                  pl.BlockSpec((16, PAGE), lambda b, s, pt: (0, 0)),
                  pl.BlockSpec(memory_space=pl.ANY), pl.BlockSpec(memory_space=pl.ANY)],
        out_specs=row,
        scratch_shapes=[pltpu.VMEM((2, pps, W_A, PAGE), F32), pltpu.VMEM((2, pps, H_A * PAGE, DV_A), F32),
                        pltpu.SemaphoreType.DMA((2, 2)),
                        pltpu.VMEM((2 * H_A, 1), F32), pltpu.VMEM((2 * H_A, 1), F32),
                        pltpu.VMEM((2 * H_A, W_A), F32)])
    return pl.pallas_call(
        functools.partial(_attn_decode_kernel, pps=pps),
        out_shape=jax.ShapeDtypeStruct((bd, 1, W_A), BF16),
        grid_spec=grid_spec,
        compiler_params=_cparams(("arbitrary", "arbitrary")),
        name="attention_decode",
    )(sidx, lam_consts, lambda_qk, subln_g3, q, k_new, v_new, tab, cache_k4, cache_v4)


def _bc_decode_pre_kernel(l_ref, rest_ref, buf_ref, convw_ref, convb_ref, v128_ref, wlr_ref,
                          xbc_ref, nbuf_ref, dt_ref, dec_ref, eg_ref):
    del l_ref
    u = rest_ref[:, R_BX:R_BX + XBC_B]
    conv = convb_ref[...] + convw_ref[CONV_W - 1:CONV_W, :] * u
    for j in range(CONV_W - 1):
        conv = conv + convw_ref[j:j + 1, :] * buf_ref[j]
    xbc_ref[...] = _silu(conv)
    for j in range(CONV_W - 2):
        nbuf_ref[j] = buf_ref[j + 1]
    nbuf_ref[CONV_W - 2] = u
    misc = rest_ref[:, R_MISC:R_MISC + 128]
    dt = _softplus(misc + v128_ref[0:1, :])
    dt_ref[...] = dt
    dec_ref[...] = jnp.exp(dt * (-jnp.exp(v128_ref[1:2, :])))
    gk = _dotf(misc.astype(BF16), wlr_ref[...]) + v128_ref[2:3, :]
    eg_ref[...] = jnp.exp(-_softplus(-gk) * (1.0 / GATE_NORM))


def bc_decode_pre(lidx, rest, conv_state_t, convw, convb3, vec128, wlr):
    bd = rest.shape[0]
    par = lambda r, w: pl.BlockSpec((None, r, w), lambda i, l: (l[0], 0, 0))
    full = lambda w: pl.BlockSpec((bd, w), lambda i, l: (0, 0))
    grid_spec = pltpu.PrefetchScalarGridSpec(
        num_scalar_prefetch=1, grid=(1,),
        in_specs=[full(W_REST),
                  pl.BlockSpec((None, CONV_W - 1, bd, XBC_B), lambda i, l: (l[0], 0, 0, 0)),
                  par(CONV_W, XBC_B), par(1, XBC_B), par(8, 128), par(128, 128)],
        out_specs=[full(XBC_B), pl.BlockSpec((CONV_W - 1, bd, XBC_B), lambda i, l: (0, 0, 0)),
                   full(128), full(128), full(128)])
    return pl.pallas_call(
        _bc_decode_pre_kernel,
        out_shape=[jax.ShapeDtypeStruct((bd, XBC_B), F32), jax.ShapeDtypeStruct((CONV_W - 1, bd, XBC_B), F32),
                   jax.ShapeDtypeStruct((bd, 128), F32), jax.ShapeDtypeStruct((bd, 128), F32),
                   jax.ShapeDtypeStruct((bd, 128), F32)],
        grid_spec=grid_spec,
        compiler_params=_cparams(("arbitrary",)),
        name="bc_decode_pre",
    )(lidx, rest, conv_state_t, convw, convb3, vec128, wlr)


def _bc_decode_state_kernel(l_ref, ssm_ref, gla_ref, x4_ref, b4_ref, c4_ref, dt4_ref, dec4_ref, bz4_ref,
                            dsk_ref, ng_ref, q4_ref, k4_ref, v4_ref, eg4_ref, cg4_ref, gng_ref,
                            ssm_out_ref, gla_out_ref, ob4_ref, oc4_ref):
    del l_ref
    x4 = x4_ref[...]
    s = ssm_ref[...] * dec4_ref[...] + (dt4_ref[...] * x4) * b4_ref[...]
    ssm_out_ref[...] = s
    y = jnp.sum(c4_ref[...] * s, axis=3, keepdims=True) + dsk_ref[...] * x4
    yg = y * _silu(bz4_ref[...])
    ms = jnp.sum(jnp.sum(yg * yg, axis=2, keepdims=True), axis=1, keepdims=True) * (1.0 / W_B)
    ob4_ref[...] = yg * lax.rsqrt(ms + EPS) * ng_ref[...]
    sg = gla_ref[...] * eg4_ref[...] + k4_ref[...] * v4_ref[...]
    gla_out_ref[...] = sg
    o = jnp.sum(q4_ref[...] * sg, axis=2, keepdims=True)
    oc4_ref[...] = _rms(o) * gng_ref[...] * _silu(cg4_ref[...])


def bc_decode_state(lidx, state_ssm, state_gla, x4, b4, c4, dt4, dec4, bz4, dsk4, ng4, q4, k4, v4, eg4, cg4, gng4):
    bd = x4.shape[0]

    def full(shape):
        n = len(shape)
        return pl.BlockSpec(shape, lambda i, l: (0,) * n)

    def layer(shape):
        n = len(shape)
        return pl.BlockSpec((None,) + shape, lambda i, l: (l[0],) + (0,) * n)

    s_ssm = (bd, H_B, P_B, N_B)
    s_gla = (bd, H_C, DK_C, DV_C)
    col_b = (bd, H_B, P_B, 1)
    row_b = (bd, H_B, 1, N_B)
    one_b = (bd, H_B, 1, 1)
    col_c = (bd, H_C, DK_C, 1)
    row_c = (bd, H_C, 1, DV_C)
    grid_spec = pltpu.PrefetchScalarGridSpec(
        num_scalar_prefetch=1, grid=(1,),
        in_specs=[layer(s_ssm), layer(s_gla), full(col_b), full(row_b), full(row_b), full(one_b), full(one_b),
                  full(col_b), layer((1, H_B, 1, 1)), layer((1, H_B, P_B, 1)),
                  full(col_c), full(col_c), full(row_c), full(col_c), full(row_c), layer((1, 1, 1, DV_C))],
        out_specs=[full(s_ssm), full(s_gla), full(col_b), full(row_c)])
    return pl.pallas_call(
        _bc_decode_state_kernel,
        out_shape=[jax.ShapeDtypeStruct(s_ssm, F32), jax.ShapeDtypeStruct(s_gla, F32),
                   jax.ShapeDtypeStruct(col_b, F32), jax.ShapeDtypeStruct(row_c, F32)],
        grid_spec=grid_spec,
        compiler_params=_cparams(("arbitrary",)),
        name="bc_decode_state",
    )(lidx, state_ssm, state_gla, x4, b4, c4, dt4, dec4, bz4, dsk4, ng4, q4, k4, v4, eg4, cg4, gng4)


def _pack_params(w_in, rel_bias, subln_g, conv_b, dt_bias, a_log, d_skip, ssd_norm_g, gla_w_lr, gla_b_lr,
                 gla_norm_g, router_w):
    depth = w_in.shape[0]
    pad = jnp.zeros(w_in.shape[:2] + (D_PACK - D_IN,), w_in.dtype)
    w_pack = jnp.concatenate([w_in[..., :OFF_BDT], w_in[..., OFF_CQ:OFF_CLR], w_in[..., OFF_BDT:OFF_CQ],
                              w_in[..., OFF_CLR:], pad], axis=-1).astype(BF16)
    vec128 = jnp.zeros((depth, 8, 128), F32)
    vec128 = vec128.at[:, 0, :H_B].set(dt_bias).at[:, 1, :H_B].set(a_log).at[:, 2, :].set(gla_b_lr)
    vec256 = jnp.zeros((depth, 8, 256), F32)
    vec256 = (vec256.at[:, 0, :].set(jnp.repeat(d_skip, P_B, axis=1)).at[:, 1, :].set(ssd_norm_g)
              .at[:, 2, :].set(jnp.tile(gla_norm_g, (1, H_C))))
    wlr = jnp.zeros((depth, 128, 128), F32).at[:, MISC_LR:MISC_LR + GATE_RANK, :].set(gla_w_lr).astype(BF16)
    router_pad = jnp.zeros(router_w.shape[:2] + (128,), F32).at[..., :N_EXPERTS].set(router_w)
    lam_init = [0.8 - 0.6 * math.exp(-0.3 * l) for l in range(depth)]
    lam_consts = jnp.asarray(np.array([[li, 1.0 - li] for li in lam_init], np.float32).reshape(-1))
    return dict(w_pack=w_pack, vec128=vec128, vec256=vec256, wlr=wlr, router_pad=router_pad,
                lam_consts=lam_consts, subln_g3=subln_g.reshape(depth, 1, DV_A),
                convb3=conv_b.reshape(depth, 1, XBC_B))


def _pick(n, pref):
    if n <= pref:
        return n
    t = pref
    while n % t:
        t //= 2
    return t


def _channel_mixer(l, depth, h2, x1, mod, pk, wts, tm, moe_blk, final):
    i = l // 2
    lmod = jnp.array([l], I32)
    if l % 2 == 0:
        return dense_ffn(jnp.array([i], I32), lmod, h2, wts['ffn_g'], wts['ffn_u'], wts['ffn_d'], x1, mod,
                         wts['final_g'], tm, _pick(D_FF, 1408), final)
    mod_g2 = mod[l, :, :, 5 * D_MODEL:6 * D_MODEL]
    return moe_ffn(i, h2, x1, mod_g2, pk['router_pad'], wts['moe_g'], wts['moe_u'], wts['moe_d'], wts['final_g'],
                   _pick(x1.shape[0] * x1.shape[1], 256), moe_blk, _pick(D_FF, 1408), final)


def _run_prompt(x, mod, pk, wts, lambda_qk, conv_w, rel_bias):
    b, seq, d = x.shape
    depth = mod.shape[0]
    tm = _pick(seq // 2, 512)
    bias_tiles = attention_bias_tiles(rel_bias, tm)
    k_all = jnp.zeros((depth, b, W_A, seq), F32)
    v_all = jnp.zeros((depth, b, H_A * seq, DV_A), F32)
    convs, ssms, glas = [], [], []
    for l in range(depth):
        lidx = jnp.array([l], I32)
        q, rest, kt, vb, k_all, v_all = in_projection_prompt(lidx, x, mod, pk['w_pack'], k_all, v_all, tm, 2 * tm)
        oa = attention_prompt(lidx, pk['lam_consts'], lambda_qk, pk['subln_g3'], q, kt, vb, bias_tiles, tm)
        ob, oc, conv_s, ssm_s, gla_s = bc_mixers_prompt(lidx, rest, conv_w, pk['convb3'], pk['vec128'],
                                                        pk['vec256'], pk['wlr'])
        moe = l % 2 == 1
        x1, h2 = out_projection(lidx, oa, ob, oc, wts['w_out'], x, mod, tm, F32 if moe else BF16)
        x = _channel_mixer(l, depth, h2, x1, mod, pk, wts, tm, _pick(b * seq, 512), l == depth - 1)
        convs.append(conv_s)
        ssms.append(_unpack_ssm_state(ssm_s))
        glas.append(_unpack_gla_state(gla_s))
    ks = jnp.transpose(k_all.reshape(depth, b, 2 * H_A, DK_A, seq), (0, 1, 4, 2, 3))
    vs = v_all.reshape(depth, b, seq, H_A, DV_A)
    return x, ks, vs, jnp.stack(convs), jnp.stack(ssms), jnp.stack(glas)


def _run_decode(x, mod, pk, wts, lambda_qk, conv_w, rel_bias, cache_k, cache_v, page_table,
                state_conv, state_ssm, state_gla, d_skip, ssd_norm_g, gla_norm_g):
    bd = x.shape[0]
    depth = mod.shape[0]
    n_pages = page_table.shape[1]
    past = n_pages * PAGE
    pps = _pick(n_pages, 16)
    xg = x.reshape(1, bd, D_MODEL)
    pool = cache_k.shape[1]
    assert PAGE >= T5_MAX_DIST
    cache_k4 = jnp.transpose(cache_k, (0, 1, 3, 4, 2)).reshape(depth, pool, 2 * H_A * DK_A, PAGE)
    cache_v4 = cache_v.reshape(depth, pool, PAGE * H_A, DV_A)
    table = _t5_table(rel_bias, PAGE + 1)
    dist = past - ((n_pages - 1) * PAGE + jnp.arange(PAGE))
    tab = jnp.concatenate([jnp.repeat(table[:, dist], 2, axis=0),
                           jnp.broadcast_to(jnp.repeat(table[:, 0], 2)[:, None], (2 * H_A, PAGE))], axis=0)
    conv_t = jnp.swapaxes(state_conv, 1, 2)
    grp = np.arange(H_B) // (H_B // G_B)
    dsk4 = d_skip.reshape(depth, 1, H_B, 1, 1)
    ng4 = ssd_norm_g.reshape(depth, 1, H_B, P_B, 1)
    gng4 = gla_norm_g.reshape(depth, 1, 1, 1, DV_C)
    pt_flat = page_table.reshape(-1).astype(I32)
    ks, vs, convs, ssms, glas = [], [], [], [], []
    for l in range(depth):
        lidx = jnp.array([l], I32)
        q, k, v, rest = in_projection(lidx, xg, mod, pk['w_pack'], bd)
        sidx = jnp.concatenate([pt_flat, lidx])
        oa = attention_decode(sidx, pk['lam_consts'], lambda_qk, pk['subln_g3'], q.reshape(bd, 1, W_A),
                              k.reshape(bd, 1, W_A), v.reshape(bd, 1, W_A), tab, cache_k4, cache_v4, n_pages, pps)
        rest2 = rest.reshape(bd, W_REST)
        xbc, nbuf, dt, dec, eg = bc_decode_pre(lidx, rest2, conv_t, conv_w, pk['convb3'], pk['vec128'], pk['wlr'])
        x4 = xbc[:, :W_B].reshape(bd, H_B, P_B, 1)
        b4 = xbc[:, W_B:W_B + G_B * N_B].reshape(bd, G_B, 1, N_B)[:, grp]
        c4 = xbc[:, W_B + G_B * N_B:].reshape(bd, G_B, 1, N_B)[:, grp]
        dt4 = dt[:, :H_B].reshape(bd, H_B, 1, 1)
        dec4 = dec[:, :H_B].reshape(bd, H_B, 1, 1)
        bz4 = rest2[:, R_BZ:R_BZ + W_B].reshape(bd, H_B, P_B, 1)
        q4 = (rest2[:, R_CQ:R_CQ + 128] * (DK_C ** -0.5)).reshape(bd, H_C, DK_C, 1)
        k4 = rest2[:, R_CK:R_CK + 128].reshape(bd, H_C, DK_C, 1)
        v4 = rest2[:, R_CV:R_CV + W_C].reshape(bd, H_C, 1, DV_C)
        eg4 = eg.reshape(bd, H_C, DK_C, 1)
        cg4 = rest2[:, R_CG:R_CG + W_C].reshape(bd, H_C, 1, DV_C)
        ssm_s, gla_s, ob4, oc4 = bc_decode_state(lidx, state_ssm, state_gla, x4, b4, c4, dt4, dec4, bz4, dsk4, ng4,
                                                 q4, k4, v4, eg4, cg4, gng4)
        ob = ob4.reshape(1, bd, W_B).astype(BF16)
        oc = oc4.reshape(1, bd, W_C).astype(BF16)
        moe = l % 2 == 1
        x1, h2 = out_projection(lidx, oa.reshape(1, bd, W_A), ob, oc, wts['w_out'], xg, mod, bd, F32 if moe else BF16)
        xg = _channel_mixer(l, depth, h2, x1, mod, pk, wts, bd, 64, l == depth - 1)
        ks.append(k.reshape(bd, 1, 2 * H_A, DK_A))
        vs.append(v.reshape(bd, 1, H_A, DV_A))
        convs.append(jnp.swapaxes(nbuf, 0, 1))
        ssms.append(ssm_s)
        glas.append(gla_s)
    return (xg.reshape(bd, 1, D_MODEL), jnp.stack(ks), jnp.stack(vs), jnp.stack(convs), jnp.stack(ssms),
            jnp.stack(glas))


def kernel(x_prompt, x_sample, c_prompt, c_sample, cache_k, cache_v, page_table, state_conv, state_ssm, state_gla,
           w_ada, b_ada, w_in, w_out, rel_bias, lambda_qk, subln_g, conv_w, conv_b, dt_bias, a_log, d_skip,
           ssd_norm_g, gla_w_lr, gla_b_lr, gla_norm_g, ffn_w_gate, ffn_w_up, ffn_w_down, router_w,
           moe_w_gate, moe_w_up, moe_w_down, final_norm_g):
    depth = w_in.shape[0]
    bp, bd = c_prompt.shape[0], c_sample.shape[0]
    pk = _pack_params(w_in, rel_bias, subln_g, conv_b, dt_bias, a_log, d_skip, ssd_norm_g, gla_w_lr, gla_b_lr,
                      gla_norm_g, router_w)
    wts = dict(w_out=w_out.astype(BF16), ffn_g=ffn_w_gate.astype(BF16), ffn_u=ffn_w_up.astype(BF16),
               ffn_d=ffn_w_down.astype(BF16), moe_g=moe_w_gate.astype(BF16), moe_u=moe_w_up.astype(BF16),
               moe_d=moe_w_down.astype(BF16), final_g=final_norm_g.reshape(1, D_MODEL))
    n_c = bp + bd
    n_cp = -(-n_c // 8) * 8
    c_all = jnp.concatenate([c_prompt, c_sample, jnp.zeros((n_cp - n_c, D_MODEL), F32)], axis=0)
    mod = ada_modulation(c_all, w_ada, b_ada)
    mod_p = mod[:, :bp].reshape(depth, bp, 1, 6 * D_MODEL)
    mod_d = mod[:, bp:n_c].reshape(depth, 1, bd, 6 * D_MODEL)
    yp, kp, vp, convp, ssmp, glap = _run_prompt(x_prompt, mod_p, pk, wts, lambda_qk, conv_w, rel_bias)
    yd, kd, vd, convd, ssmd, glad = _run_decode(x_sample, mod_d, pk, wts, lambda_qk, conv_w, rel_bias, cache_k,
                                                cache_v, page_table, state_conv, state_ssm, state_gla, d_skip,
                                                ssd_norm_g, gla_norm_g)
    return (yp, yd, kp, vp, convp, ssmp, glap, kd, vd, convd, ssmd, glad)
```

```python
import functools
import math

import numpy as np
import jax
import jax.numpy as jnp
from jax import lax
from jax.experimental import pallas as pl
from jax.experimental.pallas import tpu as pltpu

F32 = jnp.float32
BF16 = jnp.bfloat16
I32 = jnp.int32

D_MODEL = 1024
H_A, DK_A, DV_A = 4, 64, 128
W_A = H_A * DV_A
H_B, P_B, G_B, N_B, CONV_W = 4, 64, 2, 64, 4
W_B = H_B * P_B
XBC_B = W_B + 2 * G_B * N_B
H_C, DK_C, DV_C = 4, 32, 64
W_C = H_C * DV_C
GATE_RANK, GATE_NORM = 16, 16.0
T5_BUCKETS, T5_MAX_DIST = 32, 128
D_FF, N_EXPERTS, TOP_K = 2816, 8, 2
EPS = 1e-6
PAGE = 128
CHUNK = 128
FF_TILE = D_FF
OFF_AQ = 0
OFF_AK = OFF_AQ + 2 * H_A * DK_A
OFF_AV = OFF_AK + 2 * H_A * DK_A
OFF_BZ = OFF_AV + W_A
OFF_BX = OFF_BZ + W_B
OFF_BDT = OFF_BX + XBC_B
OFF_CQ = OFF_BDT + H_B
OFF_CK = OFF_CQ + H_C * DK_C
OFF_CV = OFF_CK + H_C * DK_C
OFF_CG = OFF_CV + W_C
OFF_CLR = OFF_CG + W_C
D_IN = OFF_CLR + GATE_RANK
PK_REST = 3 * W_A
R_BZ, R_BX, R_CQ, R_CK, R_CV, R_CG, R_MISC = 0, 256, 768, 896, 1024, 1280, 1536
W_REST = R_MISC + 128
D_PACK = PK_REST + W_REST
MISC_DT, MISC_LR = 0, H_B
NEG = -1e30
VMEM_LIMIT = 56 * 1024 * 1024


def _cparams(sem):
    return pltpu.CompilerParams(dimension_semantics=sem, vmem_limit_bytes=VMEM_LIMIT)


def _rms(x):
    return x * lax.rsqrt(jnp.mean(x * x, axis=-1, keepdims=True) + EPS)


def _silu(x):
    return x * jax.nn.sigmoid(x)


def _softplus(x):
    return jnp.maximum(x, 0.0) + jnp.log1p(jnp.exp(-jnp.abs(x)))


def _split2(x):
    hi = x.astype(BF16)
    return hi, (x - hi.astype(F32)).astype(BF16)


def _split3(x):
    hi = x.astype(BF16)
    r = x - hi.astype(F32)
    mid = r.astype(BF16)
    return hi, mid, (r - mid.astype(F32)).astype(BF16)


def _dotf(a, b):
    return jnp.dot(a, b, preferred_element_type=F32)


def _lmat_exact(mat01, x):
    hi, mid, lo = _split3(x)
    return _dotf(mat01, hi) + _dotf(mat01, mid) + _dotf(mat01, lo)


def _rmat_exact(x, mat01, pieces=3):
    if pieces == 2:
        hi, lo = _split2(x)
        return _dotf(hi, mat01) + _dotf(lo, mat01)
    hi, mid, lo = _split3(x)
    return _dotf(hi, mat01) + _dotf(mid, mat01) + _dotf(lo, mat01)


def _iota(shape, axis):
    return lax.broadcasted_iota(I32, shape, axis)


def _ada_kernel(c_ref, w_ref, b_ref, o_ref):
    c = c_ref[...]
    o_ref[...] = _dotf(_silu(c).astype(BF16), w_ref[...].astype(BF16)) + b_ref[...]


def ada_modulation(c_all, w_ada, b_ada):
    depth, d, n = w_ada.shape
    bc = c_all.shape[0]
    tn = 1536 if n % 1536 == 0 else n
    return pl.pallas_call(
        _ada_kernel,
        out_shape=jax.ShapeDtypeStruct((depth, bc, n), F32),
        grid=(depth, n // tn),
        in_specs=[pl.BlockSpec((bc, d), lambda l, j: (0, 0)),
                  pl.BlockSpec((None, d, tn), lambda l, j: (l, 0, j)),
                  pl.BlockSpec((None, 1, tn), lambda l, j: (l, 0, j))],
        out_specs=pl.BlockSpec((None, bc, tn), lambda l, j: (l, 0, j)),
        compiler_params=_cparams(("arbitrary", "arbitrary")),
        name="ada_modulation",
    )(c_all, w_ada, b_ada.reshape(depth, 1, n))


def _mod_spec(tm, per_row, chunk):
    if per_row:
        return pl.BlockSpec((None, None, tm, D_MODEL), lambda g, i, l: (l[0], g, i, chunk))
    return pl.BlockSpec((None, None, 1, D_MODEL), lambda g, i, l: (l[0], g, 0, chunk))


def _in_proj_kernel(l_ref, x_ref, sh_ref, sc_ref, w_ref, q_ref, k_ref, v_ref, r_ref):
    del l_ref
    h = (_rms(x_ref[...]) * (1.0 + sc_ref[...]) + sh_ref[...]).astype(BF16)
    q_ref[...] = (_dotf(h, w_ref[:, 0:W_A]) * (DK_A ** -0.5)).astype(BF16)
    k_ref[...] = _dotf(h, w_ref[:, W_A:2 * W_A])
    v_ref[...] = _dotf(h, w_ref[:, 2 * W_A:3 * W_A])
    r_ref[...] = _dotf(h, w_ref[:, PK_REST:D_PACK])


def in_projection(lidx, x, mod, w_pack, tm):
    g, r, d = x.shape
    per_row = mod.shape[2] > 1
    row = lambda w, dt: jax.ShapeDtypeStruct((g, r, w), dt)
    ospec = lambda w: pl.BlockSpec((None, tm, w), lambda gi, i, l: (gi, i, 0))
    out_specs = [ospec(W_A), ospec(W_A), ospec(W_A), ospec(W_REST)]
    out_shape = [row(W_A, BF16), row(W_A, F32), row(W_A, F32), row(W_REST, F32)]
    grid_spec = pltpu.PrefetchScalarGridSpec(
        num_scalar_prefetch=1, grid=(g, r // tm),
        in_specs=[pl.BlockSpec((None, tm, d), lambda gi, i, l: (gi, i, 0)),
                  _mod_spec(tm, per_row, 0), _mod_spec(tm, per_row, 1),
                  pl.BlockSpec((None, d, D_PACK), lambda gi, i, l: (l[0], 0, 0))],
        out_specs=out_specs)
    return pl.pallas_call(
        _in_proj_kernel,
        out_shape=out_shape,
        grid_spec=grid_spec,
        compiler_params=_cparams(("arbitrary", "arbitrary")),
        name="in_projection",
    )(lidx, x, mod, mod, w_pack)


def _in_proj_prompt_kernel(l_ref, x_ref, sh_ref, sc_ref, w_ref, kall_in_ref, vall_in_ref,
                           q_ref, r_ref, kt_ref, vb_ref, kall_ref, vall_ref, k_scr):
    del l_ref, kall_in_ref, vall_in_ref
    tm = x_ref.shape[0]
    h = (_rms(x_ref[...]) * (1.0 + sc_ref[...]) + sh_ref[...]).astype(BF16)
    q_ref[...] = (_dotf(h, w_ref[:, 0:W_A]) * (DK_A ** -0.5)).astype(BF16)
    k_scr[...] = _dotf(h, w_ref[:, W_A:2 * W_A])
    kt = k_scr[...].T
    kall_ref[...] = kt
    kt_ref[...] = kt.astype(BF16)
    v = _dotf(h, w_ref[:, 2 * W_A:3 * W_A])
    vb_ref[...] = v.astype(BF16)
    for head in range(H_A):
        vall_ref[pl.ds(head, tm, stride=H_A), :] = v[:, head * DV_A:(head + 1) * DV_A]
    r_ref[...] = _dotf(h, w_ref[:, PK_REST:D_PACK])


def in_projection_prompt(lidx, x, mod, w_pack, k_all, v_all, tm, kblk):
    b, seq, d = x.shape
    sub = kblk // tm
    act = lambda w: pl.BlockSpec((None, tm, w), lambda bi, i, l: (bi, i, 0))
    grid_spec = pltpu.PrefetchScalarGridSpec(
        num_scalar_prefetch=1, grid=(b, seq // tm),
        in_specs=[act(d), _mod_spec(tm, False, 0), _mod_spec(tm, False, 1),
                  pl.BlockSpec((None, d, D_PACK), lambda bi, i, l: (l[0], 0, 0)),
                  pl.BlockSpec(memory_space=pl.ANY), pl.BlockSpec(memory_space=pl.ANY)],
        out_specs=[act(W_A), act(W_REST),
                   pl.BlockSpec((None, None, W_A, tm), lambda bi, i, l: (bi, i // sub, 0, i % sub)),
                   act(W_A),
                   pl.BlockSpec((None, None, W_A, tm), lambda bi, i, l: (l[0], bi, 0, i)),
                   pl.BlockSpec((None, None, H_A * tm, DV_A), lambda bi, i, l: (l[0], bi, i, 0))],
        scratch_shapes=[pltpu.VMEM((tm, W_A), F32)])
    return pl.pallas_call(
        _in_proj_prompt_kernel,
        out_shape=[jax.ShapeDtypeStruct((b, seq, W_A), BF16), jax.ShapeDtypeStruct((b, seq, W_REST), F32),
                   jax.ShapeDtypeStruct((b, seq // kblk, W_A, kblk), BF16), jax.ShapeDtypeStruct((b, seq, W_A), BF16),
                   jax.ShapeDtypeStruct(k_all.shape, F32), jax.ShapeDtypeStruct(v_all.shape, F32)],
        grid_spec=grid_spec,
        input_output_aliases={5: 4, 6: 5},
        compiler_params=_cparams(("arbitrary", "arbitrary")),
        name="in_projection_prompt",
    )(lidx, x, mod, mod, w_pack, k_all, v_all)


def _lambda_value(lqk, lam_init):
    s01 = jnp.sum(lqk[0:1, :] * lqk[1:2, :], axis=1, keepdims=True)
    s23 = jnp.sum(lqk[2:3, :] * lqk[3:4, :], axis=1, keepdims=True)
    return jnp.exp(s01) - jnp.exp(s23) + lam_init


def _attn_prompt_kernel(l_ref, lam_ref, lqk_ref, g_ref, q_ref, kt_ref, v_ref, bias_ref, o_ref,
                        vaug_ref, m_ref, acc_ref, *, blk):
    qi = pl.program_id(2)

    @pl.when(qi == 0)
    def _():
        vaug_ref[:, 0:DV_A] = v_ref[...]
        ones_col = _iota((vaug_ref.shape[0], DV_A), 1) == 0
        vaug_ref[:, DV_A:2 * DV_A] = jnp.where(ones_col, 1.0, 0.0).astype(BF16)

    q = q_ref[...]
    lane = _iota(q.shape, 1)
    zero = jnp.zeros_like(q)
    qm = (jnp.where(lane < DK_A, q, zero), jnp.where(lane >= DK_A, q, zero))
    m_ref[...] = jnp.full(m_ref.shape, NEG, F32)
    acc_ref[...] = jnp.zeros(acc_ref.shape, F32)

    def update(kt, va, bias):
        for m in range(2):
            s = _dotf(qm[m], kt)
            if bias is not None:
                pieces = [s[:, i * blk:(i + 1) * blk] if b is None else s[:, i * blk:(i + 1) * blk] + b
                          for i, b in enumerate(bias)]
                s = pieces[0] if len(pieces) == 1 else jnp.concatenate(pieces, axis=1)
            m_old = m_ref[m]
            m_new = jnp.maximum(m_old, jnp.broadcast_to(jnp.max(s, axis=1, keepdims=True), m_old.shape))
            p = jnp.exp(s - jnp.tile(m_new, (1, s.shape[1] // 128))).astype(BF16)
            acc_ref[m] = jnp.tile(jnp.exp(m_old - m_new), (1, 2)) * acc_ref[m] + _dotf(p, va)
            m_ref[m] = m_new

    def key_block(j):
        return kt_ref[j], vaug_ref[pl.ds(pl.multiple_of(j * 2 * blk, 2 * blk), 2 * blk), :]

    def far(j, carry):
        update(*key_block(j), None)
        return carry

    a = qi >> 1
    odd = (qi & 1) == 1
    lax.fori_loop(0, jnp.where(odd, a, jnp.maximum(a - 1, 0)), far, 0)

    @pl.when(odd)
    def _():
        update(*key_block(a), (bias_ref[1], bias_ref[0]))

    @pl.when(jnp.logical_and(jnp.logical_not(odd), a >= 1))
    def _():
        update(*key_block(a - 1), (None, bias_ref[1]))

    @pl.when(jnp.logical_not(odd))
    def _():
        start = pl.multiple_of(a * 2 * blk, 2 * blk)
        update(kt_ref[a, :, 0:blk], vaug_ref[pl.ds(start, blk), :], (bias_ref[0],))

    layer = l_ref[0]
    lam = _lambda_value(lqk_ref[...], lam_ref[2 * layer])
    acc0 = acc_ref[0]
    acc1 = acc_ref[1]
    o = (acc0[:, 0:DV_A] * (1.0 / acc0[:, DV_A:DV_A + 1])
         - lam * (acc1[:, 0:DV_A] * (1.0 / acc1[:, DV_A:DV_A + 1])))
    o_ref[...] = (_rms(o) * g_ref[...] * lam_ref[2 * layer + 1]).astype(BF16)


def _t5_table(rel_bias, n):
    dist = jnp.arange(n)
    max_exact = T5_BUCKETS // 2
    nf = jnp.maximum(dist, 1).astype(F32)
    large = max_exact + (jnp.log(nf / max_exact) / math.log(T5_MAX_DIST / max_exact)
                         * (T5_BUCKETS - max_exact)).astype(I32)
    bucket = jnp.where(dist < max_exact, dist, jnp.minimum(large, T5_BUCKETS - 1))
    return (rel_bias[bucket] - rel_bias[T5_BUCKETS - 1][None, :]).T.astype(F32)


def _bias_tiles_kernel(rel_ref, o_ref):
    h = pl.program_id(0)
    blk = o_ref.shape[0]
    d = _iota((blk, blk), 0) - _iota((blk, blk), 1) + pl.program_id(1) * blk
    dist = jnp.maximum(d, 0)
    max_exact = T5_BUCKETS // 2
    nf = jnp.maximum(dist, 1).astype(F32)
    large = max_exact + (jnp.log(nf / max_exact) / math.log(T5_MAX_DIST / max_exact)
                         * (T5_BUCKETS - max_exact)).astype(I32)
    bucket = jnp.where(dist < max_exact, dist, jnp.minimum(large, T5_BUCKETS - 1))
    sat = rel_ref[(T5_BUCKETS - 1) * H_A + h]
    val = jnp.zeros((blk, blk), F32)
    for b in range(T5_BUCKETS - 1):
        val = jnp.where(bucket == b, rel_ref[b * H_A + h] - sat, val)
    o_ref[...] = jnp.where(d >= 0, val, NEG)


def attention_bias_tiles(rel_bias, blk):
    return pl.pallas_call(
        _bias_tiles_kernel,
        out_shape=jax.ShapeDtypeStruct((H_A, 2, blk, blk), F32),
        grid=(H_A, 2),
        in_specs=[pl.BlockSpec(memory_space=pltpu.SMEM)],
        out_specs=pl.BlockSpec((None, None, blk, blk), lambda h, t: (h, t, 0, 0)),
        compiler_params=_cparams(("arbitrary", "arbitrary")),
        name="attention_bias_tiles",
    )(rel_bias.reshape(-1))


def attention_prompt(lidx, lam_consts, lambda_qk, subln_g3, q, kt, vb, bias_tiles, blk):
    b, seq, _ = q.shape
    assert blk >= T5_MAX_DIST and blk % 128 == 0 and kt.shape == (b, seq // (2 * blk), W_A, 2 * blk)
    grid_spec = pltpu.PrefetchScalarGridSpec(
        num_scalar_prefetch=1, grid=(b, H_A, seq // blk),
        in_specs=[pl.BlockSpec(memory_space=pltpu.SMEM),
                  pl.BlockSpec((None, 4, DK_A), lambda bi, h, i, l: (l[0], 0, 0)),
                  pl.BlockSpec((None, 1, DV_A), lambda bi, h, i, l: (l[0], 0, 0)),
                  pl.BlockSpec((None, blk, DV_A), lambda bi, h, i, l: (bi, i, h)),
                  pl.BlockSpec((None, seq // (2 * blk), DV_A, 2 * blk), lambda bi, h, i, l: (bi, 0, h, 0)),
                  pl.BlockSpec((None, seq, DV_A), lambda bi, h, i, l: (bi, 0, h)),
                  pl.BlockSpec((None, 2, blk, blk), lambda bi, h, i, l: (h, 0, 0, 0))],
        out_specs=pl.BlockSpec((None, blk, DV_A), lambda bi, h, i, l: (bi, i, h)),
        scratch_shapes=[pltpu.VMEM((seq, 2 * DV_A), BF16), pltpu.VMEM((2, blk, 128), F32),
                        pltpu.VMEM((2, blk, 2 * DV_A), F32)])
    return pl.pallas_call(
        functools.partial(_attn_prompt_kernel, blk=blk),
        out_shape=jax.ShapeDtypeStruct((b, seq, W_A), BF16),
        grid_spec=grid_spec,
        compiler_params=_cparams(("arbitrary", "arbitrary", "arbitrary")),
        name="attention_prompt",
    )(lidx, lam_consts, lambda_qk, subln_g3, q, kt, vb, bias_tiles)


def _bc_prompt_kernel(l_ref, rest_ref, convw_ref, convb_ref, v128_ref, v256_ref, wlr_ref,
                      ob_ref, oc_ref, conv_out_ref, ssm_out_ref, gla_out_ref,
                      xpad_ref, sbd_ref, sg_ref, b_ref, oi_ref, *, nsub):
    del l_ref
    c = pl.program_id(1)
    q = CHUNK
    rows = nsub * q

    @pl.when(c == 0)
    def _():
        xpad_ref[0:8, :] = jnp.zeros((8, XBC_B), F32)
        sbd_ref[...] = jnp.zeros(sbd_ref.shape, F32)
        sg_ref[...] = jnp.zeros(sg_ref.shape, F32)

    xpad_ref[8:8 + rows, :] = rest_ref[:, R_BX:R_BX + XBC_B]
    conv = convb_ref[...]
    for j in range(CONV_W):
        conv = conv + convw_ref[j:j + 1, :] * xpad_ref[5 + j:5 + j + rows, :]
    xbc_all = _silu(conv)
    tail = xpad_ref[5 + rows:8 + rows, :]
    conv_out_ref[...] = tail
    xpad_ref[5:8, :] = tail

    row = _iota((q, q), 0)
    col = _iota((q, q), 1)
    causal = row >= col
    tril = jnp.where(causal, 1.0, 0.0).astype(BF16)
    lane1 = _iota((1, 128), 1)
    is_head = lane1 < H_B
    expand = jnp.where(_iota((128, W_B), 0) == (_iota((128, W_B), 1) >> 6), 1.0, 0.0).astype(BF16)
    grp_lane = _iota((q, 128), 1) >> 6
    head_lane = _iota((q, W_B), 1) >> 6
    same_grp = (_iota((128, W_B), 0) >> 6) == (_iota((128, W_B), 1) >> 7)
    same_head = (_iota((128, W_C), 0) >> 5) == (_iota((128, W_C), 1) >> 6)
    vhead = _iota((q, W_C), 1) >> 6
    a_neg = jnp.where(is_head, -jnp.exp(v128_ref[1:2, :]), 0.0)

    state = sbd_ref[...]
    gla = []
    for u in range(nsub):
        r0 = u * q
        xbc = xbc_all[r0:r0 + q, :]
        misc = rest_ref[r0:r0 + q, R_MISC:R_MISC + 128]
        xs = xbc[:, 0:W_B]
        bmat = xbc[:, W_B:W_B + G_B * N_B].astype(BF16)
        cmat = xbc[:, W_B + G_B * N_B:XBC_B].astype(BF16)
        dt = jnp.where(is_head, _softplus(misc + v128_ref[0:1, :]), 0.0)
        cum = _lmat_exact(tril, dt * a_neg)
        cum_t = cum.T
        ecum = jnp.exp(cum)
        wgt = jnp.exp(cum[q - 1:q, :] - cum) * dt
        dt_e = _rmat_exact(dt, expand, 2)
        ecum_e = _rmat_exact(ecum, expand, 2)
        wgt_e = _rmat_exact(wgt, expand, 2)
        zero_b = jnp.zeros_like(cmat)
        gmat = [lax.dot_general(jnp.where(grp_lane == g, cmat, zero_b), bmat, (((1,), (1,)), ((), ())),
                                preferred_element_type=F32) for g in range(G_B)]
        mcat = []
        for h in range(H_B):
            diff = cum[:, h:h + 1] - cum_t[h:h + 1, :]
            dec = jnp.exp(jnp.where(causal, diff, NEG))
            mcat.append((gmat[h // (H_B // G_B)] * dec).astype(BF16))
        mcat = jnp.concatenate(mcat, axis=1)
        dtx = xs * dt_e
        xbd = jnp.concatenate([jnp.where(head_lane == h, dtx, 0.0) for h in range(H_B)], axis=0).astype(BF16)
        y = _dotf(mcat, xbd) + _dotf(cmat, state.astype(BF16)) * ecum_e
        upd = lax.dot_general(bmat, (xs * wgt_e).astype(BF16), (((0,), (0,)), ((), ())), preferred_element_type=F32)
        state = state * ecum_e[q - 1:q, :] + jnp.where(same_grp, upd, 0.0)
        y = y + v256_ref[0:1, :] * xs
        ob_ref[r0:r0 + q, :] = (_rms(y * _silu(rest_ref[r0:r0 + q, R_BZ:R_BZ + W_B])) * v256_ref[1:2, :]).astype(BF16)
        qg = rest_ref[r0:r0 + q, R_CQ:R_CQ + 128] * (DK_C ** -0.5)
        kg = rest_ref[r0:r0 + q, R_CK:R_CK + 128]
        vg = rest_ref[r0:r0 + q, R_CV:R_CV + W_C]
        gk = _dotf(misc.astype(BF16), wlr_ref[...]) + v128_ref[2:3, :]
        gate = -_softplus(-gk) * (1.0 / GATE_NORM)
        bcum = _lmat_exact(tril, gate)
        gla.append((qg, kg, vg, bcum, qg * jnp.exp(bcum)))
    sbd_ref[...] = state
    ssm_out_ref[...] = state
    span = jnp.max(-gla[0][3][q - 1:q, :])
    for u in range(1, nsub):
        span = jnp.maximum(span, jnp.max(-gla[u][3][q - 1:q, :]))

    @pl.when(span <= 80.0)
    def _():
        causal4 = _iota((q, H_C * q), 0) >= (_iota((q, H_C * q), 1) & (q - 1))
        kmask = (_iota((128, 4 * q), 0) >> 5) == (_iota((128, 4 * q), 1) >> 7)
        for u in range(nsub):
            qg, kg, vg, bcum, qe = gla[u]
            ke = kg * jnp.exp(-bcum)
            kbd = jnp.where(kmask, jnp.tile(ke.T, (1, H_C)), 0.0).astype(BF16)
            att = jnp.where(causal4, _dotf(qe.astype(BF16), kbd), 0.0).astype(BF16)
            vbd = jnp.concatenate([jnp.where(vhead == h, vg, 0.0) for h in range(H_C)], axis=0).astype(BF16)
            oi_ref[u * q:(u + 1) * q, :] = _dotf(att, vbd)

    @pl.when(span > 80.0)
    def _():
        ind = jnp.where((_iota((128, W_C), 0) >> 5) == (_iota((128, W_C), 1) >> 6), 1.0, 0.0).astype(BF16)
        trow = _iota((q, 128), 0)
        for u in range(nsub):
            qg, kg, vg, bcum, qe = gla[u]
            b_ref[...] = bcum

            def body(grp, acc, u=u, qg=qg, bcum=bcum):
                base = pl.multiple_of(grp * 8, 8)
                ks8 = rest_ref[pl.ds(u * q + base, 8), R_CK:R_CK + 128]
                vs8 = rest_ref[pl.ds(u * q + base, 8), R_CV:R_CV + W_C]
                bs8 = b_ref[pl.ds(base, 8), :]
                for j in range(8):
                    d = jnp.exp(jnp.minimum(bcum - bs8[j:j + 1, :], 0.0)) * qg * ks8[j:j + 1, :]
                    d = jnp.where(trow >= base + j, d, 0.0)
                    acc = acc + _dotf(d.astype(BF16), ind) * vs8[j:j + 1, :]
                return acc

            oi_ref[u * q:(u + 1) * q, :] = lax.fori_loop(0, q // 8, body, jnp.zeros((q, W_C), F32))

    sg = sg_ref[...]
    avg = jnp.where((_iota((W_C, W_C), 0) >> 6) == (_iota((W_C, W_C), 1) >> 6), 1.0, 0.0).astype(BF16)
    for u in range(nsub):
        r0 = u * q
        qg, kg, vg, bcum, qe = gla[u]
        o = oi_ref[r0:r0 + q, :] + _dotf(qe.astype(BF16), sg.astype(BF16))
        k2 = (kg * jnp.exp(bcum[q - 1:q, :] - bcum)).astype(BF16)
        updg = lax.dot_general(k2, vg.astype(BF16), (((0,), (0,)), ((), ())), preferred_element_type=F32)
        sg = sg * jnp.exp(bcum.T[:, q - 1:q]) + jnp.where(same_head, updg, 0.0)
        ms = _rmat_exact(o * o, avg, 2) * (1.0 / DV_C)
        oc = o * lax.rsqrt(ms + EPS) * v256_ref[2:3, :] * _silu(rest_ref[r0:r0 + q, R_CG:R_CG + W_C])
        oc_ref[r0:r0 + q, :] = oc.astype(BF16)
    sg_ref[...] = sg
    gla_out_ref[...] = sg


def bc_mixers_prompt(lidx, rest, convw, convb3, vec128, vec256, wlr):
    b, seq, _ = rest.shape
    nsub = next(n for n in (4, 2, 1) if seq % (n * CHUNK) == 0)
    rows = nsub * CHUNK
    par = lambda r, w: pl.BlockSpec((None, r, w), lambda bi, c, l: (l[0], 0, 0))
    st = lambda r, w: pl.BlockSpec((None, r, w), lambda bi, c, l: (bi, 0, 0))
    grid_spec = pltpu.PrefetchScalarGridSpec(
        num_scalar_prefetch=1, grid=(b, seq // rows),
        in_specs=[pl.BlockSpec((None, rows, W_REST), lambda bi, c, l: (bi, c, 0)),
                  par(CONV_W, XBC_B), par(1, XBC_B), par(8, 128), par(8, 256), par(128, 128)],
        out_specs=[pl.BlockSpec((None, rows, W_B), lambda bi, c, l: (bi, c, 0)),
                   pl.BlockSpec((None, rows, W_C), lambda bi, c, l: (bi, c, 0)),
                   st(CONV_W - 1, XBC_B), st(128, W_B), st(128, W_C)],
        scratch_shapes=[pltpu.VMEM((8 + rows, XBC_B), F32), pltpu.VMEM((128, W_B), F32),
                        pltpu.VMEM((128, W_C), F32), pltpu.VMEM((CHUNK, 128), F32),
                        pltpu.VMEM((rows, W_C), F32)])
    return pl.pallas_call(
        functools.partial(_bc_prompt_kernel, nsub=nsub),
        out_shape=[jax.ShapeDtypeStruct((b, seq, W_B), BF16), jax.ShapeDtypeStruct((b, seq, W_C), BF16),
                   jax.ShapeDtypeStruct((b, CONV_W - 1, XBC_B), F32),
                   jax.ShapeDtypeStruct((b, 128, W_B), F32), jax.ShapeDtypeStruct((b, 128, W_C), F32)],
        grid_spec=grid_spec,
        compiler_params=_cparams(("arbitrary", "arbitrary")),
        name="bc_mixers_prompt",
    )(lidx, rest, convw, convb3, vec128, vec256, wlr)


def _unpack_ssm_state(sbd):
    b = sbd.shape[0]
    s = sbd.reshape(b, G_B, N_B, H_B, P_B)
    per_head = [s[:, h // (H_B // G_B), :, h, :] for h in range(H_B)]
    return jnp.swapaxes(jnp.stack(per_head, axis=1), 2, 3)


def _unpack_gla_state(sg):
    b = sg.shape[0]
    s = sg.reshape(b, H_C, DK_C, H_C, DV_C)
    return jnp.stack([s[:, h, :, h, :] for h in range(H_C)], axis=1)


def _out_proj_kernel(l_ref, oa_ref, ob_ref, oc_ref, w_ref, x_ref, g1_ref, sh2_ref, sc2_ref, x1_ref, h2_ref):
    del l_ref
    mix = (_dotf(oa_ref[...], w_ref[0:W_A, :]) + _dotf(ob_ref[...], w_ref[W_A:W_A + W_B, :])
           + _dotf(oc_ref[...], w_ref[W_A + W_B:, :]))
    x1 = x_ref[...] + g1_ref[...] * mix
    x1_ref[...] = x1
    h2_ref[...] = (_rms(x1) * (1.0 + sc2_ref[...]) + sh2_ref[...]).astype(h2_ref.dtype)


def out_projection(lidx, oa, ob, oc, w_out, x, mod, tm, h2_dtype):
    g, r, d = x.shape
    per_row = mod.shape[2] > 1
    act = lambda w: pl.BlockSpec((None, tm, w), lambda gi, i, l: (gi, i, 0))
    grid_spec = pltpu.PrefetchScalarGridSpec(
        num_scalar_prefetch=1, grid=(g, r // tm),
        in_specs=[act(W_A), act(W_B), act(W_C),
                  pl.BlockSpec((None, d, d), lambda gi, i, l: (l[0], 0, 0)),
                  act(d), _mod_spec(tm, per_row, 2), _mod_spec(tm, per_row, 3), _mod_spec(tm, per_row, 4)],
        out_specs=[act(d), act(d)])
    return pl.pallas_call(
        _out_proj_kernel,
        out_shape=[jax.ShapeDtypeStruct((g, r, d), F32), jax.ShapeDtypeStruct((g, r, d), h2_dtype)],
        grid_spec=grid_spec,
        compiler_params=_cparams(("arbitrary", "arbitrary")),
        name="out_projection",
    )(lidx, oa, ob, oc, w_out, x, mod, mod, mod)


def _ffn_kernel(i_ref, h_ref, wg_ref, wu_ref, wd_ref, x1_ref, g2_ref, fg_ref, o_ref, acc_ref, *, final):
    del i_ref
    f = pl.program_id(2)

    @pl.when(f == 0)
    def _():
        acc_ref[...] = jnp.zeros(acc_ref.shape, F32)

    h = h_ref[...]
    a = (_silu(_dotf(h, wg_ref[...])) * _dotf(h, wu_ref[...])).astype(BF16)
    acc_ref[...] += _dotf(a, wd_ref[...])

    @pl.when(f == pl.num_programs(2) - 1)
    def _():
        x2 = x1_ref[...] + g2_ref[...] * acc_ref[...]
        o_ref[...] = _rms(x2) * fg_ref[...] if final else x2


def dense_ffn(iidx, lidx_mod, h2, wg, wu, wd, x1, mod, final_g, tm, tf, final):
    g, r, d = x1.shape
    per_row = mod.shape[2] > 1
    nf = D_FF // tf
    act = lambda w: pl.BlockSpec((None, tm, w), lambda gi, i, f, s: (gi, i, 0))
    if per_row:
        g2 = pl.BlockSpec((None, None, tm, d), lambda gi, i, f, s: (s[1], gi, i, 5))
    else:
        g2 = pl.BlockSpec((None, None, 1, d), lambda gi, i, f, s: (s[1], gi, 0, 5))
    grid_spec = pltpu.PrefetchScalarGridSpec(
        num_scalar_prefetch=1, grid=(g, r // tm, nf),
        in_specs=[act(d),
                  pl.BlockSpec((None, d, tf), lambda gi, i, f, s: (s[0], 0, f)),
                  pl.BlockSpec((None, d, tf), lambda gi, i, f, s: (s[0], 0, f)),
                  pl.BlockSpec((None, tf, d), lambda gi, i, f, s: (s[0], f, 0)),
                  act(d), g2, pl.BlockSpec((1, d), lambda gi, i, f, s: (0, 0))],
        out_specs=act(d),
        scratch_shapes=[pltpu.VMEM((tm, d), F32)])
    sidx = jnp.concatenate([iidx, lidx_mod])
    return pl.pallas_call(
        functools.partial(_ffn_kernel, final=final),
        out_shape=jax.ShapeDtypeStruct((g, r, d), F32),
        grid_spec=grid_spec,
        compiler_params=_cparams(("arbitrary", "arbitrary", "arbitrary")),
        name="dense_ffn",
    )(sidx, h2, wg, wu, wd, x1, mod, final_g)


def _route_kernel(i_ref, h_ref, rw_ref, tri_ref, idx_ref, gate_ref, cnt_ref, carry_ref):
    del i_ref
    i = pl.program_id(0)

    @pl.when(i == 0)
    def _():
        carry_ref[...] = jnp.zeros(carry_ref.shape, F32)

    hh, hl = _split2(h_ref[...])
    wh, wl = _split2(rw_ref[...])
    logits = _dotf(hh, wh) + _dotf(hl, wh) + _dotf(hh, wl)
    lane = _iota(logits.shape, 1).astype(F32)
    logits = jnp.where(lane < N_EXPERTS, logits, NEG)
    m1 = jnp.max(logits, axis=1, keepdims=True)
    i1 = jnp.min(jnp.where(logits == m1, lane, 128.0), axis=1, keepdims=True)
    rest = jnp.where(lane == i1, NEG, logits)
    m2 = jnp.max(rest, axis=1, keepdims=True)
    i2 = jnp.min(jnp.where(rest == m2, lane, 128.0), axis=1, keepdims=True)
    e = jnp.exp(m2 - m1)
    g1 = 1.0 / (1.0 + e)
    sel = jnp.where(lane == i1, 1.0, 0.0) + jnp.where(lane == i2, 1.0, 0.0)
    before = _dotf(tri_ref[...], sel.astype(BF16)) + carry_ref[...]
    r1 = jnp.sum(jnp.where(lane == i1, before, 0.0), axis=1, keepdims=True)
    r2 = jnp.sum(jnp.where(lane == i2, before, 0.0), axis=1, keepdims=True)
    carry_ref[...] = carry_ref[...] + jnp.sum(sel, axis=0, keepdims=True)
    lane8 = _iota(idx_ref.shape, 1).astype(F32)
    idx_ref[...] = jnp.where(lane8 == 0.0, i1, jnp.where(lane8 == 1.0, i2, jnp.where(lane8 == 2.0, r1, r2))).astype(I32)
    gate_ref[...] = jnp.where(lane8 == 0.0, g1, e * g1)
    cnt_ref[...] = jnp.broadcast_to(carry_ref[...], cnt_ref.shape)


def moe_route(iidx, h2, router_pad, tm):
    t, d = h2.shape
    tri = jnp.asarray(np.tril(np.ones((tm, tm), np.float32), -1), BF16)
    grid_spec = pltpu.PrefetchScalarGridSpec(
        num_scalar_prefetch=1, grid=(t // tm,),
        in_specs=[pl.BlockSpec((tm, d), lambda i, s: (i, 0)),
                  pl.BlockSpec((None, d, 128), lambda i, s: (s[0], 0, 0)),
                  pl.BlockSpec((tm, tm), lambda i, s: (0, 0))],
        out_specs=[pl.BlockSpec((tm, 8), lambda i, s: (i, 0)), pl.BlockSpec((tm, 8), lambda i, s: (i, 0)),
                   pl.BlockSpec((8, 128), lambda i, s: (0, 0))],
        scratch_shapes=[pltpu.VMEM((1, 128), F32)])
    return pl.pallas_call(
        _route_kernel,
        out_shape=[jax.ShapeDtypeStruct((t, 8), I32), jax.ShapeDtypeStruct((t, 8), F32),
                   jax.ShapeDtypeStruct((8, 128), F32)],
        grid_spec=grid_spec,
        compiler_params=_cparams(("arbitrary",)),
        name="moe_route",
    )(iidx, h2, router_pad, tri)


def _row_copy(src_ref, src_row, dst_ref, dst_row, sem):
    return pltpu.make_async_copy(src_ref.at[pl.ds(src_row, 1), :], dst_ref.at[pl.ds(dst_row, 1), :], sem)


def _dispatch_kernel(dest_ref, fill_ref, h_ref, xb_ref, sem, zrow_ref, zsem):
    tm = h_ref.shape[0]

    @pl.when(pl.program_id(0) == 0)
    def _():
        zrow_ref[...] = jnp.zeros(zrow_ref.shape, F32)
        for phase in ("start", "wait"):
            for e in range(N_EXPERTS + 1):
                def fill(r, carry, phase=phase):
                    cp = _row_copy(zrow_ref, 0, xb_ref, r, zsem)
                    cp.start() if phase == "start" else cp.wait()
                    return carry
                lax.fori_loop(fill_ref[2 * e], fill_ref[2 * e + 1], fill, 0)

    def start(r, carry):
        for k in range(TOP_K):
            _row_copy(h_ref, r, xb_ref, dest_ref[0, TOP_K * r + k], sem).start()
        return carry

    lax.fori_loop(0, tm, start, 0, unroll=8)
    for k in range(TOP_K):
        pltpu.make_async_copy(h_ref, xb_ref.at[pl.ds(0, tm), :], sem).wait()


def moe_dispatch(dest3, fill, h2, n_rows, tm):
    t, d = h2.shape
    return pl.pallas_call(
        _dispatch_kernel,
        out_shape=jax.ShapeDtypeStruct((n_rows, d), F32),
        grid=(t // tm,),
        in_specs=[pl.BlockSpec((None, 1, TOP_K * tm), lambda i: (i, 0, 0), memory_space=pltpu.SMEM),
                  pl.BlockSpec(memory_space=pltpu.SMEM),
                  pl.BlockSpec((tm, d), lambda i: (i, 0))],
        out_specs=pl.BlockSpec(memory_space=pl.ANY),
        scratch_shapes=[pltpu.SemaphoreType.DMA(()), pltpu.VMEM((8, d), F32), pltpu.SemaphoreType.DMA(())],
        compiler_params=pltpu.CompilerParams(dimension_semantics=("arbitrary",), vmem_limit_bytes=VMEM_LIMIT,
                                             has_side_effects=True),
        name="moe_dispatch",
    )(dest3, fill, h2)


def _expert_kernel(s_ref, x_ref, wg_ref, wu_ref, wd_ref, y_ref, xb_ref, acc_ref):
    i = pl.program_id(0)
    f = pl.program_id(1)
    nb = pl.num_programs(0)

    @pl.when(i < s_ref[nb + 1])
    def _():
        @pl.when(f == 0)
        def _():
            xb_ref[...] = x_ref[...].astype(BF16)
            acc_ref[...] = jnp.zeros(acc_ref.shape, F32)

        x = xb_ref[...]
        a = (_silu(_dotf(x, wg_ref[...])) * _dotf(x, wu_ref[...])).astype(BF16)
        acc_ref[...] += _dotf(a, wd_ref[...])

        @pl.when(f == pl.num_programs(1) - 1)
        def _():
            y_ref[...] = acc_ref[...]

    @pl.when(jnp.logical_and(i >= s_ref[nb + 1], f == pl.num_programs(1) - 1))
    def _():
        y_ref[...] = jnp.zeros(y_ref.shape, F32)


def moe_experts(sidx, xb, wg, wu, wd, blk, tf):
    n_rows, d = xb.shape
    nb = n_rows // blk
    nf = D_FF // tf

    def row_map(i, f, s):
        return (jnp.minimum(i, s[nb + 1] - 1), 0)

    def f_of(i, f, s):
        return jnp.where(i < s[nb + 1], f, nf - 1)

    grid_spec = pltpu.PrefetchScalarGridSpec(
        num_scalar_prefetch=1, grid=(nb, nf),
        in_specs=[pl.BlockSpec((blk, d), row_map),
                  pl.BlockSpec((None, None, d, tf), lambda i, f, s: (s[nb], s[i], 0, f_of(i, f, s))),
                  pl.BlockSpec((None, None, d, tf), lambda i, f, s: (s[nb], s[i], 0, f_of(i, f, s))),
                  pl.BlockSpec((None, None, tf, d), lambda i, f, s: (s[nb], s[i], f_of(i, f, s), 0))],
        out_specs=pl.BlockSpec((blk, d), lambda i, f, s: (i, 0)),
        scratch_shapes=[pltpu.VMEM((blk, d), BF16), pltpu.VMEM((blk, d), F32)])
    return pl.pallas_call(
        _expert_kernel,
        out_shape=jax.ShapeDtypeStruct((n_rows, d), F32),
        grid_spec=grid_spec,
        compiler_params=_cparams(("arbitrary", "arbitrary")),
        name="moe_experts",
    )(sidx, xb, wg, wu, wd)


def _combine_kernel(dest_ref, yb_ref, gate_ref, x1_ref, g2_ref, fg_ref, o_ref, buf_ref, sem, *, final):
    tm = x1_ref.shape[0]

    def start(r, carry):
        for k in range(TOP_K):
            _row_copy(yb_ref, dest_ref[0, TOP_K * r + k], buf_ref.at[k], r, sem).start()
        return carry

    lax.fori_loop(0, tm, start, 0, unroll=8)
    for k in range(TOP_K):
        pltpu.make_async_copy(yb_ref.at[pl.ds(0, tm), :], buf_ref.at[k], sem).wait()
    gate = gate_ref[...]
    f = gate[:, 0:1] * buf_ref[0] + gate[:, 1:2] * buf_ref[1]
    x2 = x1_ref[...] + g2_ref[...] * f
    o_ref[...] = _rms(x2) * fg_ref[...] if final else x2


def moe_combine(dest3, yb, gates, x1, mod, lidx_mod, final_g, tm, final):
    g, r, d = x1.shape
    per_row = mod.shape[1] > 1
    nt = r // tm
    del lidx_mod
    act = pl.BlockSpec((None, tm, d), lambda gi, i: (gi, i, 0))
    return pl.pallas_call(
        functools.partial(_combine_kernel, final=final),
        out_shape=jax.ShapeDtypeStruct((g, r, d), F32),
        grid=(g, nt),
        in_specs=[pl.BlockSpec((None, 1, TOP_K * tm), lambda gi, i: (gi * nt + i, 0, 0), memory_space=pltpu.SMEM),
                  pl.BlockSpec(memory_space=pl.ANY),
                  pl.BlockSpec((tm, 8), lambda gi, i: (gi * nt + i, 0)),
                  act,
                  (pl.BlockSpec((None, tm, d), lambda gi, i: (gi, i, 0)) if per_row
                   else pl.BlockSpec((None, 1, d), lambda gi, i: (gi, 0, 0))),
                  pl.BlockSpec((1, d), lambda gi, i: (0, 0))],
        out_specs=act,
        scratch_shapes=[pltpu.VMEM((TOP_K, tm, d), F32), pltpu.SemaphoreType.DMA(())],
        compiler_params=_cparams(("arbitrary", "arbitrary")),
        name="moe_combine",
    )(dest3, yb, gates, x1, mod, final_g)


def moe_ffn(layer_slot, h2, x1, mod_g2, router_pad, wg, wu, wd, final_g, tm, blk, tf, final):
    g, r, d = x1.shape
    t = g * r
    h2f = h2.reshape(t, d)
    iidx = jnp.array([layer_slot], I32)
    idx, gates, counts = moe_route(iidx, h2f, router_pad, _pick(t, 1024))
    counts = counts[0, :N_EXPERTS].astype(I32)
    padded = (counts + blk - 1) // blk * blk
    pad_end = jnp.cumsum(padded)
    pad_start = pad_end - padded
    first = sum(jnp.where(idx[:, 0:TOP_K] == e, pad_start[e], 0) for e in range(N_EXPERTS))
    dest = first + idx[:, TOP_K:2 * TOP_K]
    nb = -(-(t * TOP_K) // blk) + N_EXPERTS
    n_used = pad_end[-1] // blk
    blk_e = jnp.minimum(jnp.sum(jnp.arange(nb, dtype=I32)[:, None] * blk >= pad_end[None, :], axis=1), N_EXPERTS - 1)
    blk_e = jnp.where(jnp.arange(nb) < n_used, blk_e, blk_e[jnp.maximum(n_used - 1, 0)]).astype(I32)
    dest3 = dest.astype(I32).reshape(t // tm, 1, TOP_K * tm)
    fill = jnp.stack([jnp.append(pad_start + counts, pad_end[-1]),
                      jnp.append(pad_end, nb * blk)], axis=1).reshape(-1).astype(I32)
    xb = moe_dispatch(dest3, fill, h2f, nb * blk, tm)
    sidx = jnp.concatenate([blk_e, iidx, n_used.astype(I32)[None]])
    yb = moe_experts(sidx, xb, wg, wu, wd, blk, tf)
    return moe_combine(dest3, yb, gates, x1, mod_g2, None, final_g, tm, final)


def _attn_decode_kernel(pt_ref, lam_ref, lqk_ref, g_ref, q_ref, kn_ref, vn_ref, tab_ref, ck_ref, cv_ref, o_ref,
                        kbuf_ref, vbuf_ref, sem, m_ref, s_ref, acc_ref, *, pps):
    step = pl.program_id(1)
    n_steps = pl.num_programs(1)
    layer = pt_ref[pt_ref.shape[0] - 1]
    rows = 2 * H_A
    t = pl.program_id(0) * n_steps + step
    slot = lax.rem(t, 2)

    def start_pages(tt, sl):
        for i in range(pps):
            page = pt_ref[tt * pps + i]
            pltpu.make_async_copy(ck_ref.at[layer, page], kbuf_ref.at[sl, i], sem.at[0, sl]).start()
            pltpu.make_async_copy(cv_ref.at[layer, page], vbuf_ref.at[sl, i], sem.at[1, sl]).start()

    @pl.when(t == 0)
    def _():
        start_pages(0, 0)

    @pl.when(t + 1 < pl.num_programs(0) * n_steps)
    def _():
        start_pages(t + 1, 1 - slot)

    pltpu.make_async_copy(ck_ref.at[layer, pl.ds(0, pps)], kbuf_ref.at[slot], sem.at[0, slot]).wait()
    pltpu.make_async_copy(cv_ref.at[layer, pl.ds(0, pps)], vbuf_ref.at[slot], sem.at[1, slot]).wait()

    @pl.when(step == 0)
    def _():
        m_ref[...] = jnp.full(m_ref.shape, NEG, F32)
        s_ref[...] = jnp.zeros(s_ref.shape, F32)
        acc_ref[...] = jnp.zeros(acc_ref.shape, F32)

    qrow = jnp.broadcast_to(q_ref[...].astype(F32), (rows, W_A))
    own = (_iota((rows, W_A), 1) >> 6) == _iota((rows, W_A), 0)
    q8 = jnp.where(own, qrow, 0.0).astype(BF16)

    def accumulate(s, pv):
        m_old = m_ref[...]
        m_new = jnp.maximum(m_old, jnp.max(s, axis=1, keepdims=True))
        alpha = jnp.exp(m_old - m_new)
        p = jnp.exp(s - m_new)
        s_ref[...] = alpha * s_ref[...] + jnp.sum(p, axis=1, keepdims=True)
        acc_ref[...] = alpha * acc_ref[...] + pv(p)
        m_ref[...] = m_new

    scores = [_dotf(q8, kbuf_ref[slot, i].astype(BF16)) for i in range(pps)]
    scores[pps - 1] = jnp.where(step == n_steps - 1, scores[pps - 1] + tab_ref[0:rows, :], scores[pps - 1])
    s_all = jnp.concatenate(scores, axis=1)

    def value_page(i):
        heads = [vbuf_ref[slot, i, pl.ds(h, PAGE, stride=H_A), :] for h in range(H_A)]
        return jnp.concatenate(heads, axis=1).astype(BF16)

    def pv_pages(p):
        hi, lo = _split2(p)
        out = None
        for i in range(pps):
            v = value_page(i)
            part = _dotf(hi[:, i * PAGE:(i + 1) * PAGE], v) + _dotf(lo[:, i * PAGE:(i + 1) * PAGE], v)
            out = part if out is None else out + part
        return out

    accumulate(s_all, pv_pages)

    @pl.when(step == n_steps - 1)
    def _():
        kn = kn_ref[...].astype(BF16).astype(F32)
        vn = vn_ref[...].astype(BF16).astype(F32)
        s_new = jnp.sum(q8.astype(F32) * kn, axis=1, keepdims=True) + tab_ref[rows:2 * rows, 0:1]
        accumulate(s_new, lambda p: p * vn)
        o = acc_ref[...] * (1.0 / s_ref[...])
        lam = _lambda_value(lqk_ref[...], lam_ref[2 * layer])
        outs = []
        for h in range(H_A):
            blkh = o[:, h * DV_A:(h + 1) * DV_A]
            oh = blkh[2 * h:2 * h + 1, :] - lam * blkh[2 * h + 1:2 * h + 2, :]
            outs.append(_rms(oh) * g_ref[...] * lam_ref[2 * layer + 1])
        o_ref[...] = jnp.concatenate(outs, axis=1).astype(BF16)


def attention_decode(sidx, lam_consts, lambda_qk, subln_g3, q, k_new, v_new, tab, cache_k4, cache_v4, n_pages, pps):
    bd = q.shape[0]
    nl = sidx.shape[0] - 1
    assert n_pages % pps == 0
    row = pl.BlockSpec((None, 1, W_A), lambda b, s, pt: (b, 0, 0))
    grid_spec = pltpu.PrefetchScalarGridSpec(
        num_scalar_prefetch=1, grid=(bd, n_pages // pps),
        in_specs=[pl.BlockSpec(memory_space=pltpu.SMEM),
                  pl.BlockSpec((None, 4, DK_A), lambda b, s, pt: (pt[nl], 0, 0)),
                  pl.BlockSpec((None, 1, DV_A), lambda b, s, pt: (pt[nl], 0, 0)),
                  row, row, row,
                  pl.BlockSpec((16, PAGE), lambda b, s, pt: (0, 0)),
                  pl.BlockSpec(memory_space=pl.ANY), pl.BlockSpec(memory_space=pl.ANY)],
        out_specs=row,
        scratch_shapes=[pltpu.VMEM((2, pps, W_A, PAGE), F32), pltpu.VMEM((2, pps, H_A * PAGE, DV_A), F32),
                        pltpu.SemaphoreType.DMA((2, 2)),
                        pltpu.VMEM((2 * H_A, 1), F32), pltpu.VMEM((2 * H_A, 1), F32),
                        pltpu.VMEM((2 * H_A, W_A), F32)])
    return pl.pallas_call(
        functools.partial(_attn_decode_kernel, pps=pps),
        out_shape=jax.ShapeDtypeStruct((bd, 1, W_A), BF16),
        grid_spec=grid_spec,
        compiler_params=_cparams(("arbitrary", "arbitrary")),
        name="attention_decode",
    )(sidx, lam_consts, lambda_qk, subln_g3, q, k_new, v_new, tab, cache_k4, cache_v4)


def _bc_decode_pre_kernel(l_ref, rest_ref, buf_ref, convw_ref, convb_ref, v128_ref, wlr_ref,
                          xbc_ref, nbuf_ref, dt_ref, dec_ref, eg_ref):
    del l_ref
    u = rest_ref[:, R_BX:R_BX + XBC_B]
    conv = convb_ref[...] + convw_ref[CONV_W - 1:CONV_W, :] * u
    for j in range(CONV_W - 1):
        conv = conv + convw_ref[j:j + 1, :] * buf_ref[j]
    xbc_ref[...] = _silu(conv)
    for j in range(CONV_W - 2):
        nbuf_ref[j] = buf_ref[j + 1]
    nbuf_ref[CONV_W - 2] = u
    misc = rest_ref[:, R_MISC:R_MISC + 128]
    dt = _softplus(misc + v128_ref[0:1, :])
    dt_ref[...] = dt
    dec_ref[...] = jnp.exp(dt * (-jnp.exp(v128_ref[1:2, :])))
    gk = _dotf(misc.astype(BF16), wlr_ref[...]) + v128_ref[2:3, :]
    eg_ref[...] = jnp.exp(-_softplus(-gk) * (1.0 / GATE_NORM))


def bc_decode_pre(lidx, rest, conv_state_t, convw, convb3, vec128, wlr):
    bd = rest.shape[0]
    par = lambda r, w: pl.BlockSpec((None, r, w), lambda i, l: (l[0], 0, 0))
    full = lambda w: pl.BlockSpec((bd, w), lambda i, l: (0, 0))
    grid_spec = pltpu.PrefetchScalarGridSpec(
        num_scalar_prefetch=1, grid=(1,),
        in_specs=[full(W_REST),
                  pl.BlockSpec((None, CONV_W - 1, bd, XBC_B), lambda i, l: (l[0], 0, 0, 0)),
                  par(CONV_W, XBC_B), par(1, XBC_B), par(8, 128), par(128, 128)],
        out_specs=[full(XBC_B), pl.BlockSpec((CONV_W - 1, bd, XBC_B), lambda i, l: (0, 0, 0)),
                   full(128), full(128), full(128)])
    return pl.pallas_call(
        _bc_decode_pre_kernel,
        out_shape=[jax.ShapeDtypeStruct((bd, XBC_B), F32), jax.ShapeDtypeStruct((CONV_W - 1, bd, XBC_B), F32),
                   jax.ShapeDtypeStruct((bd, 128), F32), jax.ShapeDtypeStruct((bd, 128), F32),
                   jax.ShapeDtypeStruct((bd, 128), F32)],
        grid_spec=grid_spec,
        compiler_params=_cparams(("arbitrary",)),
        name="bc_decode_pre",
    )(lidx, rest, conv_state_t, convw, convb3, vec128, wlr)


def _bc_decode_state_kernel(l_ref, ssm_ref, gla_ref, x4_ref, b4_ref, c4_ref, dt4_ref, dec4_ref, bz4_ref,
                            dsk_ref, ng_ref, q4_ref, k4_ref, v4_ref, eg4_ref, cg4_ref, gng_ref,
                            ssm_out_ref, gla_out_ref, ob4_ref, oc4_ref):
    del l_ref
    x4 = x4_ref[...]
    s = ssm_ref[...] * dec4_ref[...] + (dt4_ref[...] * x4) * b4_ref[...]
    ssm_out_ref[...] = s
    y = jnp.sum(c4_ref[...] * s, axis=3, keepdims=True) + dsk_ref[...] * x4
    yg = y * _silu(bz4_ref[...])
    ms = jnp.sum(jnp.sum(yg * yg, axis=2, keepdims=True), axis=1, keepdims=True) * (1.0 / W_B)
    ob4_ref[...] = yg * lax.rsqrt(ms + EPS) * ng_ref[...]
    sg = gla_ref[...] * eg4_ref[...] + k4_ref[...] * v4_ref[...]
    gla_out_ref[...] = sg
    o = jnp.sum(q4_ref[...] * sg, axis=2, keepdims=True)
    oc4_ref[...] = _rms(o) * gng_ref[...] * _silu(cg4_ref[...])


def bc_decode_state(lidx, state_ssm, state_gla, x4, b4, c4, dt4, dec4, bz4, dsk4, ng4, q4, k4, v4, eg4, cg4, gng4):
    bd = x4.shape[0]

    def full(shape):
        n = len(shape)
        return pl.BlockSpec(shape, lambda i, l: (0,) * n)

    def layer(shape):
        n = len(shape)
        return pl.BlockSpec((None,) + shape, lambda i, l: (l[0],) + (0,) * n)

    s_ssm = (bd, H_B, P_B, N_B)
    s_gla = (bd, H_C, DK_C, DV_C)
    col_b = (bd, H_B, P_B, 1)
    row_b = (bd, H_B, 1, N_B)
    one_b = (bd, H_B, 1, 1)
    col_c = (bd, H_C, DK_C, 1)
    row_c = (bd, H_C, 1, DV_C)
    grid_spec = pltpu.PrefetchScalarGridSpec(
        num_scalar_prefetch=1, grid=(1,),
        in_specs=[layer(s_ssm), layer(s_gla), full(col_b), full(row_b), full(row_b), full(one_b), full(one_b),
                  full(col_b), layer((1, H_B, 1, 1)), layer((1, H_B, P_B, 1)),
                  full(col_c), full(col_c), full(row_c), full(col_c), full(row_c), layer((1, 1, 1, DV_C))],
        out_specs=[full(s_ssm), full(s_gla), full(col_b), full(row_c)])
    return pl.pallas_call(
        _bc_decode_state_kernel,
        out_shape=[jax.ShapeDtypeStruct(s_ssm, F32), jax.ShapeDtypeStruct(s_gla, F32),
                   jax.ShapeDtypeStruct(col_b, F32), jax.ShapeDtypeStruct(row_c, F32)],
        grid_spec=grid_spec,
        compiler_params=_cparams(("arbitrary",)),
        name="bc_decode_state",
    )(lidx, state_ssm, state_gla, x4, b4, c4, dt4, dec4, bz4, dsk4, ng4, q4, k4, v4, eg4, cg4, gng4)


def _pack_params(w_in, rel_bias, subln_g, conv_b, dt_bias, a_log, d_skip, ssd_norm_g, gla_w_lr, gla_b_lr,
                 gla_norm_g, router_w):
    depth = w_in.shape[0]
    pad = jnp.zeros(w_in.shape[:2] + (D_PACK - D_IN,), w_in.dtype)
    w_pack = jnp.concatenate([w_in[..., :OFF_BDT], w_in[..., OFF_CQ:OFF_CLR], w_in[..., OFF_BDT:OFF_CQ],
                              w_in[..., OFF_CLR:], pad], axis=-1).astype(BF16)
    vec128 = jnp.zeros((depth, 8, 128), F32)
    vec128 = vec128.at[:, 0, :H_B].set(dt_bias).at[:, 1, :H_B].set(a_log).at[:, 2, :].set(gla_b_lr)
    vec256 = jnp.zeros((depth, 8, 256), F32)
    vec256 = (vec256.at[:, 0, :].set(jnp.repeat(d_skip, P_B, axis=1)).at[:, 1, :].set(ssd_norm_g)
              .at[:, 2, :].set(jnp.tile(gla_norm_g, (1, H_C))))
    wlr = jnp.zeros((depth, 128, 128), F32).at[:, MISC_LR:MISC_LR + GATE_RANK, :].set(gla_w_lr).astype(BF16)
    router_pad = jnp.zeros(router_w.shape[:2] + (128,), F32).at[..., :N_EXPERTS].set(router_w)
    lam_init = [0.8 - 0.6 * math.exp(-0.3 * l) for l in range(depth)]
    lam_consts = jnp.asarray(np.array([[li, 1.0 - li] for li in lam_init], np.float32).reshape(-1))
    return dict(w_pack=w_pack, vec128=vec128, vec256=vec256, wlr=wlr, router_pad=router_pad,
                lam_consts=lam_consts, subln_g3=subln_g.reshape(depth, 1, DV_A),
                convb3=conv_b.reshape(depth, 1, XBC_B))


def _pick(n, pref):
    if n <= pref:
        return n
    t = pref
    while n % t:
        t //= 2
    return t


def _channel_mixer(l, depth, h2, x1, mod, pk, wts, tm, moe_blk, final):
    i = l // 2
    lmod = jnp.array([l], I32)
    if l % 2 == 0:
        return dense_ffn(jnp.array([i], I32), lmod, h2, wts['ffn_g'], wts['ffn_u'], wts['ffn_d'], x1, mod,
                         wts['final_g'], tm, FF_TILE, final)
    mod_g2 = mod[l, :, :, 5 * D_MODEL:6 * D_MODEL]
    return moe_ffn(i, h2, x1, mod_g2, pk['router_pad'], wts['moe_g'], wts['moe_u'], wts['moe_d'], wts['final_g'],
                   _pick(x1.shape[0] * x1.shape[1], 256), moe_blk, FF_TILE, final)


def _run_prompt(x, mod, pk, wts, lambda_qk, conv_w, rel_bias):
    b, seq, d = x.shape
    depth = mod.shape[0]
    tm = _pick(seq // 2, 512)
    bias_tiles = attention_bias_tiles(rel_bias, tm)
    k_all = jnp.zeros((depth, b, W_A, seq), F32)
    v_all = jnp.zeros((depth, b, H_A * seq, DV_A), F32)
    convs, ssms, glas = [], [], []
    for l in range(depth):
        lidx = jnp.array([l], I32)
        q, rest, kt, vb, k_all, v_all = in_projection_prompt(lidx, x, mod, pk['w_pack'], k_all, v_all, tm, 2 * tm)
        oa = attention_prompt(lidx, pk['lam_consts'], lambda_qk, pk['subln_g3'], q, kt, vb, bias_tiles, tm)
        ob, oc, conv_s, ssm_s, gla_s = bc_mixers_prompt(lidx, rest, conv_w, pk['convb3'], pk['vec128'],
                                                        pk['vec256'], pk['wlr'])
        moe = l % 2 == 1
        x1, h2 = out_projection(lidx, oa, ob, oc, wts['w_out'], x, mod, tm, F32 if moe else BF16)
        x = _channel_mixer(l, depth, h2, x1, mod, pk, wts, tm, _pick(b * seq, 512), l == depth - 1)
        convs.append(conv_s)
        ssms.append(_unpack_ssm_state(ssm_s))
        glas.append(_unpack_gla_state(gla_s))
    ks = jnp.transpose(k_all.reshape(depth, b, 2 * H_A, DK_A, seq), (0, 1, 4, 2, 3))
    vs = v_all.reshape(depth, b, seq, H_A, DV_A)
    return x, ks, vs, jnp.stack(convs), jnp.stack(ssms), jnp.stack(glas)


def _run_decode(x, mod, pk, wts, lambda_qk, conv_w, rel_bias, cache_k, cache_v, page_table,
                state_conv, state_ssm, state_gla, d_skip, ssd_norm_g, gla_norm_g):
    bd = x.shape[0]
    depth = mod.shape[0]
    n_pages = page_table.shape[1]
    past = n_pages * PAGE
    pps = _pick(n_pages, 16)
    xg = x.reshape(1, bd, D_MODEL)
    pool = cache_k.shape[1]
    assert PAGE >= T5_MAX_DIST
    cache_k4 = jnp.transpose(cache_k, (0, 1, 3, 4, 2)).reshape(depth, pool, 2 * H_A * DK_A, PAGE)
    cache_v4 = cache_v.reshape(depth, pool, PAGE * H_A, DV_A)
    table = _t5_table(rel_bias, PAGE + 1)
    dist = past - ((n_pages - 1) * PAGE + jnp.arange(PAGE))
    tab = jnp.concatenate([jnp.repeat(table[:, dist], 2, axis=0),
                           jnp.broadcast_to(jnp.repeat(table[:, 0], 2)[:, None], (2 * H_A, PAGE))], axis=0)
    conv_t = jnp.swapaxes(state_conv, 1, 2)
    grp = np.arange(H_B) // (H_B // G_B)
    dsk4 = d_skip.reshape(depth, 1, H_B, 1, 1)
    ng4 = ssd_norm_g.reshape(depth, 1, H_B, P_B, 1)
    gng4 = gla_norm_g.reshape(depth, 1, 1, 1, DV_C)
    pt_flat = page_table.reshape(-1).astype(I32)
    ks, vs, convs, ssms, glas = [], [], [], [], []
    for l in range(depth):
        lidx = jnp.array([l], I32)
        q, k, v, rest = in_projection(lidx, xg, mod, pk['w_pack'], bd)
        sidx = jnp.concatenate([pt_flat, lidx])
        oa = attention_decode(sidx, pk['lam_consts'], lambda_qk, pk['subln_g3'], q.reshape(bd, 1, W_A),
                              k.reshape(bd, 1, W_A), v.reshape(bd, 1, W_A), tab, cache_k4, cache_v4, n_pages, pps)
        rest2 = rest.reshape(bd, W_REST)
        xbc, nbuf, dt, dec, eg = bc_decode_pre(lidx, rest2, conv_t, conv_w, pk['convb3'], pk['vec128'], pk['wlr'])
        x4 = xbc[:, :W_B].reshape(bd, H_B, P_B, 1)
        b4 = xbc[:, W_B:W_B + G_B * N_B].reshape(bd, G_B, 1, N_B)[:, grp]
        c4 = xbc[:, W_B + G_B * N_B:].reshape(bd, G_B, 1, N_B)[:, grp]
        dt4 = dt[:, :H_B].reshape(bd, H_B, 1, 1)
        dec4 = dec[:, :H_B].reshape(bd, H_B, 1, 1)
        bz4 = rest2[:, R_BZ:R_BZ + W_B].reshape(bd, H_B, P_B, 1)
        q4 = (rest2[:, R_CQ:R_CQ + 128] * (DK_C ** -0.5)).reshape(bd, H_C, DK_C, 1)
        k4 = rest2[:, R_CK:R_CK + 128].reshape(bd, H_C, DK_C, 1)
        v4 = rest2[:, R_CV:R_CV + W_C].reshape(bd, H_C, 1, DV_C)
        eg4 = eg.reshape(bd, H_C, DK_C, 1)
        cg4 = rest2[:, R_CG:R_CG + W_C].reshape(bd, H_C, 1, DV_C)
        ssm_s, gla_s, ob4, oc4 = bc_decode_state(lidx, state_ssm, state_gla, x4, b4, c4, dt4, dec4, bz4, dsk4, ng4,
                                                 q4, k4, v4, eg4, cg4, gng4)
        ob = ob4.reshape(1, bd, W_B).astype(BF16)
        oc = oc4.reshape(1, bd, W_C).astype(BF16)
        moe = l % 2 == 1
        x1, h2 = out_projection(lidx, oa.reshape(1, bd, W_A), ob, oc, wts['w_out'], xg, mod, bd, F32 if moe else BF16)
        xg = _channel_mixer(l, depth, h2, x1, mod, pk, wts, bd, 64, l == depth - 1)
        ks.append(k.reshape(bd, 1, 2 * H_A, DK_A))
        vs.append(v.reshape(bd, 1, H_A, DV_A))
        convs.append(jnp.swapaxes(nbuf, 0, 1))
        ssms.append(ssm_s)
        glas.append(gla_s)
    return (xg.reshape(bd, 1, D_MODEL), jnp.stack(ks), jnp.stack(vs), jnp.stack(convs), jnp.stack(ssms),
            jnp.stack(glas))


def kernel(x_prompt, x_sample, c_prompt, c_sample, cache_k, cache_v, page_table, state_conv, state_ssm, state_gla,
           w_ada, b_ada, w_in, w_out, rel_bias, lambda_qk, subln_g, conv_w, conv_b, dt_bias, a_log, d_skip,
           ssd_norm_g, gla_w_lr, gla_b_lr, gla_norm_g, ffn_w_gate, ffn_w_up, ffn_w_down, router_w,
           moe_w_gate, moe_w_up, moe_w_down, final_norm_g):
    depth = w_in.shape[0]
    bp, bd = c_prompt.shape[0], c_sample.shape[0]
    pk = _pack_params(w_in, rel_bias, subln_g, conv_b, dt_bias, a_log, d_skip, ssd_norm_g, gla_w_lr, gla_b_lr,
                      gla_norm_g, router_w)
    wts = dict(w_out=w_out.astype(BF16), ffn_g=ffn_w_gate.astype(BF16), ffn_u=ffn_w_up.astype(BF16),
               ffn_d=ffn_w_down.astype(BF16), moe_g=moe_w_gate.astype(BF16), moe_u=moe_w_up.astype(BF16),
               moe_d=moe_w_down.astype(BF16), final_g=final_norm_g.reshape(1, D_MODEL))
    n_c = bp + bd
    n_cp = -(-n_c // 8) * 8
    c_all = jnp.concatenate([c_prompt, c_sample, jnp.zeros((n_cp - n_c, D_MODEL), F32)], axis=0)
    mod = ada_modulation(c_all, w_ada, b_ada)
    mod_p = mod[:, :bp].reshape(depth, bp, 1, 6 * D_MODEL)
    mod_d = mod[:, bp:n_c].reshape(depth, 1, bd, 6 * D_MODEL)
    yp, kp, vp, convp, ssmp, glap = _run_prompt(x_prompt, mod_p, pk, wts, lambda_qk, conv_w, rel_bias)
    yd, kd, vd, convd, ssmd, glad = _run_decode(x_sample, mod_d, pk, wts, lambda_qk, conv_w, rel_bias, cache_k,
                                                cache_v, page_table, state_conv, state_ssm, state_gla, d_skip,
                                                ssd_norm_g, gla_norm_g)
    return (yp, yd, kp, vp, convp, ssmp, glap, kd, vd, convd, ssmd, glad)
```

```python
import functools
import math

import numpy as np
import jax
import jax.numpy as jnp
from jax import lax
from jax.experimental import pallas as pl
from jax.experimental.pallas import tpu as pltpu

F32 = jnp.float32
BF16 = jnp.bfloat16
I32 = jnp.int32

D_MODEL = 1024
H_A, DK_A, DV_A = 4, 64, 128
W_A = H_A * DV_A
H_B, P_B, G_B, N_B, CONV_W = 4, 64, 2, 64, 4
W_B = H_B * P_B
XBC_B = W_B + 2 * G_B * N_B
H_C, DK_C, DV_C = 4, 32, 64
W_C = H_C * DV_C
GATE_RANK, GATE_NORM = 16, 16.0
T5_BUCKETS, T5_MAX_DIST = 32, 128
D_FF, N_EXPERTS, TOP_K = 2816, 8, 2
EPS = 1e-6
PAGE = 128
CHUNK = 128
FF_TILE = D_FF
OFF_AQ = 0
OFF_AK = OFF_AQ + 2 * H_A * DK_A
OFF_AV = OFF_AK + 2 * H_A * DK_A
OFF_BZ = OFF_AV + W_A
OFF_BX = OFF_BZ + W_B
OFF_BDT = OFF_BX + XBC_B
OFF_CQ = OFF_BDT + H_B
OFF_CK = OFF_CQ + H_C * DK_C
OFF_CV = OFF_CK + H_C * DK_C
OFF_CG = OFF_CV + W_C
OFF_CLR = OFF_CG + W_C
D_IN = OFF_CLR + GATE_RANK
PK_REST = 3 * W_A
R_BZ, R_BX, R_CQ, R_CK, R_CV, R_CG, R_MISC = 0, 256, 768, 896, 1024, 1280, 1536
W_REST = R_MISC + 128
D_PACK = PK_REST + W_REST
MISC_DT, MISC_LR = 0, H_B
NEG = -1e30
VMEM_LIMIT = 56 * 1024 * 1024


def _cparams(sem):
    return pltpu.CompilerParams(dimension_semantics=sem, vmem_limit_bytes=VMEM_LIMIT)


def _rms(x):
    return x * lax.rsqrt(jnp.mean(x * x, axis=-1, keepdims=True) + EPS)


def _silu(x):
    return x * jax.nn.sigmoid(x)


def _softplus(x):
    return jnp.maximum(x, 0.0) + jnp.log1p(jnp.exp(-jnp.abs(x)))


def _split2(x):
    hi = x.astype(BF16)
    return hi, (x - hi.astype(F32)).astype(BF16)


def _split3(x):
    hi = x.astype(BF16)
    r = x - hi.astype(F32)
    mid = r.astype(BF16)
    return hi, mid, (r - mid.astype(F32)).astype(BF16)


def _dotf(a, b):
    return jnp.dot(a, b, preferred_element_type=F32)


def _lmat_exact(mat01, x):
    hi, mid, lo = _split3(x)
    return _dotf(mat01, hi) + _dotf(mat01, mid) + _dotf(mat01, lo)


def _rmat_exact(x, mat01, pieces=3):
    if pieces == 2:
        hi, lo = _split2(x)
        return _dotf(hi, mat01) + _dotf(lo, mat01)
    hi, mid, lo = _split3(x)
    return _dotf(hi, mat01) + _dotf(mid, mat01) + _dotf(lo, mat01)


def _iota(shape, axis):
    return lax.broadcasted_iota(I32, shape, axis)


def _ada_kernel(c_ref, w_ref, b_ref, o_ref):
    c = c_ref[...]
    o_ref[...] = _dotf(_silu(c).astype(BF16), w_ref[...].astype(BF16)) + b_ref[...]


def ada_modulation(c_all, w_ada, b_ada):
    depth, d, n = w_ada.shape
    bc = c_all.shape[0]
    tn = 1536 if n % 1536 == 0 else n
    return pl.pallas_call(
        _ada_kernel,
        out_shape=jax.ShapeDtypeStruct((depth, bc, n), F32),
        grid=(depth, n // tn),
        in_specs=[pl.BlockSpec((bc, d), lambda l, j: (0, 0)),
                  pl.BlockSpec((None, d, tn), lambda l, j: (l, 0, j)),
                  pl.BlockSpec((None, 1, tn), lambda l, j: (l, 0, j))],
        out_specs=pl.BlockSpec((None, bc, tn), lambda l, j: (l, 0, j)),
        compiler_params=_cparams(("arbitrary", "arbitrary")),
        name="ada_modulation",
    )(c_all, w_ada, b_ada.reshape(depth, 1, n))


def _mod_spec(tm, per_row, chunk):
    if per_row:
        return pl.BlockSpec((None, None, tm, D_MODEL), lambda g, i, l: (l[0], g, i, chunk))
    return pl.BlockSpec((None, None, 1, D_MODEL), lambda g, i, l: (l[0], g, 0, chunk))


def _in_proj_kernel(l_ref, x_ref, sh_ref, sc_ref, w_ref, q_ref, k_ref, v_ref, r_ref):
    del l_ref
    h = (_rms(x_ref[...]) * (1.0 + sc_ref[...]) + sh_ref[...]).astype(BF16)
    q_ref[...] = (_dotf(h, w_ref[:, 0:W_A]) * (DK_A ** -0.5)).astype(BF16)
    k_ref[...] = _dotf(h, w_ref[:, W_A:2 * W_A])
    v_ref[...] = _dotf(h, w_ref[:, 2 * W_A:3 * W_A])
    r_ref[...] = _dotf(h, w_ref[:, PK_REST:D_PACK])


def in_projection(lidx, x, mod, w_pack, tm):
    g, r, d = x.shape
    per_row = mod.shape[2] > 1
    row = lambda w, dt: jax.ShapeDtypeStruct((g, r, w), dt)
    ospec = lambda w: pl.BlockSpec((None, tm, w), lambda gi, i, l: (gi, i, 0))
    out_specs = [ospec(W_A), ospec(W_A), ospec(W_A), ospec(W_REST)]
    out_shape = [row(W_A, BF16), row(W_A, F32), row(W_A, F32), row(W_REST, F32)]
    grid_spec = pltpu.PrefetchScalarGridSpec(
        num_scalar_prefetch=1, grid=(g, r // tm),
        in_specs=[pl.BlockSpec((None, tm, d), lambda gi, i, l: (gi, i, 0)),
                  _mod_spec(tm, per_row, 0), _mod_spec(tm, per_row, 1),
                  pl.BlockSpec((None, d, D_PACK), lambda gi, i, l: (l[0], 0, 0))],
        out_specs=out_specs)
    return pl.pallas_call(
        _in_proj_kernel,
        out_shape=out_shape,
        grid_spec=grid_spec,
        compiler_params=_cparams(("arbitrary", "arbitrary")),
        name="in_projection",
    )(lidx, x, mod, mod, w_pack)


def _in_proj_prompt_kernel(l_ref, x_ref, sh_ref, sc_ref, w_ref, kall_in_ref, vall_in_ref,
                           q_ref, r_ref, kt_ref, vb_ref, kall_ref, vall_ref, k_scr):
    del l_ref, kall_in_ref, vall_in_ref
    tm = x_ref.shape[0]
    h = (_rms(x_ref[...]) * (1.0 + sc_ref[...]) + sh_ref[...]).astype(BF16)
    q_ref[...] = (_dotf(h, w_ref[:, 0:W_A]) * (DK_A ** -0.5)).astype(BF16)
    k_scr[...] = _dotf(h, w_ref[:, W_A:2 * W_A])
    kt = k_scr[...].T
    kall_ref[...] = kt
    kt_ref[...] = kt.astype(BF16)
    v = _dotf(h, w_ref[:, 2 * W_A:3 * W_A])
    vb_ref[...] = v.astype(BF16)
    for head in range(H_A):
        vall_ref[pl.ds(head, tm, stride=H_A), :] = v[:, head * DV_A:(head + 1) * DV_A]
    r_ref[...] = _dotf(h, w_ref[:, PK_REST:D_PACK])


def in_projection_prompt(lidx, x, mod, w_pack, k_all, v_all, tm, kblk):
    b, seq, d = x.shape
    sub = kblk // tm
    act = lambda w: pl.BlockSpec((None, tm, w), lambda bi, i, l: (bi, i, 0))
    grid_spec = pltpu.PrefetchScalarGridSpec(
        num_scalar_prefetch=1, grid=(b, seq // tm),
        in_specs=[act(d), _mod_spec(tm, False, 0), _mod_spec(tm, False, 1),
                  pl.BlockSpec((None, d, D_PACK), lambda bi, i, l: (l[0], 0, 0)),
                  pl.BlockSpec(memory_space=pl.ANY), pl.BlockSpec(memory_space=pl.ANY)],
        out_specs=[act(W_A), act(W_REST),
                   pl.BlockSpec((None, None, W_A, tm), lambda bi, i, l: (bi, i // sub, 0, i % sub)),
                   act(W_A),
                   pl.BlockSpec((None, None, W_A, tm), lambda bi, i, l: (l[0], bi, 0, i)),
                   pl.BlockSpec((None, None, H_A * tm, DV_A), lambda bi, i, l: (l[0], bi, i, 0))],
        scratch_shapes=[pltpu.VMEM((tm, W_A), F32)])
    return pl.pallas_call(
        _in_proj_prompt_kernel,
        out_shape=[jax.ShapeDtypeStruct((b, seq, W_A), BF16), jax.ShapeDtypeStruct((b, seq, W_REST), F32),
                   jax.ShapeDtypeStruct((b, seq // kblk, W_A, kblk), BF16), jax.ShapeDtypeStruct((b, seq, W_A), BF16),
                   jax.ShapeDtypeStruct(k_all.shape, F32), jax.ShapeDtypeStruct(v_all.shape, F32)],
        grid_spec=grid_spec,
        input_output_aliases={5: 4, 6: 5},
        compiler_params=_cparams(("arbitrary", "arbitrary")),
        name="in_projection_prompt",
    )(lidx, x, mod, mod, w_pack, k_all, v_all)


def _lambda_value(lqk, lam_init):
    s01 = jnp.sum(lqk[0:1, :] * lqk[1:2, :], axis=1, keepdims=True)
    s23 = jnp.sum(lqk[2:3, :] * lqk[3:4, :], axis=1, keepdims=True)
    return jnp.exp(s01) - jnp.exp(s23) + lam_init


def _attn_prompt_kernel(l_ref, lam_ref, lqk_ref, g_ref, q_ref, kt_ref, v_ref, bias_ref, o_ref,
                        vaug_ref, m_ref, acc_ref, *, blk):
    qi = pl.program_id(2)

    @pl.when(qi == 0)
    def _():
        vaug_ref[:, 0:DV_A] = v_ref[...]
        ones_col = _iota((vaug_ref.shape[0], DV_A), 1) == 0
        vaug_ref[:, DV_A:2 * DV_A] = jnp.where(ones_col, 1.0, 0.0).astype(BF16)

    q = q_ref[...]
    lane = _iota(q.shape, 1)
    zero = jnp.zeros_like(q)
    qm = (jnp.where(lane < DK_A, q, zero), jnp.where(lane >= DK_A, q, zero))
    m_ref[...] = jnp.full(m_ref.shape, NEG, F32)
    acc_ref[...] = jnp.zeros(acc_ref.shape, F32)

    def update(kt, va, bias):
        for m in range(2):
            s = _dotf(qm[m], kt)
            if bias is not None:
                pieces = [s[:, i * blk:(i + 1) * blk] if b is None else s[:, i * blk:(i + 1) * blk] + b
                          for i, b in enumerate(bias)]
                s = pieces[0] if len(pieces) == 1 else jnp.concatenate(pieces, axis=1)
            m_old = m_ref[m]
            m_new = jnp.maximum(m_old, jnp.broadcast_to(jnp.max(s, axis=1, keepdims=True), m_old.shape))
            p = jnp.exp(s - jnp.tile(m_new, (1, s.shape[1] // 128))).astype(BF16)
            acc_ref[m] = jnp.tile(jnp.exp(m_old - m_new), (1, 2)) * acc_ref[m] + _dotf(p, va)
            m_ref[m] = m_new

    def key_block(j):
        return kt_ref[j], vaug_ref[pl.ds(pl.multiple_of(j * 2 * blk, 2 * blk), 2 * blk), :]

    def far(j, carry):
        update(*key_block(j), None)
        return carry

    a = qi >> 1
    odd = (qi & 1) == 1
    lax.fori_loop(0, jnp.where(odd, a, jnp.maximum(a - 1, 0)), far, 0)

    @pl.when(odd)
    def _():
        update(*key_block(a), (bias_ref[1], bias_ref[0]))

    @pl.when(jnp.logical_and(jnp.logical_not(odd), a >= 1))
    def _():
        update(*key_block(a - 1), (None, bias_ref[1]))

    @pl.when(jnp.logical_not(odd))
    def _():
        start = pl.multiple_of(a * 2 * blk, 2 * blk)
        update(kt_ref[a, :, 0:blk], vaug_ref[pl.ds(start, blk), :], (bias_ref[0],))

    layer = l_ref[0]
    lam = _lambda_value(lqk_ref[...], lam_ref[2 * layer])
    acc0 = acc_ref[0]
    acc1 = acc_ref[1]
    o = (acc0[:, 0:DV_A] * (1.0 / acc0[:, DV_A:DV_A + 1])
         - lam * (acc1[:, 0:DV_A] * (1.0 / acc1[:, DV_A:DV_A + 1])))
    o_ref[...] = (_rms(o) * g_ref[...] * lam_ref[2 * layer + 1]).astype(BF16)


def _t5_table(rel_bias, n):
    dist = jnp.arange(n)
    max_exact = T5_BUCKETS // 2
    nf = jnp.maximum(dist, 1).astype(F32)
    large = max_exact + (jnp.log(nf / max_exact) / math.log(T5_MAX_DIST / max_exact)
                         * (T5_BUCKETS - max_exact)).astype(I32)
    bucket = jnp.where(dist < max_exact, dist, jnp.minimum(large, T5_BUCKETS - 1))
    return (rel_bias[bucket] - rel_bias[T5_BUCKETS - 1][None, :]).T.astype(F32)


def _bias_tiles_kernel(rel_ref, o_ref):
    h = pl.program_id(0)
    blk = o_ref.shape[0]
    d = _iota((blk, blk), 0) - _iota((blk, blk), 1) + pl.program_id(1) * blk
    dist = jnp.maximum(d, 0)
    max_exact = T5_BUCKETS // 2
    nf = jnp.maximum(dist, 1).astype(F32)
    large = max_exact + (jnp.log(nf / max_exact) / math.log(T5_MAX_DIST / max_exact)
                         * (T5_BUCKETS - max_exact)).astype(I32)
    bucket = jnp.where(dist < max_exact, dist, jnp.minimum(large, T5_BUCKETS - 1))
    sat = rel_ref[(T5_BUCKETS - 1) * H_A + h]
    val = jnp.zeros((blk, blk), F32)
    for b in range(T5_BUCKETS - 1):
        val = jnp.where(bucket == b, rel_ref[b * H_A + h] - sat, val)
    o_ref[...] = jnp.where(d >= 0, val, NEG)


def attention_bias_tiles(rel_bias, blk):
    return pl.pallas_call(
        _bias_tiles_kernel,
        out_shape=jax.ShapeDtypeStruct((H_A, 2, blk, blk), F32),
        grid=(H_A, 2),
        in_specs=[pl.BlockSpec(memory_space=pltpu.SMEM)],
        out_specs=pl.BlockSpec((None, None, blk, blk), lambda h, t: (h, t, 0, 0)),
        compiler_params=_cparams(("arbitrary", "arbitrary")),
        name="attention_bias_tiles",
    )(rel_bias.reshape(-1))


def attention_prompt(lidx, lam_consts, lambda_qk, subln_g3, q, kt, vb, bias_tiles, blk):
    b, seq, _ = q.shape
    assert blk >= T5_MAX_DIST and blk % 128 == 0 and kt.shape == (b, seq // (2 * blk), W_A, 2 * blk)
    grid_spec = pltpu.PrefetchScalarGridSpec(
        num_scalar_prefetch=1, grid=(b, H_A, seq // blk),
        in_specs=[pl.BlockSpec(memory_space=pltpu.SMEM),
                  pl.BlockSpec((None, 4, DK_A), lambda bi, h, i, l: (l[0], 0, 0)),
                  pl.BlockSpec((None, 1, DV_A), lambda bi, h, i, l: (l[0], 0, 0)),
                  pl.BlockSpec((None, blk, DV_A), lambda bi, h, i, l: (bi, i, h)),
                  pl.BlockSpec((None, seq // (2 * blk), DV_A, 2 * blk), lambda bi, h, i, l: (bi, 0, h, 0)),
                  pl.BlockSpec((None, seq, DV_A), lambda bi, h, i, l: (bi, 0, h)),
                  pl.BlockSpec((None, 2, blk, blk), lambda bi, h, i, l: (h, 0, 0, 0))],
        out_specs=pl.BlockSpec((None, blk, DV_A), lambda bi, h, i, l: (bi, i, h)),
        scratch_shapes=[pltpu.VMEM((seq, 2 * DV_A), BF16), pltpu.VMEM((2, blk, 128), F32),
                        pltpu.VMEM((2, blk, 2 * DV_A), F32)])
    return pl.pallas_call(
        functools.partial(_attn_prompt_kernel, blk=blk),
        out_shape=jax.ShapeDtypeStruct((b, seq, W_A), BF16),
        grid_spec=grid_spec,
        compiler_params=_cparams(("arbitrary", "arbitrary", "arbitrary")),
        name="attention_prompt",
    )(lidx, lam_consts, lambda_qk, subln_g3, q, kt, vb, bias_tiles)


def _bc_prompt_kernel(l_ref, rest_ref, convw_ref, convb_ref, v128_ref, v256_ref, wlr_ref,
                      ob_ref, oc_ref, conv_out_ref, ssm_out_ref, gla_out_ref,
                      xpad_ref, sbd_ref, sg_ref, b_ref, oi_ref, *, nsub):
    del l_ref
    c = pl.program_id(1)
    q = CHUNK
    rows = nsub * q

    @pl.when(c == 0)
    def _():
        xpad_ref[0:8, :] = jnp.zeros((8, XBC_B), F32)
        sbd_ref[...] = jnp.zeros(sbd_ref.shape, F32)
        sg_ref[...] = jnp.zeros(sg_ref.shape, F32)

    xpad_ref[8:8 + rows, :] = rest_ref[:, R_BX:R_BX + XBC_B]
    conv = convb_ref[...]
    for j in range(CONV_W):
        conv = conv + convw_ref[j:j + 1, :] * xpad_ref[5 + j:5 + j + rows, :]
    xbc_all = _silu(conv)
    tail = xpad_ref[5 + rows:8 + rows, :]
    conv_out_ref[...] = tail
    xpad_ref[5:8, :] = tail

    row = _iota((q, q), 0)
    col = _iota((q, q), 1)
    causal = row >= col
    tril = jnp.where(causal, 1.0, 0.0).astype(BF16)
    lane1 = _iota((1, 128), 1)
    is_head = lane1 < H_B
    expand = jnp.where(_iota((128, W_B), 0) == (_iota((128, W_B), 1) >> 6), 1.0, 0.0).astype(BF16)
    grp_lane = _iota((q, 128), 1) >> 6
    head_lane = _iota((q, W_B), 1) >> 6
    same_grp = (_iota((128, W_B), 0) >> 6) == (_iota((128, W_B), 1) >> 7)
    same_head = (_iota((128, W_C), 0) >> 5) == (_iota((128, W_C), 1) >> 6)
    vhead = _iota((q, W_C), 1) >> 6
    a_neg = jnp.where(is_head, -jnp.exp(v128_ref[1:2, :]), 0.0)

    state = sbd_ref[...]
    gla = []
    for u in range(nsub):
        r0 = u * q
        xbc = xbc_all[r0:r0 + q, :]
        misc = rest_ref[r0:r0 + q, R_MISC:R_MISC + 128]
        xs = xbc[:, 0:W_B]
        bmat = xbc[:, W_B:W_B + G_B * N_B].astype(BF16)
        cmat = xbc[:, W_B + G_B * N_B:XBC_B].astype(BF16)
        dt = jnp.where(is_head, _softplus(misc + v128_ref[0:1, :]), 0.0)
        cum = _lmat_exact(tril, dt * a_neg)
        cum_t = cum.T
        ecum = jnp.exp(cum)
        wgt = jnp.exp(cum[q - 1:q, :] - cum) * dt
        dt_e = _rmat_exact(dt, expand, 2)
        ecum_e = _rmat_exact(ecum, expand, 2)
        wgt_e = _rmat_exact(wgt, expand, 2)
        zero_b = jnp.zeros_like(cmat)
        gmat = [lax.dot_general(jnp.where(grp_lane == g, cmat, zero_b), bmat, (((1,), (1,)), ((), ())),
                                preferred_element_type=F32) for g in range(G_B)]
        mcat = []
        for h in range(H_B):
            diff = cum[:, h:h + 1] - cum_t[h:h + 1, :]
            dec = jnp.exp(jnp.where(causal, diff, NEG))
            mcat.append((gmat[h // (H_B // G_B)] * dec).astype(BF16))
        mcat = jnp.concatenate(mcat, axis=1)
        dtx = xs * dt_e
        xbd = jnp.concatenate([jnp.where(head_lane == h, dtx, 0.0) for h in range(H_B)], axis=0).astype(BF16)
        y = _dotf(mcat, xbd) + _dotf(cmat, state.astype(BF16)) * ecum_e
        upd = lax.dot_general(bmat, (xs * wgt_e).astype(BF16), (((0,), (0,)), ((), ())), preferred_element_type=F32)
        state = state * ecum_e[q - 1:q, :] + jnp.where(same_grp, upd, 0.0)
        y = y + v256_ref[0:1, :] * xs
        ob_ref[r0:r0 + q, :] = (_rms(y * _silu(rest_ref[r0:r0 + q, R_BZ:R_BZ + W_B])) * v256_ref[1:2, :]).astype(BF16)
        qg = rest_ref[r0:r0 + q, R_CQ:R_CQ + 128] * (DK_C ** -0.5)
        kg = rest_ref[r0:r0 + q, R_CK:R_CK + 128]
        vg = rest_ref[r0:r0 + q, R_CV:R_CV + W_C]
        gk = _dotf(misc.astype(BF16), wlr_ref[...]) + v128_ref[2:3, :]
        gate = -_softplus(-gk) * (1.0 / GATE_NORM)
        bcum = _lmat_exact(tril, gate)
        gla.append((qg, kg, vg, bcum, qg * jnp.exp(bcum)))
    sbd_ref[...] = state
    ssm_out_ref[...] = state
    span = jnp.max(-gla[0][3][q - 1:q, :])
    for u in range(1, nsub):
        span = jnp.maximum(span, jnp.max(-gla[u][3][q - 1:q, :]))

    @pl.when(span <= 80.0)
    def _():
        causal4 = _iota((q, H_C * q), 0) >= (_iota((q, H_C * q), 1) & (q - 1))
        kmask = (_iota((128, 4 * q), 0) >> 5) == (_iota((128, 4 * q), 1) >> 7)
        for u in range(nsub):
            qg, kg, vg, bcum, qe = gla[u]
            ke = kg * jnp.exp(-bcum)
            kbd = jnp.where(kmask, jnp.tile(ke.T, (1, H_C)), 0.0).astype(BF16)
            att = jnp.where(causal4, _dotf(qe.astype(BF16), kbd), 0.0).astype(BF16)
            vbd = jnp.concatenate([jnp.where(vhead == h, vg, 0.0) for h in range(H_C)], axis=0).astype(BF16)
            oi_ref[u * q:(u + 1) * q, :] = _dotf(att, vbd)

    @pl.when(span > 80.0)
    def _():
        ind = jnp.where((_iota((128, W_C), 0) >> 5) == (_iota((128, W_C), 1) >> 6), 1.0, 0.0).astype(BF16)
        trow = _iota((q, 128), 0)
        for u in range(nsub):
            qg, kg, vg, bcum, qe = gla[u]
            b_ref[...] = bcum

            def body(grp, acc, u=u, qg=qg, bcum=bcum):
                base = pl.multiple_of(grp * 8, 8)
                ks8 = rest_ref[pl.ds(u * q + base, 8), R_CK:R_CK + 128]
                vs8 = rest_ref[pl.ds(u * q + base, 8), R_CV:R_CV + W_C]
                bs8 = b_ref[pl.ds(base, 8), :]
                for j in range(8):
                    d = jnp.exp(jnp.minimum(bcum - bs8[j:j + 1, :], 0.0)) * qg * ks8[j:j + 1, :]
                    d = jnp.where(trow >= base + j, d, 0.0)
                    acc = acc + _dotf(d.astype(BF16), ind) * vs8[j:j + 1, :]
                return acc

            oi_ref[u * q:(u + 1) * q, :] = lax.fori_loop(0, q // 8, body, jnp.zeros((q, W_C), F32))

    sg = sg_ref[...]
    avg = jnp.where((_iota((W_C, W_C), 0) >> 6) == (_iota((W_C, W_C), 1) >> 6), 1.0, 0.0).astype(BF16)
    for u in range(nsub):
        r0 = u * q
        qg, kg, vg, bcum, qe = gla[u]
        o = oi_ref[r0:r0 + q, :] + _dotf(qe.astype(BF16), sg.astype(BF16))
        k2 = (kg * jnp.exp(bcum[q - 1:q, :] - bcum)).astype(BF16)
        updg = lax.dot_general(k2, vg.astype(BF16), (((0,), (0,)), ((), ())), preferred_element_type=F32)
        sg = sg * jnp.exp(bcum.T[:, q - 1:q]) + jnp.where(same_head, updg, 0.0)
        ms = _rmat_exact(o * o, avg, 2) * (1.0 / DV_C)
        oc = o * lax.rsqrt(ms + EPS) * v256_ref[2:3, :] * _silu(rest_ref[r0:r0 + q, R_CG:R_CG + W_C])
        oc_ref[r0:r0 + q, :] = oc.astype(BF16)
    sg_ref[...] = sg
    gla_out_ref[...] = sg


def bc_mixers_prompt(lidx, rest, convw, convb3, vec128, vec256, wlr):
    b, seq, _ = rest.shape
    nsub = next(n for n in (4, 2, 1) if seq % (n * CHUNK) == 0)
    rows = nsub * CHUNK
    par = lambda r, w: pl.BlockSpec((None, r, w), lambda bi, c, l: (l[0], 0, 0))
    st = lambda r, w: pl.BlockSpec((None, r, w), lambda bi, c, l: (bi, 0, 0))
    grid_spec = pltpu.PrefetchScalarGridSpec(
        num_scalar_prefetch=1, grid=(b, seq // rows),
        in_specs=[pl.BlockSpec((None, rows, W_REST), lambda bi, c, l: (bi, c, 0)),
                  par(CONV_W, XBC_B), par(1, XBC_B), par(8, 128), par(8, 256), par(128, 128)],
        out_specs=[pl.BlockSpec((None, rows, W_B), lambda bi, c, l: (bi, c, 0)),
                   pl.BlockSpec((None, rows, W_C), lambda bi, c, l: (bi, c, 0)),
                   st(CONV_W - 1, XBC_B), st(128, W_B), st(128, W_C)],
        scratch_shapes=[pltpu.VMEM((8 + rows, XBC_B), F32), pltpu.VMEM((128, W_B), F32),
                        pltpu.VMEM((128, W_C), F32), pltpu.VMEM((CHUNK, 128), F32),
                        pltpu.VMEM((rows, W_C), F32)])
    return pl.pallas_call(
        functools.partial(_bc_prompt_kernel, nsub=nsub),
        out_shape=[jax.ShapeDtypeStruct((b, seq, W_B), BF16), jax.ShapeDtypeStruct((b, seq, W_C), BF16),
                   jax.ShapeDtypeStruct((b, CONV_W - 1, XBC_B), F32),
                   jax.ShapeDtypeStruct((b, 128, W_B), F32), jax.ShapeDtypeStruct((b, 128, W_C), F32)],
        grid_spec=grid_spec,
        compiler_params=_cparams(("arbitrary", "arbitrary")),
        name="bc_mixers_prompt",
    )(lidx, rest, convw, convb3, vec128, vec256, wlr)


def _unpack_ssm_state(sbd):
    b = sbd.shape[0]
    s = sbd.reshape(b, G_B, N_B, H_B, P_B)
    per_head = [s[:, h // (H_B // G_B), :, h, :] for h in range(H_B)]
    return jnp.swapaxes(jnp.stack(per_head, axis=1), 2, 3)


def _unpack_gla_state(sg):
    b = sg.shape[0]
    s = sg.reshape(b, H_C, DK_C, H_C, DV_C)
    return jnp.stack([s[:, h, :, h, :] for h in range(H_C)], axis=1)


def _out_proj_kernel(l_ref, oa_ref, ob_ref, oc_ref, w_ref, x_ref, g1_ref, sh2_ref, sc2_ref, x1_ref, h2_ref):
    del l_ref
    mix = (_dotf(oa_ref[...], w_ref[0:W_A, :]) + _dotf(ob_ref[...], w_ref[W_A:W_A + W_B, :])
           + _dotf(oc_ref[...], w_ref[W_A + W_B:, :]))
    x1 = x_ref[...] + g1_ref[...] * mix
    x1_ref[...] = x1
    h2_ref[...] = (_rms(x1) * (1.0 + sc2_ref[...]) + sh2_ref[...]).astype(h2_ref.dtype)


def out_projection(lidx, oa, ob, oc, w_out, x, mod, tm, h2_dtype):
    g, r, d = x.shape
    per_row = mod.shape[2] > 1
    act = lambda w: pl.BlockSpec((None, tm, w), lambda gi, i, l: (gi, i, 0))
    grid_spec = pltpu.PrefetchScalarGridSpec(
        num_scalar_prefetch=1, grid=(g, r // tm),
        in_specs=[act(W_A), act(W_B), act(W_C),
                  pl.BlockSpec((None, d, d), lambda gi, i, l: (l[0], 0, 0)),
                  act(d), _mod_spec(tm, per_row, 2), _mod_spec(tm, per_row, 3), _mod_spec(tm, per_row, 4)],
        out_specs=[act(d), act(d)])
    return pl.pallas_call(
        _out_proj_kernel,
        out_shape=[jax.ShapeDtypeStruct((g, r, d), F32), jax.ShapeDtypeStruct((g, r, d), h2_dtype)],
        grid_spec=grid_spec,
        compiler_params=_cparams(("arbitrary", "arbitrary")),
        name="out_projection",
    )(lidx, oa, ob, oc, w_out, x, mod, mod, mod)


def _ffn_kernel(i_ref, h_ref, wg_ref, wu_ref, wd_ref, x1_ref, g2_ref, fg_ref, o_ref, acc_ref, *, final):
    del i_ref
    f = pl.program_id(2)

    @pl.when(f == 0)
    def _():
        acc_ref[...] = jnp.zeros(acc_ref.shape, F32)

    h = h_ref[...]
    a = (_silu(_dotf(h, wg_ref[...])) * _dotf(h, wu_ref[...])).astype(BF16)
    acc_ref[...] += _dotf(a, wd_ref[...])

    @pl.when(f == pl.num_programs(2) - 1)
    def _():
        x2 = x1_ref[...] + g2_ref[...] * acc_ref[...]
        o_ref[...] = _rms(x2) * fg_ref[...] if final else x2


def dense_ffn(iidx, lidx_mod, h2, wg, wu, wd, x1, mod, final_g, tm, tf, final):
    g, r, d = x1.shape
    per_row = mod.shape[2] > 1
    nf = D_FF // tf
    act = lambda w: pl.BlockSpec((None, tm, w), lambda gi, i, f, s: (gi, i, 0))
    if per_row:
        g2 = pl.BlockSpec((None, None, tm, d), lambda gi, i, f, s: (s[1], gi, i, 5))
    else:
        g2 = pl.BlockSpec((None, None, 1, d), lambda gi, i, f, s: (s[1], gi, 0, 5))
    grid_spec = pltpu.PrefetchScalarGridSpec(
        num_scalar_prefetch=1, grid=(g, r // tm, nf),
        in_specs=[act(d),
                  pl.BlockSpec((None, d, tf), lambda gi, i, f, s: (s[0], 0, f)),
                  pl.BlockSpec((None, d, tf), lambda gi, i, f, s: (s[0], 0, f)),
                  pl.BlockSpec((None, tf, d), lambda gi, i, f, s: (s[0], f, 0)),
                  act(d), g2, pl.BlockSpec((1, d), lambda gi, i, f, s: (0, 0))],
        out_specs=act(d),
        scratch_shapes=[pltpu.VMEM((tm, d), F32)])
    sidx = jnp.concatenate([iidx, lidx_mod])
    return pl.pallas_call(
        functools.partial(_ffn_kernel, final=final),
        out_shape=jax.ShapeDtypeStruct((g, r, d), F32),
        grid_spec=grid_spec,
        compiler_params=_cparams(("arbitrary", "arbitrary", "arbitrary")),
        name="dense_ffn",
    )(sidx, h2, wg, wu, wd, x1, mod, final_g)


def _route_kernel(i_ref, h_ref, rw_ref, tri_ref, idx_ref, gate_ref, cnt_ref, carry_ref):
    del i_ref
    i = pl.program_id(0)

    @pl.when(i == 0)
    def _():
        carry_ref[...] = jnp.zeros(carry_ref.shape, F32)

    hh, hl = _split2(h_ref[...])
    wh, wl = _split2(rw_ref[...])
    logits = _dotf(hh, wh) + _dotf(hl, wh) + _dotf(hh, wl)
    lane = _iota(logits.shape, 1).astype(F32)
    logits = jnp.where(lane < N_EXPERTS, logits, NEG)
    m1 = jnp.max(logits, axis=1, keepdims=True)
    i1 = jnp.min(jnp.where(logits == m1, lane, 128.0), axis=1, keepdims=True)
    rest = jnp.where(lane == i1, NEG, logits)
    m2 = jnp.max(rest, axis=1, keepdims=True)
    i2 = jnp.min(jnp.where(rest == m2, lane, 128.0), axis=1, keepdims=True)
    e = jnp.exp(m2 - m1)
    g1 = 1.0 / (1.0 + e)
    sel = jnp.where(lane == i1, 1.0, 0.0) + jnp.where(lane == i2, 1.0, 0.0)
    before = _dotf(tri_ref[...], sel.astype(BF16)) + carry_ref[...]
    r1 = jnp.sum(jnp.where(lane == i1, before, 0.0), axis=1, keepdims=True)
    r2 = jnp.sum(jnp.where(lane == i2, before, 0.0), axis=1, keepdims=True)
    carry_ref[...] = carry_ref[...] + jnp.sum(sel, axis=0, keepdims=True)
    lane8 = _iota(idx_ref.shape, 1).astype(F32)
    idx_ref[...] = jnp.where(lane8 == 0.0, i1, jnp.where(lane8 == 1.0, i2, jnp.where(lane8 == 2.0, r1, r2))).astype(I32)
    gate_ref[...] = jnp.where(lane8 == 0.0, g1, e * g1)
    cnt_ref[...] = jnp.broadcast_to(carry_ref[...], cnt_ref.shape)


def moe_route(iidx, h2, router_pad, tm):
    t, d = h2.shape
    tri = jnp.asarray(np.tril(np.ones((tm, tm), np.float32), -1), BF16)
    grid_spec = pltpu.PrefetchScalarGridSpec(
        num_scalar_prefetch=1, grid=(t // tm,),
        in_specs=[pl.BlockSpec((tm, d), lambda i, s: (i, 0)),
                  pl.BlockSpec((None, d, 128), lambda i, s: (s[0], 0, 0)),
                  pl.BlockSpec((tm, tm), lambda i, s: (0, 0))],
        out_specs=[pl.BlockSpec((tm, 8), lambda i, s: (i, 0)), pl.BlockSpec((tm, 8), lambda i, s: (i, 0)),
                   pl.BlockSpec((8, 128), lambda i, s: (0, 0))],
        scratch_shapes=[pltpu.VMEM((1, 128), F32)])
    return pl.pallas_call(
        _route_kernel,
        out_shape=[jax.ShapeDtypeStruct((t, 8), I32), jax.ShapeDtypeStruct((t, 8), F32),
                   jax.ShapeDtypeStruct((8, 128), F32)],
        grid_spec=grid_spec,
        compiler_params=_cparams(("arbitrary",)),
        name="moe_route",
    )(iidx, h2, router_pad, tri)


def _row_copy(src_ref, src_row, dst_ref, dst_row, sem):
    return pltpu.make_async_copy(src_ref.at[pl.ds(src_row, 1), :], dst_ref.at[pl.ds(dst_row, 1), :], sem)


def _dispatch_kernel(dest_ref, fill_ref, h_ref, xb_ref, sem, zrow_ref, zsem):
    tm = h_ref.shape[0]

    @pl.when(pl.program_id(0) == 0)
    def _():
        zrow_ref[...] = jnp.zeros(zrow_ref.shape, F32)
        zr = zrow_ref.shape[0]
        for phase in ("start", "wait"):
            for e in range(N_EXPERTS + 1):
                lo, hi = fill_ref[2 * e], fill_ref[2 * e + 1]
                lo8 = jnp.minimum(((lo + 7) >> 3) << 3, hi)
                n_big = (hi - lo8) // zr
                mid = lo8 + n_big * zr

                def piece(rows, base, phase=phase):
                    def body(i, carry):
                        at = pl.multiple_of(base + i * rows, 8)
                        cp = pltpu.make_async_copy(zrow_ref.at[pl.ds(0, rows), :], xb_ref.at[pl.ds(at, rows), :], zsem)
                        cp.start() if phase == "start" else cp.wait()
                        return carry
                    return body

                def one(r, carry, phase=phase):
                    cp = _row_copy(zrow_ref, 0, xb_ref, r, zsem)
                    cp.start() if phase == "start" else cp.wait()
                    return carry

                lax.fori_loop(lo, lo8, one, 0)
                lax.fori_loop(0, n_big, piece(zr, lo8), 0)
                lax.fori_loop(0, (hi - mid) >> 3, piece(8, mid), 0)

    def start(r, carry):
        for k in range(TOP_K):
            _row_copy(h_ref, r, xb_ref, dest_ref[0, TOP_K * r + k], sem).start()
        return carry

    lax.fori_loop(0, tm, start, 0, unroll=8)
    for k in range(TOP_K):
        pltpu.make_async_copy(h_ref, xb_ref.at[pl.ds(0, tm), :], sem).wait()


def moe_dispatch(dest3, fill, h2, n_rows, tm):
    t, d = h2.shape
    return pl.pallas_call(
        _dispatch_kernel,
        out_shape=jax.ShapeDtypeStruct((n_rows, d), F32),
        grid=(t // tm,),
        in_specs=[pl.BlockSpec((None, 1, TOP_K * tm), lambda i: (i, 0, 0), memory_space=pltpu.SMEM),
                  pl.BlockSpec(memory_space=pltpu.SMEM),
                  pl.BlockSpec((tm, d), lambda i: (i, 0))],
        out_specs=pl.BlockSpec(memory_space=pl.ANY),
        scratch_shapes=[pltpu.SemaphoreType.DMA(()), pltpu.VMEM((64, d), F32), pltpu.SemaphoreType.DMA(())],
        compiler_params=pltpu.CompilerParams(dimension_semantics=("arbitrary",), vmem_limit_bytes=VMEM_LIMIT,
                                             has_side_effects=True),
        name="moe_dispatch",
    )(dest3, fill, h2)


def _expert_kernel(s_ref, x_ref, wg_ref, wu_ref, wd_ref, y_ref, xb_ref, acc_ref):
    i = pl.program_id(0)
    f = pl.program_id(1)
    nb = pl.num_programs(0)

    @pl.when(i < s_ref[nb + 1])
    def _():
        @pl.when(f == 0)
        def _():
            xb_ref[...] = x_ref[...].astype(BF16)
            acc_ref[...] = jnp.zeros(acc_ref.shape, F32)

        x = xb_ref[...]
        a = (_silu(_dotf(x, wg_ref[...])) * _dotf(x, wu_ref[...])).astype(BF16)
        acc_ref[...] += _dotf(a, wd_ref[...])

        @pl.when(f == pl.num_programs(1) - 1)
        def _():
            y_ref[...] = acc_ref[...]

    @pl.when(jnp.logical_and(i >= s_ref[nb + 1], f == pl.num_programs(1) - 1))
    def _():
        y_ref[...] = jnp.zeros(y_ref.shape, F32)


def moe_experts(sidx, xb, wg, wu, wd, blk, tf):
    n_rows, d = xb.shape
    nb = n_rows // blk
    nf = D_FF // tf

    def row_map(i, f, s):
        return (jnp.minimum(i, s[nb + 1] - 1), 0)

    def f_of(i, f, s):
        return jnp.where(i < s[nb + 1], f, nf - 1)

    grid_spec = pltpu.PrefetchScalarGridSpec(
        num_scalar_prefetch=1, grid=(nb, nf),
        in_specs=[pl.BlockSpec((blk, d), row_map),
                  pl.BlockSpec((None, None, d, tf), lambda i, f, s: (s[nb], s[i], 0, f_of(i, f, s))),
                  pl.BlockSpec((None, None, d, tf), lambda i, f, s: (s[nb], s[i], 0, f_of(i, f, s))),
                  pl.BlockSpec((None, None, tf, d), lambda i, f, s: (s[nb], s[i], f_of(i, f, s), 0))],
        out_specs=pl.BlockSpec((blk, d), lambda i, f, s: (i, 0)),
        scratch_shapes=[pltpu.VMEM((blk, d), BF16), pltpu.VMEM((blk, d), F32)])
    return pl.pallas_call(
        _expert_kernel,
        out_shape=jax.ShapeDtypeStruct((n_rows, d), F32),
        grid_spec=grid_spec,
        compiler_params=_cparams(("arbitrary", "arbitrary")),
        name="moe_experts",
    )(sidx, xb, wg, wu, wd)


def _combine_kernel(dest_ref, yb_ref, gate_ref, x1_ref, g2_ref, fg_ref, o_ref, buf_ref, sem, *, final):
    tm = x1_ref.shape[0]

    def start(r, carry):
        for k in range(TOP_K):
            _row_copy(yb_ref, dest_ref[0, TOP_K * r + k], buf_ref.at[k], r, sem).start()
        return carry

    lax.fori_loop(0, tm, start, 0, unroll=8)
    for k in range(TOP_K):
        pltpu.make_async_copy(yb_ref.at[pl.ds(0, tm), :], buf_ref.at[k], sem).wait()
    gate = gate_ref[...]
    f = gate[:, 0:1] * buf_ref[0] + gate[:, 1:2] * buf_ref[1]
    x2 = x1_ref[...] + g2_ref[...] * f
    o_ref[...] = _rms(x2) * fg_ref[...] if final else x2


def moe_combine(dest3, yb, gates, x1, mod, final_g, tm, final):
    g, r, d = x1.shape
    per_row = mod.shape[1] > 1
    nt = r // tm
    act = pl.BlockSpec((None, tm, d), lambda gi, i: (gi, i, 0))
    return pl.pallas_call(
        functools.partial(_combine_kernel, final=final),
        out_shape=jax.ShapeDtypeStruct((g, r, d), F32),
        grid=(g, nt),
        in_specs=[pl.BlockSpec((None, 1, TOP_K * tm), lambda gi, i: (gi * nt + i, 0, 0), memory_space=pltpu.SMEM),
                  pl.BlockSpec(memory_space=pl.ANY),
                  pl.BlockSpec((tm, 8), lambda gi, i: (gi * nt + i, 0)),
                  act,
                  (pl.BlockSpec((None, tm, d), lambda gi, i: (gi, i, 0)) if per_row
                   else pl.BlockSpec((None, 1, d), lambda gi, i: (gi, 0, 0))),
                  pl.BlockSpec((1, d), lambda gi, i: (0, 0))],
        out_specs=act,
        scratch_shapes=[pltpu.VMEM((TOP_K, tm, d), F32), pltpu.SemaphoreType.DMA(())],
        compiler_params=_cparams(("arbitrary", "arbitrary")),
        name="moe_combine",
    )(dest3, yb, gates, x1, mod, final_g)


def moe_ffn(layer_slot, h2, x1, mod_g2, router_pad, wg, wu, wd, final_g, tm, blk, tf, final):
    g, r, d = x1.shape
    t = g * r
    h2f = h2.reshape(t, d)
    iidx = jnp.array([layer_slot], I32)
    idx, gates, counts = moe_route(iidx, h2f, router_pad, _pick(t, 1024))
    counts = counts[0, :N_EXPERTS].astype(I32)
    padded = (counts + blk - 1) // blk * blk
    pad_end = jnp.cumsum(padded)
    pad_start = pad_end - padded
    first = sum(jnp.where(idx[:, 0:TOP_K] == e, pad_start[e], 0) for e in range(N_EXPERTS))
    dest = first + idx[:, TOP_K:2 * TOP_K]
    nb = -(-(t * TOP_K) // blk) + N_EXPERTS
    n_used = pad_end[-1] // blk
    blk_e = jnp.minimum(jnp.sum(jnp.arange(nb, dtype=I32)[:, None] * blk >= pad_end[None, :], axis=1), N_EXPERTS - 1)
    blk_e = jnp.where(jnp.arange(nb) < n_used, blk_e, blk_e[jnp.maximum(n_used - 1, 0)]).astype(I32)
    dest3 = dest.astype(I32).reshape(t // tm, 1, TOP_K * tm)
    fill = jnp.stack([jnp.append(pad_start + counts, pad_end[-1]),
                      jnp.append(pad_end, nb * blk)], axis=1).reshape(-1).astype(I32)
    xb = moe_dispatch(dest3, fill, h2f, nb * blk, tm)
    sidx = jnp.concatenate([blk_e, iidx, n_used.astype(I32)[None]])
    yb = moe_experts(sidx, xb, wg, wu, wd, blk, tf)
    return moe_combine(dest3, yb, gates, x1, mod_g2, final_g, tm, final)


def _attn_decode_kernel(pt_ref, lam_ref, lqk_ref, g_ref, q_ref, kn_ref, vn_ref, tab_ref, ck_ref, cv_ref, o_ref,
                        kbuf_ref, vbuf_ref, sem, m_ref, s_ref, acc_ref, *, pps):
    step = pl.program_id(1)
    n_steps = pl.num_programs(1)
    layer = pt_ref[pt_ref.shape[0] - 1]
    rows = 2 * H_A
    t = pl.program_id(0) * n_steps + step
    slot = lax.rem(t, 2)

    def start_pages(tt, sl):
        for i in range(pps):
            page = pt_ref[tt * pps + i]
            pltpu.make_async_copy(ck_ref.at[layer, page], kbuf_ref.at[sl, i], sem.at[0, sl]).start()
            pltpu.make_async_copy(cv_ref.at[layer, page], vbuf_ref.at[sl, i], sem.at[1, sl]).start()

    @pl.when(t == 0)
    def _():
        start_pages(0, 0)

    @pl.when(t + 1 < pl.num_programs(0) * n_steps)
    def _():
        start_pages(t + 1, 1 - slot)

    pltpu.make_async_copy(ck_ref.at[layer, pl.ds(0, pps)], kbuf_ref.at[slot], sem.at[0, slot]).wait()
    pltpu.make_async_copy(cv_ref.at[layer, pl.ds(0, pps)], vbuf_ref.at[slot], sem.at[1, slot]).wait()

    @pl.when(step == 0)
    def _():
        m_ref[...] = jnp.full(m_ref.shape, NEG, F32)
        s_ref[...] = jnp.zeros(s_ref.shape, F32)
        acc_ref[...] = jnp.zeros(acc_ref.shape, F32)

    qrow = jnp.broadcast_to(q_ref[...].astype(F32), (rows, W_A))
    own = (_iota((rows, W_A), 1) >> 6) == _iota((rows, W_A), 0)
    q8 = jnp.where(own, qrow, 0.0).astype(BF16)

    def accumulate(s, pv):
        m_old = m_ref[...]
        m_new = jnp.maximum(m_old, jnp.max(s, axis=1, keepdims=True))
        alpha = jnp.exp(m_old - m_new)
        p = jnp.exp(s - m_new)
        s_ref[...] = alpha * s_ref[...] + jnp.sum(p, axis=1, keepdims=True)
        acc_ref[...] = alpha * acc_ref[...] + pv(p)
        m_ref[...] = m_new

    scores = [_dotf(q8, kbuf_ref[slot, i].astype(BF16)) for i in range(pps)]
    scores[pps - 1] = jnp.where(step == n_steps - 1, scores[pps - 1] + tab_ref[0:rows, :], scores[pps - 1])
    s_all = jnp.concatenate(scores, axis=1)

    def value_page(i):
        heads = [vbuf_ref[slot, i, pl.ds(h, PAGE, stride=H_A), :] for h in range(H_A)]
        return jnp.concatenate(heads, axis=1).astype(BF16)

    def pv_pages(p):
        hi, lo = _split2(p)
        out = None
        for i in range(pps):
            v = value_page(i)
            part = _dotf(hi[:, i * PAGE:(i + 1) * PAGE], v) + _dotf(lo[:, i * PAGE:(i + 1) * PAGE], v)
            out = part if out is None else out + part
        return out

    accumulate(s_all, pv_pages)

    @pl.when(step == n_steps - 1)
    def _():
        kn = kn_ref[...].astype(BF16).astype(F32)
        vn = vn_ref[...].astype(BF16).astype(F32)
        s_new = jnp.sum(q8.astype(F32) * kn, axis=1, keepdims=True) + tab_ref[rows:2 * rows, 0:1]
        accumulate(s_new, lambda p: p * vn)
        o = acc_ref[...] * (1.0 / s_ref[...])
        lam = _lambda_value(lqk_ref[...], lam_ref[2 * layer])
        outs = []
        for h in range(H_A):
            blkh = o[:, h * DV_A:(h + 1) * DV_A]
            oh = blkh[2 * h:2 * h + 1, :] - lam * blkh[2 * h + 1:2 * h + 2, :]
            outs.append(_rms(oh) * g_ref[...] * lam_ref[2 * layer + 1])
        o_ref[...] = jnp.concatenate(outs, axis=1).astype(BF16)


def attention_decode(sidx, lam_consts, lambda_qk, subln_g3, q, k_new, v_new, tab, cache_k4, cache_v4, n_pages, pps):
    bd = q.shape[0]
    nl = sidx.shape[0] - 1
    assert n_pages % pps == 0
    row = pl.BlockSpec((None, 1, W_A), lambda b, s, pt: (b, 0, 0))
    grid_spec = pltpu.PrefetchScalarGridSpec(
        num_scalar_prefetch=1, grid=(bd, n_pages // pps),
        in_specs=[pl.BlockSpec(memory_space=pltpu.SMEM),
                  pl.BlockSpec((None, 4, DK_A), lambda b, s, pt: (pt[nl], 0, 0)),
                  pl.BlockSpec((None, 1, DV_A), lambda b, s, pt: (pt[nl], 0, 0)),
                  row, row, row,
                  pl.BlockSpec((16, PAGE), lambda b, s, pt: (0, 0)),
                  pl.BlockSpec(memory_space=pl.ANY), pl.BlockSpec(memory_space=pl.ANY)],
        out_specs=row,
        scratch_shapes=[pltpu.VMEM((2, pps, W_A, PAGE), F32), pltpu.VMEM((2, pps, H_A * PAGE, DV_A), F32),
                        pltpu.SemaphoreType.DMA((2, 2)),
                        pltpu.VMEM((2 * H_A, 1), F32), pltpu.VMEM((2 * H_A, 1), F32),
                        pltpu.VMEM((2 * H_A, W_A), F32)])
    return pl.pallas_call(
        functools.partial(_attn_decode_kernel, pps=pps),
        out_shape=jax.ShapeDtypeStruct((bd, 1, W_A), BF16),
        grid_spec=grid_spec,
        compiler_params=_cparams(("arbitrary", "arbitrary")),
        name="attention_decode",
    )(sidx, lam_consts, lambda_qk, subln_g3, q, k_new, v_new, tab, cache_k4, cache_v4)


def _bc_decode_pre_kernel(l_ref, rest_ref, buf_ref, convw_ref, convb_ref, v128_ref, wlr_ref,
                          xbc_ref, nbuf_ref, dt_ref, dec_ref, eg_ref):
    del l_ref
    u = rest_ref[:, R_BX:R_BX + XBC_B]
    conv = convb_ref[...] + convw_ref[CONV_W - 1:CONV_W, :] * u
    for j in range(CONV_W - 1):
        conv = conv + convw_ref[j:j + 1, :] * buf_ref[j]
    xbc_ref[...] = _silu(conv)
    for j in range(CONV_W - 2):
        nbuf_ref[j] = buf_ref[j + 1]
    nbuf_ref[CONV_W - 2] = u
    misc = rest_ref[:, R_MISC:R_MISC + 128]
    dt = _softplus(misc + v128_ref[0:1, :])
    dt_ref[...] = dt
    dec_ref[...] = jnp.exp(dt * (-jnp.exp(v128_ref[1:2, :])))
    gk = _dotf(misc.astype(BF16), wlr_ref[...]) + v128_ref[2:3, :]
    eg_ref[...] = jnp.exp(-_softplus(-gk) * (1.0 / GATE_NORM))


def bc_decode_pre(lidx, rest, conv_state_t, convw, convb3, vec128, wlr):
    bd = rest.shape[0]
    par = lambda r, w: pl.BlockSpec((None, r, w), lambda i, l: (l[0], 0, 0))
    full = lambda w: pl.BlockSpec((bd, w), lambda i, l: (0, 0))
    grid_spec = pltpu.PrefetchScalarGridSpec(
        num_scalar_prefetch=1, grid=(1,),
        in_specs=[full(W_REST),
                  pl.BlockSpec((None, CONV_W - 1, bd, XBC_B), lambda i, l: (l[0], 0, 0, 0)),
                  par(CONV_W, XBC_B), par(1, XBC_B), par(8, 128), par(128, 128)],
        out_specs=[full(XBC_B), pl.BlockSpec((CONV_W - 1, bd, XBC_B), lambda i, l: (0, 0, 0)),
                   full(128), full(128), full(128)])
    return pl.pallas_call(
        _bc_decode_pre_kernel,
        out_shape=[jax.ShapeDtypeStruct((bd, XBC_B), F32), jax.ShapeDtypeStruct((CONV_W - 1, bd, XBC_B), F32),
                   jax.ShapeDtypeStruct((bd, 128), F32), jax.ShapeDtypeStruct((bd, 128), F32),
                   jax.ShapeDtypeStruct((bd, 128), F32)],
        grid_spec=grid_spec,
        compiler_params=_cparams(("arbitrary",)),
        name="bc_decode_pre",
    )(lidx, rest, conv_state_t, convw, convb3, vec128, wlr)


def _bc_decode_state_kernel(l_ref, ssm_ref, gla_ref, x4_ref, b4_ref, c4_ref, dt4_ref, dec4_ref, bz4_ref,
                            dsk_ref, ng_ref, q4_ref, k4_ref, v4_ref, eg4_ref, cg4_ref, gng_ref,
                            ssm_out_ref, gla_out_ref, ob4_ref, oc4_ref):
    del l_ref
    x4 = x4_ref[...]
    s = ssm_ref[...] * dec4_ref[...] + (dt4_ref[...] * x4) * b4_ref[...]
    ssm_out_ref[...] = s
    y = jnp.sum(c4_ref[...] * s, axis=3, keepdims=True) + dsk_ref[...] * x4
    yg = y * _silu(bz4_ref[...])
    ms = jnp.sum(jnp.sum(yg * yg, axis=2, keepdims=True), axis=1, keepdims=True) * (1.0 / W_B)
    ob4_ref[...] = yg * lax.rsqrt(ms + EPS) * ng_ref[...]
    sg = gla_ref[...] * eg4_ref[...] + k4_ref[...] * v4_ref[...]
    gla_out_ref[...] = sg
    o = jnp.sum(q4_ref[...] * sg, axis=2, keepdims=True)
    oc4_ref[...] = _rms(o) * gng_ref[...] * _silu(cg4_ref[...])


def bc_decode_state(lidx, state_ssm, state_gla, x4, b4, c4, dt4, dec4, bz4, dsk4, ng4, q4, k4, v4, eg4, cg4, gng4):
    bd = x4.shape[0]

    def full(shape):
        n = len(shape)
        return pl.BlockSpec(shape, lambda i, l: (0,) * n)

    def layer(shape):
        n = len(shape)
        return pl.BlockSpec((None,) + shape, lambda i, l: (l[0],) + (0,) * n)

    s_ssm = (bd, H_B, P_B, N_B)
    s_gla = (bd, H_C, DK_C, DV_C)
    col_b = (bd, H_B, P_B, 1)
    row_b = (bd, H_B, 1, N_B)
    one_b = (bd, H_B, 1, 1)
    col_c = (bd, H_C, DK_C, 1)
    row_c = (bd, H_C, 1, DV_C)
    grid_spec = pltpu.PrefetchScalarGridSpec(
        num_scalar_prefetch=1, grid=(1,),
        in_specs=[layer(s_ssm), layer(s_gla), full(col_b), full(row_b), full(row_b), full(one_b), full(one_b),
                  full(col_b), layer((1, H_B, 1, 1)), layer((1, H_B, P_B, 1)),
                  full(col_c), full(col_c), full(row_c), full(col_c), full(row_c), layer((1, 1, 1, DV_C))],
        out_specs=[full(s_ssm), full(s_gla), full(col_b), full(row_c)])
    return pl.pallas_call(
        _bc_decode_state_kernel,
        out_shape=[jax.ShapeDtypeStruct(s_ssm, F32), jax.ShapeDtypeStruct(s_gla, F32),
                   jax.ShapeDtypeStruct(col_b, F32), jax.ShapeDtypeStruct(row_c, F32)],
        grid_spec=grid_spec,
        compiler_params=_cparams(("arbitrary",)),
        name="bc_decode_state",
    )(lidx, state_ssm, state_gla, x4, b4, c4, dt4, dec4, bz4, dsk4, ng4, q4, k4, v4, eg4, cg4, gng4)


def _pack_params(w_in, rel_bias, subln_g, conv_b, dt_bias, a_log, d_skip, ssd_norm_g, gla_w_lr, gla_b_lr,
                 gla_norm_g, router_w):
    depth = w_in.shape[0]
    pad = jnp.zeros(w_in.shape[:2] + (D_PACK - D_IN,), w_in.dtype)
    w_pack = jnp.concatenate([w_in[..., :OFF_BDT], w_in[..., OFF_CQ:OFF_CLR], w_in[..., OFF_BDT:OFF_CQ],
                              w_in[..., OFF_CLR:], pad], axis=-1).astype(BF16)
    vec128 = jnp.zeros((depth, 8, 128), F32)
    vec128 = vec128.at[:, 0, :H_B].set(dt_bias).at[:, 1, :H_B].set(a_log).at[:, 2, :].set(gla_b_lr)
    vec256 = jnp.zeros((depth, 8, 256), F32)
    vec256 = (vec256.at[:, 0, :].set(jnp.repeat(d_skip, P_B, axis=1)).at[:, 1, :].set(ssd_norm_g)
              .at[:, 2, :].set(jnp.tile(gla_norm_g, (1, H_C))))
    wlr = jnp.zeros((depth, 128, 128), F32).at[:, MISC_LR:MISC_LR + GATE_RANK, :].set(gla_w_lr).astype(BF16)
    router_pad = jnp.zeros(router_w.shape[:2] + (128,), F32).at[..., :N_EXPERTS].set(router_w)
    lam_init = [0.8 - 0.6 * math.exp(-0.3 * l) for l in range(depth)]
    lam_consts = jnp.asarray(np.array([[li, 1.0 - li] for li in lam_init], np.float32).reshape(-1))
    return dict(w_pack=w_pack, vec128=vec128, vec256=vec256, wlr=wlr, router_pad=router_pad,
                lam_consts=lam_consts, subln_g3=subln_g.reshape(depth, 1, DV_A),
                convb3=conv_b.reshape(depth, 1, XBC_B))


def _pick(n, pref):
    if n <= pref:
        return n
    t = pref
    while n % t:
        t //= 2
    return t


def _channel_mixer(l, depth, h2, x1, mod, pk, wts, tm, moe_blk, final):
    i = l // 2
    lmod = jnp.array([l], I32)
    if l % 2 == 0:
        return dense_ffn(jnp.array([i], I32), lmod, h2, wts['ffn_g'], wts['ffn_u'], wts['ffn_d'], x1, mod,
                         wts['final_g'], tm, FF_TILE, final)
    mod_g2 = mod[l, :, :, 5 * D_MODEL:6 * D_MODEL]
    return moe_ffn(i, h2, x1, mod_g2, pk['router_pad'], wts['moe_g'], wts['moe_u'], wts['moe_d'], wts['final_g'],
                   _pick(x1.shape[0] * x1.shape[1], 256), moe_blk, FF_TILE, final)


def _run_prompt(x, mod, pk, wts, lambda_qk, conv_w, rel_bias):
    b, seq, d = x.shape
    depth = mod.shape[0]
    tm = _pick(seq // 2, 512)
    bias_tiles = attention_bias_tiles(rel_bias, tm)
    k_all = jnp.zeros((depth, b, W_A, seq), F32)
    v_all = jnp.zeros((depth, b, H_A * seq, DV_A), F32)
    convs, ssms, glas = [], [], []
    for l in range(depth):
        lidx = jnp.array([l], I32)
        q, rest, kt, vb, k_all, v_all = in_projection_prompt(lidx, x, mod, pk['w_pack'], k_all, v_all, tm, 2 * tm)
        oa = attention_prompt(lidx, pk['lam_consts'], lambda_qk, pk['subln_g3'], q, kt, vb, bias_tiles, tm)
        ob, oc, conv_s, ssm_s, gla_s = bc_mixers_prompt(lidx, rest, conv_w, pk['convb3'], pk['vec128'],
                                                        pk['vec256'], pk['wlr'])
        moe = l % 2 == 1
        x1, h2 = out_projection(lidx, oa, ob, oc, wts['w_out'], x, mod, tm, F32 if moe else BF16)
        x = _channel_mixer(l, depth, h2, x1, mod, pk, wts, tm, _pick(b * seq, 512), l == depth - 1)
        convs.append(conv_s)
        ssms.append(_unpack_ssm_state(ssm_s))
        glas.append(_unpack_gla_state(gla_s))
    ks = jnp.transpose(k_all.reshape(depth, b, 2 * H_A, DK_A, seq), (0, 1, 4, 2, 3))
    vs = v_all.reshape(depth, b, seq, H_A, DV_A)
    return x, ks, vs, jnp.stack(convs), jnp.stack(ssms), jnp.stack(glas)


def _run_decode(x, mod, pk, wts, lambda_qk, conv_w, rel_bias, cache_k, cache_v, page_table,
                state_conv, state_ssm, state_gla, d_skip, ssd_norm_g, gla_norm_g):
    bd = x.shape[0]
    depth = mod.shape[0]
    n_pages = page_table.shape[1]
    past = n_pages * PAGE
    pps = _pick(n_pages, 16)
    xg = x.reshape(1, bd, D_MODEL)
    pool = cache_k.shape[1]
    assert PAGE >= T5_MAX_DIST
    cache_k4 = jnp.transpose(cache_k, (0, 1, 3, 4, 2)).reshape(depth, pool, 2 * H_A * DK_A, PAGE)
    cache_v4 = cache_v.reshape(depth, pool, PAGE * H_A, DV_A)
    table = _t5_table(rel_bias, PAGE + 1)
    dist = past - ((n_pages - 1) * PAGE + jnp.arange(PAGE))
    tab = jnp.concatenate([jnp.repeat(table[:, dist], 2, axis=0),
                           jnp.broadcast_to(jnp.repeat(table[:, 0], 2)[:, None], (2 * H_A, PAGE))], axis=0)
    conv_t = jnp.swapaxes(state_conv, 1, 2)
    grp = np.arange(H_B) // (H_B // G_B)
    dsk4 = d_skip.reshape(depth, 1, H_B, 1, 1)
    ng4 = ssd_norm_g.reshape(depth, 1, H_B, P_B, 1)
    gng4 = gla_norm_g.reshape(depth, 1, 1, 1, DV_C)
    pt_flat = page_table.reshape(-1).astype(I32)
    ks, vs, convs, ssms, glas = [], [], [], [], []
    for l in range(depth):
        lidx = jnp.array([l], I32)
        q, k, v, rest = in_projection(lidx, xg, mod, pk['w_pack'], bd)
        sidx = jnp.concatenate([pt_flat, lidx])
        oa = attention_decode(sidx, pk['lam_consts'], lambda_qk, pk['subln_g3'], q.reshape(bd, 1, W_A),
                              k.reshape(bd, 1, W_A), v.reshape(bd, 1, W_A), tab, cache_k4, cache_v4, n_pages, pps)
        rest2 = rest.reshape(bd, W_REST)
        xbc, nbuf, dt, dec, eg = bc_decode_pre(lidx, rest2, conv_t, conv_w, pk['convb3'], pk['vec128'], pk['wlr'])
        x4 = xbc[:, :W_B].reshape(bd, H_B, P_B, 1)
        b4 = xbc[:, W_B:W_B + G_B * N_B].reshape(bd, G_B, 1, N_B)[:, grp]
        c4 = xbc[:, W_B + G_B * N_B:].reshape(bd, G_B, 1, N_B)[:, grp]
        dt4 = dt[:, :H_B].reshape(bd, H_B, 1, 1)
        dec4 = dec[:, :H_B].reshape(bd, H_B, 1, 1)
        bz4 = rest2[:, R_BZ:R_BZ + W_B].reshape(bd, H_B, P_B, 1)
        q4 = (rest2[:, R_CQ:R_CQ + 128] * (DK_C ** -0.5)).reshape(bd, H_C, DK_C, 1)
        k4 = rest2[:, R_CK:R_CK + 128].reshape(bd, H_C, DK_C, 1)
        v4 = rest2[:, R_CV:R_CV + W_C].reshape(bd, H_C, 1, DV_C)
        eg4 = eg.reshape(bd, H_C, DK_C, 1)
        cg4 = rest2[:, R_CG:R_CG + W_C].reshape(bd, H_C, 1, DV_C)
        ssm_s, gla_s, ob4, oc4 = bc_decode_state(lidx, state_ssm, state_gla, x4, b4, c4, dt4, dec4, bz4, dsk4, ng4,
                                                 q4, k4, v4, eg4, cg4, gng4)
        ob = ob4.reshape(1, bd, W_B).astype(BF16)
        oc = oc4.reshape(1, bd, W_C).astype(BF16)
        moe = l % 2 == 1
        x1, h2 = out_projection(lidx, oa.reshape(1, bd, W_A), ob, oc, wts['w_out'], xg, mod, bd, F32 if moe else BF16)
        xg = _channel_mixer(l, depth, h2, x1, mod, pk, wts, bd, 64, l == depth - 1)
        ks.append(k.reshape(bd, 1, 2 * H_A, DK_A))
        vs.append(v.reshape(bd, 1, H_A, DV_A))
        convs.append(jnp.swapaxes(nbuf, 0, 1))
        ssms.append(ssm_s)
        glas.append(gla_s)
    return (xg.reshape(bd, 1, D_MODEL), jnp.stack(ks), jnp.stack(vs), jnp.stack(convs), jnp.stack(ssms),
            jnp.stack(glas))


def kernel(x_prompt, x_sample, c_prompt, c_sample, cache_k, cache_v, page_table, state_conv, state_ssm, state_gla,
           w_ada, b_ada, w_in, w_out, rel_bias, lambda_qk, subln_g, conv_w, conv_b, dt_bias, a_log, d_skip,
           ssd_norm_g, gla_w_lr, gla_b_lr, gla_norm_g, ffn_w_gate, ffn_w_up, ffn_w_down, router_w,
           moe_w_gate, moe_w_up, moe_w_down, final_norm_g):
    depth = w_in.shape[0]
    bp, bd = c_prompt.shape[0], c_sample.shape[0]
    pk = _pack_params(w_in, rel_bias, subln_g, conv_b, dt_bias, a_log, d_skip, ssd_norm_g, gla_w_lr, gla_b_lr,
                      gla_norm_g, router_w)
    wts = dict(w_out=w_out.astype(BF16), ffn_g=ffn_w_gate.astype(BF16), ffn_u=ffn_w_up.astype(BF16),
               ffn_d=ffn_w_down.astype(BF16), moe_g=moe_w_gate.astype(BF16), moe_u=moe_w_up.astype(BF16),
               moe_d=moe_w_down.astype(BF16), final_g=final_norm_g.reshape(1, D_MODEL))
    n_c = bp + bd
    n_cp = -(-n_c // 8) * 8
    c_all = jnp.concatenate([c_prompt, c_sample, jnp.zeros((n_cp - n_c, D_MODEL), F32)], axis=0)
    mod = ada_modulation(c_all, w_ada, b_ada)
    mod_p = mod[:, :bp].reshape(depth, bp, 1, 6 * D_MODEL)
    mod_d = mod[:, bp:n_c].reshape(depth, 1, bd, 6 * D_MODEL)
    yp, kp, vp, convp, ssmp, glap = _run_prompt(x_prompt, mod_p, pk, wts, lambda_qk, conv_w, rel_bias)
    yd, kd, vd, convd, ssmd, glad = _run_decode(x_sample, mod_d, pk, wts, lambda_qk, conv_w, rel_bias, cache_k,
                                                cache_v, page_table, state_conv, state_ssm, state_gla, d_skip,
                                                ssd_norm_g, gla_norm_g)
    return (yp, yd, kp, vp, convp, ssmp, glap, kd, vd, convd, ssmd, glad)
```

```python
import functools
import math

import numpy as np
import jax
import jax.numpy as jnp
from jax import lax
from jax.experimental import pallas as pl
from jax.experimental.pallas import tpu as pltpu

F32 = jnp.float32
BF16 = jnp.bfloat16
I32 = jnp.int32

D_MODEL = 1024
H_A, DK_A, DV_A = 4, 64, 128
W_A = H_A * DV_A
H_B, P_B, G_B, N_B, CONV_W = 4, 64, 2, 64, 4
W_B = H_B * P_B
XBC_B = W_B + 2 * G_B * N_B
H_C, DK_C, DV_C = 4, 32, 64
W_C = H_C * DV_C
GATE_RANK, GATE_NORM = 16, 16.0
T5_BUCKETS, T5_MAX_DIST = 32, 128
D_FF, N_EXPERTS, TOP_K = 2816, 8, 2
EPS = 1e-6
PAGE = 128
CHUNK = 128
FF_TILE = D_FF
OFF_AQ = 0
OFF_AK = OFF_AQ + 2 * H_A * DK_A
OFF_AV = OFF_AK + 2 * H_A * DK_A
OFF_BZ = OFF_AV + W_A
OFF_BX = OFF_BZ + W_B
OFF_BDT = OFF_BX + XBC_B
OFF_CQ = OFF_BDT + H_B
OFF_CK = OFF_CQ + H_C * DK_C
OFF_CV = OFF_CK + H_C * DK_C
OFF_CG = OFF_CV + W_C
OFF_CLR = OFF_CG + W_C
D_IN = OFF_CLR + GATE_RANK
PK_REST = 3 * W_A
R_BZ, R_BX, R_CQ, R_CK, R_CV, R_CG, R_MISC = 0, 256, 768, 896, 1024, 1280, 1536
W_REST = R_MISC + 128
D_PACK = PK_REST + W_REST
MISC_DT, MISC_LR = 0, H_B
NEG = -1e30
VMEM_LIMIT = 56 * 1024 * 1024


def _cparams(sem):
    return pltpu.CompilerParams(dimension_semantics=sem, vmem_limit_bytes=VMEM_LIMIT)


def _rms(x):
    return x * lax.rsqrt(jnp.mean(x * x, axis=-1, keepdims=True) + EPS)


def _silu(x):
    return x * jax.nn.sigmoid(x)


def _softplus(x):
    return jnp.maximum(x, 0.0) + jnp.log1p(jnp.exp(-jnp.abs(x)))


def _split2(x):
    hi = x.astype(BF16)
    return hi, (x - hi.astype(F32)).astype(BF16)


def _split3(x):
    hi = x.astype(BF16)
    r = x - hi.astype(F32)
    mid = r.astype(BF16)
    return hi, mid, (r - mid.astype(F32)).astype(BF16)


def _dotf(a, b):
    return jnp.dot(a, b, preferred_element_type=F32)


def _lmat_exact(mat01, x):
    hi, mid, lo = _split3(x)
    return _dotf(mat01, hi) + _dotf(mat01, mid) + _dotf(mat01, lo)


def _rmat_exact(x, mat01, pieces=3):
    if pieces == 2:
        hi, lo = _split2(x)
        return _dotf(hi, mat01) + _dotf(lo, mat01)
    hi, mid, lo = _split3(x)
    return _dotf(hi, mat01) + _dotf(mid, mat01) + _dotf(lo, mat01)


def _iota(shape, axis):
    return lax.broadcasted_iota(I32, shape, axis)


def _ada_kernel(c_ref, w_ref, b_ref, o_ref):
    c = c_ref[...]
    o_ref[...] = _dotf(_silu(c).astype(BF16), w_ref[...].astype(BF16)) + b_ref[...]


def ada_modulation(c_all, w_ada, b_ada):
    depth, d, n = w_ada.shape
    bc = c_all.shape[0]
    tn = 1536 if n % 1536 == 0 else n
    return pl.pallas_call(
        _ada_kernel,
        out_shape=jax.ShapeDtypeStruct((depth, bc, n), F32),
        grid=(depth, n // tn),
        in_specs=[pl.BlockSpec((bc, d), lambda l, j: (0, 0)),
                  pl.BlockSpec((None, d, tn), lambda l, j: (l, 0, j)),
                  pl.BlockSpec((None, 1, tn), lambda l, j: (l, 0, j))],
        out_specs=pl.BlockSpec((None, bc, tn), lambda l, j: (l, 0, j)),
        compiler_params=_cparams(("arbitrary", "arbitrary")),
        name="ada_modulation",
    )(c_all, w_ada, b_ada.reshape(depth, 1, n))


def _mod_spec(tm, per_row, chunk):
    if per_row:
        return pl.BlockSpec((None, None, tm, D_MODEL), lambda g, i, l: (l[0], g, i, chunk))
    return pl.BlockSpec((None, None, 1, D_MODEL), lambda g, i, l: (l[0], g, 0, chunk))


def _in_proj_kernel(l_ref, x_ref, sh_ref, sc_ref, w_ref, q_ref, k_ref, v_ref, r_ref):
    del l_ref
    h = (_rms(x_ref[...]) * (1.0 + sc_ref[...]) + sh_ref[...]).astype(BF16)
    q_ref[...] = (_dotf(h, w_ref[:, 0:W_A]) * (DK_A ** -0.5)).astype(BF16)
    k_ref[...] = _dotf(h, w_ref[:, W_A:2 * W_A])
    v_ref[...] = _dotf(h, w_ref[:, 2 * W_A:3 * W_A])
    r_ref[...] = _dotf(h, w_ref[:, PK_REST:D_PACK])


def in_projection(lidx, x, mod, w_pack, tm):
    g, r, d = x.shape
    per_row = mod.shape[2] > 1
    row = lambda w, dt: jax.ShapeDtypeStruct((g, r, w), dt)
    ospec = lambda w: pl.BlockSpec((None, tm, w), lambda gi, i, l: (gi, i, 0))
    out_specs = [ospec(W_A), ospec(W_A), ospec(W_A), ospec(W_REST)]
    out_shape = [row(W_A, BF16), row(W_A, F32), row(W_A, F32), row(W_REST, F32)]
    grid_spec = pltpu.PrefetchScalarGridSpec(
        num_scalar_prefetch=1, grid=(g, r // tm),
        in_specs=[pl.BlockSpec((None, tm, d), lambda gi, i, l: (gi, i, 0)),
                  _mod_spec(tm, per_row, 0), _mod_spec(tm, per_row, 1),
                  pl.BlockSpec((None, d, D_PACK), lambda gi, i, l: (l[0], 0, 0))],
        out_specs=out_specs)
    return pl.pallas_call(
        _in_proj_kernel,
        out_shape=out_shape,
        grid_spec=grid_spec,
        compiler_params=_cparams(("arbitrary", "arbitrary")),
        name="in_projection",
    )(lidx, x, mod, mod, w_pack)


def _in_proj_prompt_kernel(l_ref, x_ref, sh_ref, sc_ref, w_ref, kall_in_ref, vall_in_ref,
                           q_ref, r_ref, kt_ref, vb_ref, kall_ref, vall_ref, k_scr):
    del l_ref, kall_in_ref, vall_in_ref
    tm = x_ref.shape[0]
    h = (_rms(x_ref[...]) * (1.0 + sc_ref[...]) + sh_ref[...]).astype(BF16)
    q_ref[...] = (_dotf(h, w_ref[:, 0:W_A]) * (DK_A ** -0.5)).astype(BF16)
    k_scr[...] = _dotf(h, w_ref[:, W_A:2 * W_A])
    kt = k_scr[...].T
    kall_ref[...] = kt
    kt_ref[...] = kt.astype(BF16)
    v = _dotf(h, w_ref[:, 2 * W_A:3 * W_A])
    vb_ref[...] = v.astype(BF16)
    for head in range(H_A):
        vall_ref[pl.ds(head, tm, stride=H_A), :] = v[:, head * DV_A:(head + 1) * DV_A]
    r_ref[...] = _dotf(h, w_ref[:, PK_REST:D_PACK])


def in_projection_prompt(lidx, x, mod, w_pack, k_all, v_all, tm, kblk):
    b, seq, d = x.shape
    sub = kblk // tm
    act = lambda w: pl.BlockSpec((None, tm, w), lambda bi, i, l: (bi, i, 0))
    grid_spec = pltpu.PrefetchScalarGridSpec(
        num_scalar_prefetch=1, grid=(b, seq // tm),
        in_specs=[act(d), _mod_spec(tm, False, 0), _mod_spec(tm, False, 1),
                  pl.BlockSpec((None, d, D_PACK), lambda bi, i, l: (l[0], 0, 0)),
                  pl.BlockSpec(memory_space=pl.ANY), pl.BlockSpec(memory_space=pl.ANY)],
        out_specs=[act(W_A), act(W_REST),
                   pl.BlockSpec((None, None, W_A, tm), lambda bi, i, l: (bi, i // sub, 0, i % sub)),
                   act(W_A),
                   pl.BlockSpec((None, None, W_A, tm), lambda bi, i, l: (l[0], bi, 0, i)),
                   pl.BlockSpec((None, None, H_A * tm, DV_A), lambda bi, i, l: (l[0], bi, i, 0))],
        scratch_shapes=[pltpu.VMEM((tm, W_A), F32)])
    return pl.pallas_call(
        _in_proj_prompt_kernel,
        out_shape=[jax.ShapeDtypeStruct((b, seq, W_A), BF16), jax.ShapeDtypeStruct((b, seq, W_REST), F32),
                   jax.ShapeDtypeStruct((b, seq // kblk, W_A, kblk), BF16), jax.ShapeDtypeStruct((b, seq, W_A), BF16),
                   jax.ShapeDtypeStruct(k_all.shape, F32), jax.ShapeDtypeStruct(v_all.shape, F32)],
        grid_spec=grid_spec,
        input_output_aliases={5: 4, 6: 5},
        compiler_params=_cparams(("arbitrary", "arbitrary")),
        name="in_projection_prompt",
    )(lidx, x, mod, mod, w_pack, k_all, v_all)


def _lambda_value(lqk, lam_init):
    s01 = jnp.sum(lqk[0:1, :] * lqk[1:2, :], axis=1, keepdims=True)
    s23 = jnp.sum(lqk[2:3, :] * lqk[3:4, :], axis=1, keepdims=True)
    return jnp.exp(s01) - jnp.exp(s23) + lam_init


def _attn_prompt_kernel(l_ref, lam_ref, lqk_ref, g_ref, q_ref, kt_ref, v_ref, bias_ref, o_ref,
                        vaug_ref, m_ref, acc_ref, *, blk):
    qi = pl.program_id(2)

    @pl.when(qi == 0)
    def _():
        vaug_ref[:, 0:DV_A] = v_ref[...]
        ones_col = _iota((vaug_ref.shape[0], DV_A), 1) == 0
        vaug_ref[:, DV_A:2 * DV_A] = jnp.where(ones_col, 1.0, 0.0).astype(BF16)

    q = q_ref[...]
    lane = _iota(q.shape, 1)
    zero = jnp.zeros_like(q)
    qm = (jnp.where(lane < DK_A, q, zero), jnp.where(lane >= DK_A, q, zero))
    m_ref[...] = jnp.full(m_ref.shape, NEG, F32)
    acc_ref[...] = jnp.zeros(acc_ref.shape, F32)

    def update(kt, va, bias):
        for m in range(2):
            s = _dotf(qm[m], kt)
            if bias is not None:
                pieces = [s[:, i * blk:(i + 1) * blk] if b is None else s[:, i * blk:(i + 1) * blk] + b
                          for i, b in enumerate(bias)]
                s = pieces[0] if len(pieces) == 1 else jnp.concatenate(pieces, axis=1)
            m_old = m_ref[m]
            m_new = jnp.maximum(m_old, jnp.broadcast_to(jnp.max(s, axis=1, keepdims=True), m_old.shape))
            p = jnp.exp(s - jnp.tile(m_new, (1, s.shape[1] // 128))).astype(BF16)
            acc_ref[m] = jnp.tile(jnp.exp(m_old - m_new), (1, 2)) * acc_ref[m] + _dotf(p, va)
            m_ref[m] = m_new

    def key_block(j):
        return kt_ref[j], vaug_ref[pl.ds(pl.multiple_of(j * 2 * blk, 2 * blk), 2 * blk), :]

    def far(j, carry):
        update(*key_block(j), None)
        return carry

    a = qi >> 1
    odd = (qi & 1) == 1
    lax.fori_loop(0, jnp.where(odd, a, jnp.maximum(a - 1, 0)), far, 0)

    @pl.when(odd)
    def _():
        update(*key_block(a), (bias_ref[1], bias_ref[0]))

    @pl.when(jnp.logical_and(jnp.logical_not(odd), a >= 1))
    def _():
        update(*key_block(a - 1), (None, bias_ref[1]))

    @pl.when(jnp.logical_not(odd))
    def _():
        start = pl.multiple_of(a * 2 * blk, 2 * blk)
        update(kt_ref[a, :, 0:blk], vaug_ref[pl.ds(start, blk), :], (bias_ref[0],))

    layer = l_ref[0]
    lam = _lambda_value(lqk_ref[...], lam_ref[2 * layer])
    acc0 = acc_ref[0]
    acc1 = acc_ref[1]
    o = (acc0[:, 0:DV_A] * (1.0 / acc0[:, DV_A:DV_A + 1])
         - lam * (acc1[:, 0:DV_A] * (1.0 / acc1[:, DV_A:DV_A + 1])))
    o_ref[...] = (_rms(o) * g_ref[...] * lam_ref[2 * layer + 1]).astype(BF16)


def _t5_table(rel_bias, n):
    dist = jnp.arange(n)
    max_exact = T5_BUCKETS // 2
    nf = jnp.maximum(dist, 1).astype(F32)
    large = max_exact + (jnp.log(nf / max_exact) / math.log(T5_MAX_DIST / max_exact)
                         * (T5_BUCKETS - max_exact)).astype(I32)
    bucket = jnp.where(dist < max_exact, dist, jnp.minimum(large, T5_BUCKETS - 1))
    return (rel_bias[bucket] - rel_bias[T5_BUCKETS - 1][None, :]).T.astype(F32)


def _bias_tiles_kernel(rel_ref, o_ref):
    h = pl.program_id(0)
    blk = o_ref.shape[0]
    d = _iota((blk, blk), 0) - _iota((blk, blk), 1) + pl.program_id(1) * blk
    dist = jnp.maximum(d, 0)
    max_exact = T5_BUCKETS // 2
    nf = jnp.maximum(dist, 1).astype(F32)
    large = max_exact + (jnp.log(nf / max_exact) / math.log(T5_MAX_DIST / max_exact)
                         * (T5_BUCKETS - max_exact)).astype(I32)
    bucket = jnp.where(dist < max_exact, dist, jnp.minimum(large, T5_BUCKETS - 1))
    sat = rel_ref[(T5_BUCKETS - 1) * H_A + h]
    val = jnp.zeros((blk, blk), F32)
    for b in range(T5_BUCKETS - 1):
        val = jnp.where(bucket == b, rel_ref[b * H_A + h] - sat, val)
    o_ref[...] = jnp.where(d >= 0, val, NEG)


def attention_bias_tiles(rel_bias, blk):
    return pl.pallas_call(
        _bias_tiles_kernel,
        out_shape=jax.ShapeDtypeStruct((H_A, 2, blk, blk), F32),
        grid=(H_A, 2),
        in_specs=[pl.BlockSpec(memory_space=pltpu.SMEM)],
        out_specs=pl.BlockSpec((None, None, blk, blk), lambda h, t: (h, t, 0, 0)),
        compiler_params=_cparams(("arbitrary", "arbitrary")),
        name="attention_bias_tiles",
    )(rel_bias.reshape(-1))


def attention_prompt(lidx, lam_consts, lambda_qk, subln_g3, q, kt, vb, bias_tiles, blk):
    b, seq, _ = q.shape
    assert blk >= T5_MAX_DIST and blk % 128 == 0 and kt.shape == (b, seq // (2 * blk), W_A, 2 * blk)
    grid_spec = pltpu.PrefetchScalarGridSpec(
        num_scalar_prefetch=1, grid=(b, H_A, seq // blk),
        in_specs=[pl.BlockSpec(memory_space=pltpu.SMEM),
                  pl.BlockSpec((None, 4, DK_A), lambda bi, h, i, l: (l[0], 0, 0)),
                  pl.BlockSpec((None, 1, DV_A), lambda bi, h, i, l: (l[0], 0, 0)),
                  pl.BlockSpec((None, blk, DV_A), lambda bi, h, i, l: (bi, i, h)),
                  pl.BlockSpec((None, seq // (2 * blk), DV_A, 2 * blk), lambda bi, h, i, l: (bi, 0, h, 0)),
                  pl.BlockSpec((None, seq, DV_A), lambda bi, h, i, l: (bi, 0, h)),
                  pl.BlockSpec((None, 2, blk, blk), lambda bi, h, i, l: (h, 0, 0, 0))],
        out_specs=pl.BlockSpec((None, blk, DV_A), lambda bi, h, i, l: (bi, i, h)),
        scratch_shapes=[pltpu.VMEM((seq, 2 * DV_A), BF16), pltpu.VMEM((2, blk, 128), F32),
                        pltpu.VMEM((2, blk, 2 * DV_A), F32)])
    return pl.pallas_call(
        functools.partial(_attn_prompt_kernel, blk=blk),
        out_shape=jax.ShapeDtypeStruct((b, seq, W_A), BF16),
        grid_spec=grid_spec,
        compiler_params=_cparams(("arbitrary", "arbitrary", "arbitrary")),
        name="attention_prompt",
    )(lidx, lam_consts, lambda_qk, subln_g3, q, kt, vb, bias_tiles)


def _bc_prompt_kernel(l_ref, rest_ref, convw_ref, convb_ref, v128_ref, v256_ref, wlr_ref,
                      ob_ref, oc_ref, conv_out_ref, ssm_out_ref, gla_out_ref,
                      xpad_ref, sbd_ref, sg_ref, b_ref, oi_ref, *, nsub):
    del l_ref
    c = pl.program_id(1)
    q = CHUNK
    rows = nsub * q

    @pl.when(c == 0)
    def _():
        xpad_ref[0:8, :] = jnp.zeros((8, XBC_B), F32)
        sbd_ref[...] = jnp.zeros(sbd_ref.shape, F32)
        sg_ref[...] = jnp.zeros(sg_ref.shape, F32)

    xpad_ref[8:8 + rows, :] = rest_ref[:, R_BX:R_BX + XBC_B]
    conv = convb_ref[...]
    for j in range(CONV_W):
        conv = conv + convw_ref[j:j + 1, :] * xpad_ref[5 + j:5 + j + rows, :]
    xbc_all = _silu(conv)
    tail = xpad_ref[5 + rows:8 + rows, :]
    conv_out_ref[...] = tail
    xpad_ref[5:8, :] = tail

    row = _iota((q, q), 0)
    col = _iota((q, q), 1)
    causal = row >= col
    tril = jnp.where(causal, 1.0, 0.0).astype(BF16)
    lane1 = _iota((1, 128), 1)
    is_head = lane1 < H_B
    expand = jnp.where(_iota((128, W_B), 0) == (_iota((128, W_B), 1) >> 6), 1.0, 0.0).astype(BF16)
    grp_lane = _iota((q, 128), 1) >> 6
    head_lane = _iota((q, W_B), 1) >> 6
    same_grp = (_iota((128, W_B), 0) >> 6) == (_iota((128, W_B), 1) >> 7)
    same_head = (_iota((128, W_C), 0) >> 5) == (_iota((128, W_C), 1) >> 6)
    vhead = _iota((q, W_C), 1) >> 6
    a_neg = jnp.where(is_head, -jnp.exp(v128_ref[1:2, :]), 0.0)

    state = sbd_ref[...]
    gla = []
    for u in range(nsub):
        r0 = u * q
        xbc = xbc_all[r0:r0 + q, :]
        misc = rest_ref[r0:r0 + q, R_MISC:R_MISC + 128]
        xs = xbc[:, 0:W_B]
        bmat = xbc[:, W_B:W_B + G_B * N_B].astype(BF16)
        cmat = xbc[:, W_B + G_B * N_B:XBC_B].astype(BF16)
        dt = jnp.where(is_head, _softplus(misc + v128_ref[0:1, :]), 0.0)
        cum = _lmat_exact(tril, dt * a_neg)
        cum_t = cum.T
        ecum = jnp.exp(cum)
        wgt = jnp.exp(cum[q - 1:q, :] - cum) * dt
        dt_e = _rmat_exact(dt, expand, 2)
        ecum_e = _rmat_exact(ecum, expand, 2)
        wgt_e = _rmat_exact(wgt, expand, 2)
        zero_b = jnp.zeros_like(cmat)
        gmat = [lax.dot_general(jnp.where(grp_lane == g, cmat, zero_b), bmat, (((1,), (1,)), ((), ())),
                                preferred_element_type=F32) for g in range(G_B)]
        mcat = []
        for h in range(H_B):
            diff = cum[:, h:h + 1] - cum_t[h:h + 1, :]
            dec = jnp.exp(jnp.where(causal, diff, NEG))
            mcat.append((gmat[h // (H_B // G_B)] * dec).astype(BF16))
        mcat = jnp.concatenate(mcat, axis=1)
        dtx = xs * dt_e
        xbd = jnp.concatenate([jnp.where(head_lane == h, dtx, 0.0) for h in range(H_B)], axis=0).astype(BF16)
        y = _dotf(mcat, xbd) + _dotf(cmat, state.astype(BF16)) * ecum_e
        upd = lax.dot_general(bmat, (xs * wgt_e).astype(BF16), (((0,), (0,)), ((), ())), preferred_element_type=F32)
        state = state * ecum_e[q - 1:q, :] + jnp.where(same_grp, upd, 0.0)
        y = y + v256_ref[0:1, :] * xs
        ob_ref[r0:r0 + q, :] = (_rms(y * _silu(rest_ref[r0:r0 + q, R_BZ:R_BZ + W_B])) * v256_ref[1:2, :]).astype(BF16)
        qg = rest_ref[r0:r0 + q, R_CQ:R_CQ + 128] * (DK_C ** -0.5)
        kg = rest_ref[r0:r0 + q, R_CK:R_CK + 128]
        vg = rest_ref[r0:r0 + q, R_CV:R_CV + W_C]
        gk = _dotf(misc.astype(BF16), wlr_ref[...]) + v128_ref[2:3, :]
        gate = -_softplus(-gk) * (1.0 / GATE_NORM)
        bcum = _lmat_exact(tril, gate)
        gla.append((qg, kg, vg, bcum, qg * jnp.exp(bcum)))
    sbd_ref[...] = state
    ssm_out_ref[...] = state
    span = jnp.max(-gla[0][3][q - 1:q, :])
    for u in range(1, nsub):
        span = jnp.maximum(span, jnp.max(-gla[u][3][q - 1:q, :]))

    @pl.when(span <= 80.0)
    def _():
        causal4 = _iota((q, H_C * q), 0) >= (_iota((q, H_C * q), 1) & (q - 1))
        kmask = (_iota((128, 4 * q), 0) >> 5) == (_iota((128, 4 * q), 1) >> 7)
        for u in range(nsub):
            qg, kg, vg, bcum, qe = gla[u]
            ke = kg * jnp.exp(-bcum)
            kbd = jnp.where(kmask, jnp.tile(ke.T, (1, H_C)), 0.0).astype(BF16)
            att = jnp.where(causal4, _dotf(qe.astype(BF16), kbd), 0.0).astype(BF16)
            vbd = jnp.concatenate([jnp.where(vhead == h, vg, 0.0) for h in range(H_C)], axis=0).astype(BF16)
            oi_ref[u * q:(u + 1) * q, :] = _dotf(att, vbd)

    @pl.when(span > 80.0)
    def _():
        ind = jnp.where((_iota((128, W_C), 0) >> 5) == (_iota((128, W_C), 1) >> 6), 1.0, 0.0).astype(BF16)
        trow = _iota((q, 128), 0)
        for u in range(nsub):
            qg, kg, vg, bcum, qe = gla[u]
            b_ref[...] = bcum

            def body(grp, acc, u=u, qg=qg, bcum=bcum):
                base = pl.multiple_of(grp * 8, 8)
                ks8 = rest_ref[pl.ds(u * q + base, 8), R_CK:R_CK + 128]
                vs8 = rest_ref[pl.ds(u * q + base, 8), R_CV:R_CV + W_C]
                bs8 = b_ref[pl.ds(base, 8), :]
                for j in range(8):
                    d = jnp.exp(jnp.minimum(bcum - bs8[j:j + 1, :], 0.0)) * qg * ks8[j:j + 1, :]
                    d = jnp.where(trow >= base + j, d, 0.0)
                    acc = acc + _dotf(d.astype(BF16), ind) * vs8[j:j + 1, :]
                return acc

            oi_ref[u * q:(u + 1) * q, :] = lax.fori_loop(0, q // 8, body, jnp.zeros((q, W_C), F32))

    sg = sg_ref[...]
    avg = jnp.where((_iota((W_C, W_C), 0) >> 6) == (_iota((W_C, W_C), 1) >> 6), 1.0, 0.0).astype(BF16)
    for u in range(nsub):
        r0 = u * q
        qg, kg, vg, bcum, qe = gla[u]
        o = oi_ref[r0:r0 + q, :] + _dotf(qe.astype(BF16), sg.astype(BF16))
        k2 = (kg * jnp.exp(bcum[q - 1:q, :] - bcum)).astype(BF16)
        updg = lax.dot_general(k2, vg.astype(BF16), (((0,), (0,)), ((), ())), preferred_element_type=F32)
        sg = sg * jnp.exp(bcum.T[:, q - 1:q]) + jnp.where(same_head, updg, 0.0)
        ms = _rmat_exact(o * o, avg, 2) * (1.0 / DV_C)
        oc = o * lax.rsqrt(ms + EPS) * v256_ref[2:3, :] * _silu(rest_ref[r0:r0 + q, R_CG:R_CG + W_C])
        oc_ref[r0:r0 + q, :] = oc.astype(BF16)
    sg_ref[...] = sg
    gla_out_ref[...] = sg


def bc_mixers_prompt(lidx, rest, convw, convb3, vec128, vec256, wlr):
    b, seq, _ = rest.shape
    nsub = next(n for n in (4, 2, 1) if seq % (n * CHUNK) == 0)
    rows = nsub * CHUNK
    par = lambda r, w: pl.BlockSpec((None, r, w), lambda bi, c, l: (l[0], 0, 0))
    st = lambda r, w: pl.BlockSpec((None, r, w), lambda bi, c, l: (bi, 0, 0))
    grid_spec = pltpu.PrefetchScalarGridSpec(
        num_scalar_prefetch=1, grid=(b, seq // rows),
        in_specs=[pl.BlockSpec((None, rows, W_REST), lambda bi, c, l: (bi, c, 0)),
                  par(CONV_W, XBC_B), par(1, XBC_B), par(8, 128), par(8, 256), par(128, 128)],
        out_specs=[pl.BlockSpec((None, rows, W_B), lambda bi, c, l: (bi, c, 0)),
                   pl.BlockSpec((None, rows, W_C), lambda bi, c, l: (bi, c, 0)),
                   st(CONV_W - 1, XBC_B), st(128, W_B), st(128, W_C)],
        scratch_shapes=[pltpu.VMEM((8 + rows, XBC_B), F32), pltpu.VMEM((128, W_B), F32),
                        pltpu.VMEM((128, W_C), F32), pltpu.VMEM((CHUNK, 128), F32),
                        pltpu.VMEM((rows, W_C), F32)])
    return pl.pallas_call(
        functools.partial(_bc_prompt_kernel, nsub=nsub),
        out_shape=[jax.ShapeDtypeStruct((b, seq, W_B), BF16), jax.ShapeDtypeStruct((b, seq, W_C), BF16),
                   jax.ShapeDtypeStruct((b, CONV_W - 1, XBC_B), F32),
                   jax.ShapeDtypeStruct((b, 128, W_B), F32), jax.ShapeDtypeStruct((b, 128, W_C), F32)],
        grid_spec=grid_spec,
        compiler_params=_cparams(("arbitrary", "arbitrary")),
        name="bc_mixers_prompt",
    )(lidx, rest, convw, convb3, vec128, vec256, wlr)


def _unpack_ssm_state(sbd):
    b = sbd.shape[0]
    s = sbd.reshape(b, G_B, N_B, H_B, P_B)
    per_head = [s[:, h // (H_B // G_B), :, h, :] for h in range(H_B)]
    return jnp.swapaxes(jnp.stack(per_head, axis=1), 2, 3)


def _unpack_gla_state(sg):
    b = sg.shape[0]
    s = sg.reshape(b, H_C, DK_C, H_C, DV_C)
    return jnp.stack([s[:, h, :, h, :] for h in range(H_C)], axis=1)


def _out_proj_kernel(l_ref, oa_ref, ob_ref, oc_ref, w_ref, x_ref, g1_ref, sh2_ref, sc2_ref, x1_ref, h2_ref):
    del l_ref
    mix = (_dotf(oa_ref[...], w_ref[0:W_A, :]) + _dotf(ob_ref[...], w_ref[W_A:W_A + W_B, :])
           + _dotf(oc_ref[...], w_ref[W_A + W_B:, :]))
    x1 = x_ref[...] + g1_ref[...] * mix
    x1_ref[...] = x1
    h2_ref[...] = (_rms(x1) * (1.0 + sc2_ref[...]) + sh2_ref[...]).astype(h2_ref.dtype)


def out_projection(lidx, oa, ob, oc, w_out, x, mod, tm, h2_dtype):
    g, r, d = x.shape
    per_row = mod.shape[2] > 1
    act = lambda w: pl.BlockSpec((None, tm, w), lambda gi, i, l: (gi, i, 0))
    grid_spec = pltpu.PrefetchScalarGridSpec(
        num_scalar_prefetch=1, grid=(g, r // tm),
        in_specs=[act(W_A), act(W_B), act(W_C),
                  pl.BlockSpec((None, d, d), lambda gi, i, l: (l[0], 0, 0)),
                  act(d), _mod_spec(tm, per_row, 2), _mod_spec(tm, per_row, 3), _mod_spec(tm, per_row, 4)],
        out_specs=[act(d), act(d)])
    return pl.pallas_call(
        _out_proj_kernel,
        out_shape=[jax.ShapeDtypeStruct((g, r, d), F32), jax.ShapeDtypeStruct((g, r, d), h2_dtype)],
        grid_spec=grid_spec,
        compiler_params=_cparams(("arbitrary", "arbitrary")),
        name="out_projection",
    )(lidx, oa, ob, oc, w_out, x, mod, mod, mod)


def _ffn_kernel(i_ref, h_ref, wg_ref, wu_ref, wd_ref, x1_ref, g2_ref, fg_ref, o_ref, acc_ref, *, final):
    del i_ref
    f = pl.program_id(2)

    @pl.when(f == 0)
    def _():
        acc_ref[...] = jnp.zeros(acc_ref.shape, F32)

    h = h_ref[...]
    a = (_silu(_dotf(h, wg_ref[...])) * _dotf(h, wu_ref[...])).astype(BF16)
    acc_ref[...] += _dotf(a, wd_ref[...])

    @pl.when(f == pl.num_programs(2) - 1)
    def _():
        x2 = x1_ref[...] + g2_ref[...] * acc_ref[...]
        o_ref[...] = _rms(x2) * fg_ref[...] if final else x2


def dense_ffn(iidx, lidx_mod, h2, wg, wu, wd, x1, mod, final_g, tm, tf, final):
    g, r, d = x1.shape
    per_row = mod.shape[2] > 1
    nf = D_FF // tf
    act = lambda w: pl.BlockSpec((None, tm, w), lambda gi, i, f, s: (gi, i, 0))
    if per_row:
        g2 = pl.BlockSpec((None, None, tm, d), lambda gi, i, f, s: (s[1], gi, i, 5))
    else:
        g2 = pl.BlockSpec((None, None, 1, d), lambda gi, i, f, s: (s[1], gi, 0, 5))
    grid_spec = pltpu.PrefetchScalarGridSpec(
        num_scalar_prefetch=1, grid=(g, r // tm, nf),
        in_specs=[act(d),
                  pl.BlockSpec((None, d, tf), lambda gi, i, f, s: (s[0], 0, f)),
                  pl.BlockSpec((None, d, tf), lambda gi, i, f, s: (s[0], 0, f)),
                  pl.BlockSpec((None, tf, d), lambda gi, i, f, s: (s[0], f, 0)),
                  act(d), g2, pl.BlockSpec((1, d), lambda gi, i, f, s: (0, 0))],
        out_specs=act(d),
        scratch_shapes=[pltpu.VMEM((tm, d), F32)])
    sidx = jnp.concatenate([iidx, lidx_mod])
    return pl.pallas_call(
        functools.partial(_ffn_kernel, final=final),
        out_shape=jax.ShapeDtypeStruct((g, r, d), F32),
        grid_spec=grid_spec,
        compiler_params=_cparams(("arbitrary", "arbitrary", "arbitrary")),
        name="dense_ffn",
    )(sidx, h2, wg, wu, wd, x1, mod, final_g)


def _route_kernel(i_ref, h_ref, rw_ref, tri_ref, idx_ref, gate_ref, cnt_ref, carry_ref):
    del i_ref
    i = pl.program_id(0)

    @pl.when(i == 0)
    def _():
        carry_ref[...] = jnp.zeros(carry_ref.shape, F32)

    hh, hl = _split2(h_ref[...])
    wh, wl = _split2(rw_ref[...])
    logits = _dotf(hh, wh) + _dotf(hl, wh) + _dotf(hh, wl)
    lane = _iota(logits.shape, 1).astype(F32)
    logits = jnp.where(lane < N_EXPERTS, logits, NEG)
    m1 = jnp.max(logits, axis=1, keepdims=True)
    i1 = jnp.min(jnp.where(logits == m1, lane, 128.0), axis=1, keepdims=True)
    rest = jnp.where(lane == i1, NEG, logits)
    m2 = jnp.max(rest, axis=1, keepdims=True)
    i2 = jnp.min(jnp.where(rest == m2, lane, 128.0), axis=1, keepdims=True)
    e = jnp.exp(m2 - m1)
    g1 = 1.0 / (1.0 + e)
    sel = jnp.where(lane == i1, 1.0, 0.0) + jnp.where(lane == i2, 1.0, 0.0)
    before = _dotf(tri_ref[...], sel.astype(BF16)) + carry_ref[...]
    r1 = jnp.sum(jnp.where(lane == i1, before, 0.0), axis=1, keepdims=True)
    r2 = jnp.sum(jnp.where(lane == i2, before, 0.0), axis=1, keepdims=True)
    carry_ref[...] = carry_ref[...] + jnp.sum(sel, axis=0, keepdims=True)
    lane8 = _iota(idx_ref.shape, 1).astype(F32)
    idx_ref[...] = jnp.where(lane8 == 0.0, i1, jnp.where(lane8 == 1.0, i2, jnp.where(lane8 == 2.0, r1, r2))).astype(I32)
    gate_ref[...] = jnp.where(lane8 == 0.0, g1, e * g1)
    cnt_ref[...] = jnp.broadcast_to(carry_ref[...], cnt_ref.shape)


def moe_route(iidx, h2, router_pad, tm):
    t, d = h2.shape
    tri = jnp.asarray(np.tril(np.ones((tm, tm), np.float32), -1), BF16)
    grid_spec = pltpu.PrefetchScalarGridSpec(
        num_scalar_prefetch=1, grid=(t // tm,),
        in_specs=[pl.BlockSpec((tm, d), lambda i, s: (i, 0)),
                  pl.BlockSpec((None, d, 128), lambda i, s: (s[0], 0, 0)),
                  pl.BlockSpec((tm, tm), lambda i, s: (0, 0))],
        out_specs=[pl.BlockSpec((tm, 8), lambda i, s: (i, 0)), pl.BlockSpec((tm, 8), lambda i, s: (i, 0)),
                   pl.BlockSpec((8, 128), lambda i, s: (0, 0))],
        scratch_shapes=[pltpu.VMEM((1, 128), F32)])
    return pl.pallas_call(
        _route_kernel,
        out_shape=[jax.ShapeDtypeStruct((t, 8), I32), jax.ShapeDtypeStruct((t, 8), F32),
                   jax.ShapeDtypeStruct((8, 128), F32)],
        grid_spec=grid_spec,
        compiler_params=_cparams(("arbitrary",)),
        name="moe_route",
    )(iidx, h2, router_pad, tri)


def _row_copy(src_ref, src_row, dst_ref, dst_row, sem):
    return pltpu.make_async_copy(src_ref.at[pl.ds(src_row, 1), :], dst_ref.at[pl.ds(dst_row, 1), :], sem)


def _dispatch_kernel(dest_ref, fill_ref, h_ref, xb_ref, sem, zrow_ref, zsem):
    tm = h_ref.shape[0]

    @pl.when(pl.program_id(0) == 0)
    def _():
        zrow_ref[...] = jnp.zeros(zrow_ref.shape, F32)
        zr = zrow_ref.shape[0]
        for phase in ("start", "wait"):
            for e in range(N_EXPERTS + 1):
                lo, hi = fill_ref[2 * e], fill_ref[2 * e + 1]
                lo8 = jnp.minimum(((lo + 7) >> 3) << 3, hi)
                n_big = (hi - lo8) // zr
                mid = lo8 + n_big * zr

                def piece(rows, base, phase=phase):
                    def body(i, carry):
                        at = pl.multiple_of(base + i * rows, 8)
                        cp = pltpu.make_async_copy(zrow_ref.at[pl.ds(0, rows), :], xb_ref.at[pl.ds(at, rows), :], zsem)
                        cp.start() if phase == "start" else cp.wait()
                        return carry
                    return body

                def one(r, carry, phase=phase):
                    cp = _row_copy(zrow_ref, 0, xb_ref, r, zsem)
                    cp.start() if phase == "start" else cp.wait()
                    return carry

                lax.fori_loop(lo, lo8, one, 0)
                lax.fori_loop(0, n_big, piece(zr, lo8), 0)
                lax.fori_loop(0, (hi - mid) >> 3, piece(8, mid), 0)

    def start(r, carry):
        for k in range(TOP_K):
            _row_copy(h_ref, r, xb_ref, dest_ref[0, TOP_K * r + k], sem).start()
        return carry

    lax.fori_loop(0, tm, start, 0, unroll=8)
    for k in range(TOP_K):
        pltpu.make_async_copy(h_ref, xb_ref.at[pl.ds(0, tm), :], sem).wait()


def moe_dispatch(dest3, fill, h2, n_rows, tm):
    t, d = h2.shape
    return pl.pallas_call(
        _dispatch_kernel,
        out_shape=jax.ShapeDtypeStruct((n_rows, d), F32),
        grid=(t // tm,),
        in_specs=[pl.BlockSpec((None, 1, TOP_K * tm), lambda i: (i, 0, 0), memory_space=pltpu.SMEM),
                  pl.BlockSpec(memory_space=pltpu.SMEM),
                  pl.BlockSpec((tm, d), lambda i: (i, 0))],
        out_specs=pl.BlockSpec(memory_space=pl.ANY),
        scratch_shapes=[pltpu.SemaphoreType.DMA(()), pltpu.VMEM((64, d), F32), pltpu.SemaphoreType.DMA(())],
        compiler_params=pltpu.CompilerParams(dimension_semantics=("arbitrary",), vmem_limit_bytes=VMEM_LIMIT,
                                             has_side_effects=True),
        name="moe_dispatch",
    )(dest3, fill, h2)


def _expert_kernel(s_ref, x_ref, wg_ref, wu_ref, wd_ref, y_ref, xb_ref, acc_ref):
    i = pl.program_id(0)
    f = pl.program_id(1)
    nb = pl.num_programs(0)

    @pl.when(i < s_ref[nb + 1])
    def _():
        @pl.when(f == 0)
        def _():
            xb_ref[...] = x_ref[...].astype(BF16)
            acc_ref[...] = jnp.zeros(acc_ref.shape, F32)

        x = xb_ref[...]
        a = (_silu(_dotf(x, wg_ref[...])) * _dotf(x, wu_ref[...])).astype(BF16)
        acc_ref[...] += _dotf(a, wd_ref[...])

        @pl.when(f == pl.num_programs(1) - 1)
        def _():
            y_ref[...] = acc_ref[...]

    @pl.when(jnp.logical_and(i >= s_ref[nb + 1], f == pl.num_programs(1) - 1))
    def _():
        y_ref[...] = jnp.zeros(y_ref.shape, F32)


def moe_experts(sidx, xb, wg, wu, wd, blk, tf):
    n_rows, d = xb.shape
    nb = n_rows // blk
    nf = D_FF // tf

    def row_map(i, f, s):
        return (jnp.minimum(i, s[nb + 1] - 1), 0)

    def f_of(i, f, s):
        return jnp.where(i < s[nb + 1], f, nf - 1)

    grid_spec = pltpu.PrefetchScalarGridSpec(
        num_scalar_prefetch=1, grid=(nb, nf),
        in_specs=[pl.BlockSpec((blk, d), row_map),
                  pl.BlockSpec((None, None, d, tf), lambda i, f, s: (s[nb], s[i], 0, f_of(i, f, s))),
                  pl.BlockSpec((None, None, d, tf), lambda i, f, s: (s[nb], s[i], 0, f_of(i, f, s))),
                  pl.BlockSpec((None, None, tf, d), lambda i, f, s: (s[nb], s[i], f_of(i, f, s), 0))],
        out_specs=pl.BlockSpec((blk, d), lambda i, f, s: (i, 0)),
        scratch_shapes=[pltpu.VMEM((blk, d), BF16), pltpu.VMEM((blk, d), F32)])
    return pl.pallas_call(
        _expert_kernel,
        out_shape=jax.ShapeDtypeStruct((n_rows, d), F32),
        grid_spec=grid_spec,
        compiler_params=_cparams(("arbitrary", "arbitrary")),
        name="moe_experts",
    )(sidx, xb, wg, wu, wd)


def _combine_kernel(dest_ref, yb_ref, gate_ref, x1_ref, g2_ref, fg_ref, o_ref, buf_ref, sem, *, final):
    tm = x1_ref.shape[0]

    def start(r, carry):
        for k in range(TOP_K):
            _row_copy(yb_ref, dest_ref[0, TOP_K * r + k], buf_ref.at[k], r, sem).start()
        return carry

    lax.fori_loop(0, tm, start, 0, unroll=8)
    for k in range(TOP_K):
        pltpu.make_async_copy(yb_ref.at[pl.ds(0, tm), :], buf_ref.at[k], sem).wait()
    gate = gate_ref[...]
    f = gate[:, 0:1] * buf_ref[0] + gate[:, 1:2] * buf_ref[1]
    x2 = x1_ref[...] + g2_ref[...] * f
    o_ref[...] = _rms(x2) * fg_ref[...] if final else x2


def moe_combine(dest3, yb, gates, x1, mod, final_g, tm, final):
    g, r, d = x1.shape
    per_row = mod.shape[1] > 1
    nt = r // tm
    act = pl.BlockSpec((None, tm, d), lambda gi, i: (gi, i, 0))
    return pl.pallas_call(
        functools.partial(_combine_kernel, final=final),
        out_shape=jax.ShapeDtypeStruct((g, r, d), F32),
        grid=(g, nt),
        in_specs=[pl.BlockSpec((None, 1, TOP_K * tm), lambda gi, i: (gi * nt + i, 0, 0), memory_space=pltpu.SMEM),
                  pl.BlockSpec(memory_space=pl.ANY),
                  pl.BlockSpec((tm, 8), lambda gi, i: (gi * nt + i, 0)),
                  act,
                  (pl.BlockSpec((None, tm, d), lambda gi, i: (gi, i, 0)) if per_row
                   else pl.BlockSpec((None, 1, d), lambda gi, i: (gi, 0, 0))),
                  pl.BlockSpec((1, d), lambda gi, i: (0, 0))],
        out_specs=act,
        scratch_shapes=[pltpu.VMEM((TOP_K, tm, d), F32), pltpu.SemaphoreType.DMA(())],
        compiler_params=_cparams(("arbitrary", "arbitrary")),
        name="moe_combine",
    )(dest3, yb, gates, x1, mod, final_g)


def moe_ffn(layer_slot, h2, x1, mod_g2, router_pad, wg, wu, wd, final_g, tm, blk, tf, final):
    g, r, d = x1.shape
    t = g * r
    h2f = h2.reshape(t, d)
    iidx = jnp.array([layer_slot], I32)
    idx, gates, counts = moe_route(iidx, h2f, router_pad, _pick(t, 1024))
    counts = counts[0, :N_EXPERTS].astype(I32)
    padded = (counts + blk - 1) // blk * blk
    pad_end = jnp.cumsum(padded)
    pad_start = pad_end - padded
    first = sum(jnp.where(idx[:, 0:TOP_K] == e, pad_start[e], 0) for e in range(N_EXPERTS))
    dest = first + idx[:, TOP_K:2 * TOP_K]
    nb = -(-(t * TOP_K) // blk) + N_EXPERTS
    n_used = pad_end[-1] // blk
    blk_e = jnp.minimum(jnp.sum(jnp.arange(nb, dtype=I32)[:, None] * blk >= pad_end[None, :], axis=1), N_EXPERTS - 1)
    blk_e = jnp.where(jnp.arange(nb) < n_used, blk_e, blk_e[jnp.maximum(n_used - 1, 0)]).astype(I32)
    dest3 = dest.astype(I32).reshape(t // tm, 1, TOP_K * tm)
    fill = jnp.stack([jnp.append(pad_start + counts, pad_end[-1]),
                      jnp.append(pad_end, nb * blk)], axis=1).reshape(-1).astype(I32)
    xb = moe_dispatch(dest3, fill, h2f, nb * blk, tm)
    sidx = jnp.concatenate([blk_e, iidx, n_used.astype(I32)[None]])
    yb = moe_experts(sidx, xb, wg, wu, wd, blk, tf)
    return moe_combine(dest3, yb, gates, x1, mod_g2, final_g, tm, final)


def _attn_decode_kernel(pt_ref, lam_ref, lqk_ref, g_ref, q_ref, kn_ref, vn_ref, tab_ref, ck_ref, cv_ref, o_ref,
                        kbuf_ref, vbuf_ref, sem, m_ref, s_ref, acc_ref, *, pps):
    step = pl.program_id(1)
    n_steps = pl.num_programs(1)
    layer = pt_ref[pt_ref.shape[0] - 1]
    rows = 2 * H_A
    t = pl.program_id(0) * n_steps + step
    slot = lax.rem(t, 2)

    def start_pages(tt, sl):
        for i in range(pps):
            page = pt_ref[tt * pps + i]
            pltpu.make_async_copy(ck_ref.at[layer, page], kbuf_ref.at[sl, i], sem.at[0, sl]).start()
            pltpu.make_async_copy(cv_ref.at[layer, page], vbuf_ref.at[sl, i], sem.at[1, sl]).start()

    @pl.when(t == 0)
    def _():
        start_pages(0, 0)

    @pl.when(t + 1 < pl.num_programs(0) * n_steps)
    def _():
        start_pages(t + 1, 1 - slot)

    pltpu.make_async_copy(ck_ref.at[layer, pl.ds(0, pps)], kbuf_ref.at[slot], sem.at[0, slot]).wait()
    pltpu.make_async_copy(cv_ref.at[layer, pl.ds(0, pps)], vbuf_ref.at[slot], sem.at[1, slot]).wait()

    @pl.when(step == 0)
    def _():
        m_ref[...] = jnp.full(m_ref.shape, NEG, F32)
        s_ref[...] = jnp.zeros(s_ref.shape, F32)
        acc_ref[...] = jnp.zeros(acc_ref.shape, F32)

    qrow = jnp.broadcast_to(q_ref[...].astype(F32), (rows, W_A))
    own = (_iota((rows, W_A), 1) >> 6) == _iota((rows, W_A), 0)
    q8 = jnp.where(own, qrow, 0.0).astype(BF16)

    def accumulate(s, pv):
        m_old = m_ref[...]
        m_new = jnp.maximum(m_old, jnp.max(s, axis=1, keepdims=True))
        alpha = jnp.exp(m_old - m_new)
        p = jnp.exp(s - m_new)
        s_ref[...] = alpha * s_ref[...] + jnp.sum(p, axis=1, keepdims=True)
        acc_ref[...] = alpha * acc_ref[...] + pv(p)
        m_ref[...] = m_new

    scores = [_dotf(q8, kbuf_ref[slot, i].astype(BF16)) for i in range(pps)]
    scores[pps - 1] = jnp.where(step == n_steps - 1, scores[pps - 1] + tab_ref[0:rows, :], scores[pps - 1])
    s_all = jnp.concatenate(scores, axis=1)

    def value_page(i):
        heads = [vbuf_ref[slot, i, pl.ds(h, PAGE, stride=H_A), :] for h in range(H_A)]
        return jnp.concatenate(heads, axis=1).astype(BF16)

    def pv_pages(p):
        hi, lo = _split2(p)
        out = None
        for i in range(pps):
            v = value_page(i)
            part = _dotf(hi[:, i * PAGE:(i + 1) * PAGE], v) + _dotf(lo[:, i * PAGE:(i + 1) * PAGE], v)
            out = part if out is None else out + part
        return out

    accumulate(s_all, pv_pages)

    @pl.when(step == n_steps - 1)
    def _():
        kn = kn_ref[...].astype(BF16).astype(F32)
        vn = vn_ref[...].astype(BF16).astype(F32)
        s_new = jnp.sum(q8.astype(F32) * kn, axis=1, keepdims=True) + tab_ref[rows:2 * rows, 0:1]
        accumulate(s_new, lambda p: p * vn)
        o = acc_ref[...] * (1.0 / s_ref[...])
        lam = _lambda_value(lqk_ref[...], lam_ref[2 * layer])
        outs = []
        for h in range(H_A):
            blkh = o[:, h * DV_A:(h + 1) * DV_A]
            oh = blkh[2 * h:2 * h + 1, :] - lam * blkh[2 * h + 1:2 * h + 2, :]
            outs.append(_rms(oh) * g_ref[...] * lam_ref[2 * layer + 1])
        o_ref[...] = jnp.concatenate(outs, axis=1).astype(BF16)


def attention_decode(sidx, lam_consts, lambda_qk, subln_g3, q, k_new, v_new, tab, cache_k4, cache_v4, n_pages, pps):
    bd = q.shape[0]
    nl = sidx.shape[0] - 1
    assert n_pages % pps == 0
    row = pl.BlockSpec((None, 1, W_A), lambda b, s, pt: (b, 0, 0))
    grid_spec = pltpu.PrefetchScalarGridSpec(
        num_scalar_prefetch=1, grid=(bd, n_pages // pps),
        in_specs=[pl.BlockSpec(memory_space=pltpu.SMEM),
                  pl.BlockSpec((None, 4, DK_A), lambda b, s, pt: (pt[nl], 0, 0)),
                  pl.BlockSpec((None, 1, DV_A), lambda b, s, pt: (pt[nl], 0, 0)),
                  row, row, row,
                  pl.BlockSpec((16, PAGE), lambda b, s, pt: (0, 0)),
                  pl.BlockSpec(memory_space=pl.ANY), pl.BlockSpec(memory_space=pl.ANY)],
        out_specs=row,
        scratch_shapes=[pltpu.VMEM((2, pps, W_A, PAGE), F32), pltpu.VMEM((2, pps, H_A * PAGE, DV_A), F32),
                        pltpu.SemaphoreType.DMA((2, 2)),
                        pltpu.VMEM((2 * H_A, 1), F32), pltpu.VMEM((2 * H_A, 1), F32),
                        pltpu.VMEM((2 * H_A, W_A), F32)])
    return pl.pallas_call(
        functools.partial(_attn_decode_kernel, pps=pps),
        out_shape=jax.ShapeDtypeStruct((bd, 1, W_A), BF16),
        grid_spec=grid_spec,
        compiler_params=_cparams(("arbitrary", "arbitrary")),
        name="attention_decode",
    )(sidx, lam_consts, lambda_qk, subln_g3, q, k_new, v_new, tab, cache_k4, cache_v4)


def _bc_decode_pre_kernel(l_ref, rest_ref, buf_ref, convw_ref, convb_ref, v128_ref, wlr_ref,
                          xbc_ref, nbuf_ref, dt_ref, dec_ref, eg_ref):
    del l_ref
    u = rest_ref[:, R_BX:R_BX + XBC_B]
    conv = convb_ref[...] + convw_ref[CONV_W - 1:CONV_W, :] * u
    for j in range(CONV_W - 1):
        conv = conv + convw_ref[j:j + 1, :] * buf_ref[j]
    xbc_ref[...] = _silu(conv)
    for j in range(CONV_W - 2):
        nbuf_ref[j] = buf_ref[j + 1]
    nbuf_ref[CONV_W - 2] = u
    misc = rest_ref[:, R_MISC:R_MISC + 128]
    dt = _softplus(misc + v128_ref[0:1, :])
    dt_ref[...] = dt
    dec_ref[...] = jnp.exp(dt * (-jnp.exp(v128_ref[1:2, :])))
    gk = _dotf(misc.astype(BF16), wlr_ref[...]) + v128_ref[2:3, :]
    eg_ref[...] = jnp.exp(-_softplus(-gk) * (1.0 / GATE_NORM))


def bc_decode_pre(lidx, rest, conv_state_t, convw, convb3, vec128, wlr):
    bd = rest.shape[0]
    par = lambda r, w: pl.BlockSpec((None, r, w), lambda i, l: (l[0], 0, 0))
    full = lambda w: pl.BlockSpec((bd, w), lambda i, l: (0, 0))
    grid_spec = pltpu.PrefetchScalarGridSpec(
        num_scalar_prefetch=1, grid=(1,),
        in_specs=[full(W_REST),
                  pl.BlockSpec((None, CONV_W - 1, bd, XBC_B), lambda i, l: (l[0], 0, 0, 0)),
                  par(CONV_W, XBC_B), par(1, XBC_B), par(8, 128), par(128, 128)],
        out_specs=[full(XBC_B), pl.BlockSpec((CONV_W - 1, bd, XBC_B), lambda i, l: (0, 0, 0)),
                   full(128), full(128), full(128)])
    return pl.pallas_call(
        _bc_decode_pre_kernel,
        out_shape=[jax.ShapeDtypeStruct((bd, XBC_B), F32), jax.ShapeDtypeStruct((CONV_W - 1, bd, XBC_B), F32),
                   jax.ShapeDtypeStruct((bd, 128), F32), jax.ShapeDtypeStruct((bd, 128), F32),
                   jax.ShapeDtypeStruct((bd, 128), F32)],
        grid_spec=grid_spec,
        compiler_params=_cparams(("arbitrary",)),
        name="bc_decode_pre",
    )(lidx, rest, conv_state_t, convw, convb3, vec128, wlr)


def _bc_decode_state_kernel(l_ref, ssm_ref, gla_ref, x4_ref, b4_ref, c4_ref, dt4_ref, dec4_ref, bz4_ref,
                            dsk_ref, ng_ref, q4_ref, k4_ref, v4_ref, eg4_ref, cg4_ref, gng_ref,
                            ssm_out_ref, gla_out_ref, ob4_ref, oc4_ref):
    del l_ref
    x4 = x4_ref[...]
    s = ssm_ref[...] * dec4_ref[...] + (dt4_ref[...] * x4) * b4_ref[...]
    ssm_out_ref[...] = s
    y = jnp.sum(c4_ref[...] * s, axis=3, keepdims=True) + dsk_ref[...] * x4
    yg = y * _silu(bz4_ref[...])
    ms = jnp.sum(jnp.sum(yg * yg, axis=2, keepdims=True), axis=1, keepdims=True) * (1.0 / W_B)
    ob4_ref[...] = yg * lax.rsqrt(ms + EPS) * ng_ref[...]
    sg = gla_ref[...] * eg4_ref[...] + k4_ref[...] * v4_ref[...]
    gla_out_ref[...] = sg
    o = jnp.sum(q4_ref[...] * sg, axis=2, keepdims=True)
    oc4_ref[...] = _rms(o) * gng_ref[...] * _silu(cg4_ref[...])


def bc_decode_state(lidx, state_ssm, state_gla, x4, b4, c4, dt4, dec4, bz4, dsk4, ng4, q4, k4, v4, eg4, cg4, gng4):
    bd = x4.shape[0]

    def full(shape):
        n = len(shape)
        return pl.BlockSpec(shape, lambda i, l: (0,) * n)

    def layer(shape):
        n = len(shape)
        return pl.BlockSpec((None,) + shape, lambda i, l: (l[0],) + (0,) * n)

    s_ssm = (bd, H_B, P_B, N_B)
    s_gla = (bd, H_C, DK_C, DV_C)
    col_b = (bd, H_B, P_B, 1)
    row_b = (bd, H_B, 1, N_B)
    one_b = (bd, H_B, 1, 1)
    col_c = (bd, H_C, DK_C, 1)
    row_c = (bd, H_C, 1, DV_C)
    grid_spec = pltpu.PrefetchScalarGridSpec(
        num_scalar_prefetch=1, grid=(1,),
        in_specs=[layer(s_ssm), layer(s_gla), full(col_b), full(row_b), full(row_b), full(one_b), full(one_b),
                  full(col_b), layer((1, H_B, 1, 1)), layer((1, H_B, P_B, 1)),
                  full(col_c), full(col_c), full(row_c), full(col_c), full(row_c), layer((1, 1, 1, DV_C))],
        out_specs=[full(s_ssm), full(s_gla), full(col_b), full(row_c)])
    return pl.pallas_call(
        _bc_decode_state_kernel,
        out_shape=[jax.ShapeDtypeStruct(s_ssm, F32), jax.ShapeDtypeStruct(s_gla, F32),
                   jax.ShapeDtypeStruct(col_b, F32), jax.ShapeDtypeStruct(row_c, F32)],
        grid_spec=grid_spec,
        compiler_params=_cparams(("arbitrary",)),
        name="bc_decode_state",
    )(lidx, state_ssm, state_gla, x4, b4, c4, dt4, dec4, bz4, dsk4, ng4, q4, k4, v4, eg4, cg4, gng4)


def _pack_params(w_in, rel_bias, subln_g, conv_b, dt_bias, a_log, d_skip, ssd_norm_g, gla_w_lr, gla_b_lr,
                 gla_norm_g, router_w):
    depth = w_in.shape[0]
    pad = jnp.zeros(w_in.shape[:2] + (D_PACK - D_IN,), w_in.dtype)
    w_pack = jnp.concatenate([w_in[..., :OFF_BDT], w_in[..., OFF_CQ:OFF_CLR], w_in[..., OFF_BDT:OFF_CQ],
                              w_in[..., OFF_CLR:], pad], axis=-1).astype(BF16)
    vec128 = jnp.zeros((depth, 8, 128), F32)
    vec128 = vec128.at[:, 0, :H_B].set(dt_bias).at[:, 1, :H_B].set(a_log).at[:, 2, :].set(gla_b_lr)
    vec256 = jnp.zeros((depth, 8, 256), F32)
    vec256 = (vec256.at[:, 0, :].set(jnp.repeat(d_skip, P_B, axis=1)).at[:, 1, :].set(ssd_norm_g)
              .at[:, 2, :].set(jnp.tile(gla_norm_g, (1, H_C))))
    wlr = jnp.zeros((depth, 128, 128), F32).at[:, MISC_LR:MISC_LR + GATE_RANK, :].set(gla_w_lr).astype(BF16)
    router_pad = jnp.zeros(router_w.shape[:2] + (128,), F32).at[..., :N_EXPERTS].set(router_w)
    lam_init = [0.8 - 0.6 * math.exp(-0.3 * l) for l in range(depth)]
    lam_consts = jnp.asarray(np.array([[li, 1.0 - li] for li in lam_init], np.float32).reshape(-1))
    return dict(w_pack=w_pack, vec128=vec128, vec256=vec256, wlr=wlr, router_pad=router_pad,
                lam_consts=lam_consts, subln_g3=subln_g.reshape(depth, 1, DV_A),
                convb3=conv_b.reshape(depth, 1, XBC_B))


def _pick(n, pref):
    if n <= pref:
        return n
    t = pref
    while n % t:
        t //= 2
    return t


def _channel_mixer(l, depth, h2, x1, mod, pk, wts, tm, moe_blk, final):
    i = l // 2
    lmod = jnp.array([l], I32)
    if l % 2 == 0:
        return dense_ffn(jnp.array([i], I32), lmod, h2, wts['ffn_g'], wts['ffn_u'], wts['ffn_d'], x1, mod,
                         wts['final_g'], tm, FF_TILE, final)
    mod_g2 = mod[l, :, :, 5 * D_MODEL:6 * D_MODEL]
    return moe_ffn(i, h2, x1, mod_g2, pk['router_pad'], wts['moe_g'], wts['moe_u'], wts['moe_d'], wts['final_g'],
                   _pick(x1.shape[0] * x1.shape[1], 512), moe_blk, FF_TILE, final)


def _run_prompt(x, mod, pk, wts, lambda_qk, conv_w, rel_bias):
    b, seq, d = x.shape
    depth = mod.shape[0]
    tm = _pick(seq // 2, 512)
    bias_tiles = attention_bias_tiles(rel_bias, tm)
    k_all = jnp.zeros((depth, b, W_A, seq), F32)
    v_all = jnp.zeros((depth, b, H_A * seq, DV_A), F32)
    convs, ssms, glas = [], [], []
    for l in range(depth):
        lidx = jnp.array([l], I32)
        q, rest, kt, vb, k_all, v_all = in_projection_prompt(lidx, x, mod, pk['w_pack'], k_all, v_all, tm, 2 * tm)
        oa = attention_prompt(lidx, pk['lam_consts'], lambda_qk, pk['subln_g3'], q, kt, vb, bias_tiles, tm)
        ob, oc, conv_s, ssm_s, gla_s = bc_mixers_prompt(lidx, rest, conv_w, pk['convb3'], pk['vec128'],
                                                        pk['vec256'], pk['wlr'])
        moe = l % 2 == 1
        x1, h2 = out_projection(lidx, oa, ob, oc, wts['w_out'], x, mod, tm, F32 if moe else BF16)
        x = _channel_mixer(l, depth, h2, x1, mod, pk, wts, tm, _pick(b * seq, 512), l == depth - 1)
        convs.append(conv_s)
        ssms.append(_unpack_ssm_state(ssm_s))
        glas.append(_unpack_gla_state(gla_s))
    ks = jnp.transpose(k_all.reshape(depth, b, 2 * H_A, DK_A, seq), (0, 1, 4, 2, 3))
    vs = v_all.reshape(depth, b, seq, H_A, DV_A)
    return x, ks, vs, jnp.stack(convs), jnp.stack(ssms), jnp.stack(glas)


def _run_decode(x, mod, pk, wts, lambda_qk, conv_w, rel_bias, cache_k, cache_v, page_table,
                state_conv, state_ssm, state_gla, d_skip, ssd_norm_g, gla_norm_g):
    bd = x.shape[0]
    depth = mod.shape[0]
    n_pages = page_table.shape[1]
    past = n_pages * PAGE
    pps = _pick(n_pages, 32)
    xg = x.reshape(1, bd, D_MODEL)
    pool = cache_k.shape[1]
    assert PAGE >= T5_MAX_DIST
    cache_k4 = jnp.transpose(cache_k, (0, 1, 3, 4, 2)).reshape(depth, pool, 2 * H_A * DK_A, PAGE)
    cache_v4 = cache_v.reshape(depth, pool, PAGE * H_A, DV_A)
    table = _t5_table(rel_bias, PAGE + 1)
    dist = past - ((n_pages - 1) * PAGE + jnp.arange(PAGE))
    tab = jnp.concatenate([jnp.repeat(table[:, dist], 2, axis=0),
                           jnp.broadcast_to(jnp.repeat(table[:, 0], 2)[:, None], (2 * H_A, PAGE))], axis=0)
    conv_t = jnp.swapaxes(state_conv, 1, 2)
    grp = np.arange(H_B) // (H_B // G_B)
    dsk4 = d_skip.reshape(depth, 1, H_B, 1, 1)
    ng4 = ssd_norm_g.reshape(depth, 1, H_B, P_B, 1)
    gng4 = gla_norm_g.reshape(depth, 1, 1, 1, DV_C)
    pt_flat = page_table.reshape(-1).astype(I32)
    ks, vs, convs, ssms, glas = [], [], [], [], []
    for l in range(depth):
        lidx = jnp.array([l], I32)
        q, k, v, rest = in_projection(lidx, xg, mod, pk['w_pack'], bd)
        sidx = jnp.concatenate([pt_flat, lidx])
        oa = attention_decode(sidx, pk['lam_consts'], lambda_qk, pk['subln_g3'], q.reshape(bd, 1, W_A),
                              k.reshape(bd, 1, W_A), v.reshape(bd, 1, W_A), tab, cache_k4, cache_v4, n_pages, pps)
        rest2 = rest.reshape(bd, W_REST)
        xbc, nbuf, dt, dec, eg = bc_decode_pre(lidx, rest2, conv_t, conv_w, pk['convb3'], pk['vec128'], pk['wlr'])
        x4 = xbc[:, :W_B].reshape(bd, H_B, P_B, 1)
        b4 = xbc[:, W_B:W_B + G_B * N_B].reshape(bd, G_B, 1, N_B)[:, grp]
        c4 = xbc[:, W_B + G_B * N_B:].reshape(bd, G_B, 1, N_B)[:, grp]
        dt4 = dt[:, :H_B].reshape(bd, H_B, 1, 1)
        dec4 = dec[:, :H_B].reshape(bd, H_B, 1, 1)
        bz4 = rest2[:, R_BZ:R_BZ + W_B].reshape(bd, H_B, P_B, 1)
        q4 = (rest2[:, R_CQ:R_CQ + 128] * (DK_C ** -0.5)).reshape(bd, H_C, DK_C, 1)
        k4 = rest2[:, R_CK:R_CK + 128].reshape(bd, H_C, DK_C, 1)
        v4 = rest2[:, R_CV:R_CV + W_C].reshape(bd, H_C, 1, DV_C)
        eg4 = eg.reshape(bd, H_C, DK_C, 1)
        cg4 = rest2[:, R_CG:R_CG + W_C].reshape(bd, H_C, 1, DV_C)
        ssm_s, gla_s, ob4, oc4 = bc_decode_state(lidx, state_ssm, state_gla, x4, b4, c4, dt4, dec4, bz4, dsk4, ng4,
                                                 q4, k4, v4, eg4, cg4, gng4)
        ob = ob4.reshape(1, bd, W_B).astype(BF16)
        oc = oc4.reshape(1, bd, W_C).astype(BF16)
        moe = l % 2 == 1
        x1, h2 = out_projection(lidx, oa.reshape(1, bd, W_A), ob, oc, wts['w_out'], xg, mod, bd, F32 if moe else BF16)
        xg = _channel_mixer(l, depth, h2, x1, mod, pk, wts, bd, 64, l == depth - 1)
        ks.append(k.reshape(bd, 1, 2 * H_A, DK_A))
        vs.append(v.reshape(bd, 1, H_A, DV_A))
        convs.append(jnp.swapaxes(nbuf, 0, 1))
        ssms.append(ssm_s)
        glas.append(gla_s)
    return (xg.reshape(bd, 1, D_MODEL), jnp.stack(ks), jnp.stack(vs), jnp.stack(convs), jnp.stack(ssms),
            jnp.stack(glas))


def kernel(x_prompt, x_sample, c_prompt, c_sample, cache_k, cache_v, page_table, state_conv, state_ssm, state_gla,
           w_ada, b_ada, w_in, w_out, rel_bias, lambda_qk, subln_g, conv_w, conv_b, dt_bias, a_log, d_skip,
           ssd_norm_g, gla_w_lr, gla_b_lr, gla_norm_g, ffn_w_gate, ffn_w_up, ffn_w_down, router_w,
           moe_w_gate, moe_w_up, moe_w_down, final_norm_g):
    depth = w_in.shape[0]
    bp, bd = c_prompt.shape[0], c_sample.shape[0]
    pk = _pack_params(w_in, rel_bias, subln_g, conv_b, dt_bias, a_log, d_skip, ssd_norm_g, gla_w_lr, gla_b_lr,
                      gla_norm_g, router_w)
    wts = dict(w_out=w_out.astype(BF16), ffn_g=ffn_w_gate.astype(BF16), ffn_u=ffn_w_up.astype(BF16),
               ffn_d=ffn_w_down.astype(BF16), moe_g=moe_w_gate.astype(BF16), moe_u=moe_w_up.astype(BF16),
               moe_d=moe_w_down.astype(BF16), final_g=final_norm_g.reshape(1, D_MODEL))
    n_c = bp + bd
    n_cp = -(-n_c // 8) * 8
    c_all = jnp.concatenate([c_prompt, c_sample, jnp.zeros((n_cp - n_c, D_MODEL), F32)], axis=0)
    mod = ada_modulation(c_all, w_ada, b_ada)
    mod_p = mod[:, :bp].reshape(depth, bp, 1, 6 * D_MODEL)
    mod_d = mod[:, bp:n_c].reshape(depth, 1, bd, 6 * D_MODEL)
    yp, kp, vp, convp, ssmp, glap = _run_prompt(x_prompt, mod_p, pk, wts, lambda_qk, conv_w, rel_bias)
    yd, kd, vd, convd, ssmd, glad = _run_decode(x_sample, mod_d, pk, wts, lambda_qk, conv_w, rel_bias, cache_k,
                                                cache_v, page_table, state_conv, state_ssm, state_gla, d_skip,
                                                ssd_norm_g, gla_norm_g)
    return (yp, yd, kp, vp, convp, ssmp, glap, kd, vd, convd, ssmd, glad)
```

```python
import functools
import math

import numpy as np
import jax
import jax.numpy as jnp
from jax import lax
from jax.experimental import pallas as pl
from jax.experimental.pallas import tpu as pltpu

F32 = jnp.float32
BF16 = jnp.bfloat16
I32 = jnp.int32

D_MODEL = 1024
H_A, DK_A, DV_A = 4, 64, 128
W_A = H_A * DV_A
H_B, P_B, G_B, N_B, CONV_W = 4, 64, 2, 64, 4
W_B = H_B * P_B
XBC_B = W_B + 2 * G_B * N_B
H_C, DK_C, DV_C = 4, 32, 64
W_C = H_C * DV_C
GATE_RANK, GATE_NORM = 16, 16.0
T5_BUCKETS, T5_MAX_DIST = 32, 128
D_FF, N_EXPERTS, TOP_K = 2816, 8, 2
EPS = 1e-6
PAGE = 128
CHUNK = 128
FF_TILE = D_FF
OFF_AQ = 0
OFF_AK = OFF_AQ + 2 * H_A * DK_A
OFF_AV = OFF_AK + 2 * H_A * DK_A
OFF_BZ = OFF_AV + W_A
OFF_BX = OFF_BZ + W_B
OFF_BDT = OFF_BX + XBC_B
OFF_CQ = OFF_BDT + H_B
OFF_CK = OFF_CQ + H_C * DK_C
OFF_CV = OFF_CK + H_C * DK_C
OFF_CG = OFF_CV + W_C
OFF_CLR = OFF_CG + W_C
D_IN = OFF_CLR + GATE_RANK
PK_REST = 3 * W_A
R_BZ, R_BX, R_CQ, R_CK, R_CV, R_CG, R_MISC = 0, 256, 768, 896, 1024, 1280, 1536
W_REST = R_MISC + 128
D_PACK = PK_REST + W_REST
MISC_DT, MISC_LR = 0, H_B
NEG = -1e30
VMEM_LIMIT = 56 * 1024 * 1024


def _cparams(sem):
    return pltpu.CompilerParams(dimension_semantics=sem, vmem_limit_bytes=VMEM_LIMIT)


def _rms(x):
    return x * lax.rsqrt(jnp.mean(x * x, axis=-1, keepdims=True) + EPS)


def _silu(x):
    return x * jax.nn.sigmoid(x)


def _softplus(x):
    return jnp.maximum(x, 0.0) + jnp.log1p(jnp.exp(-jnp.abs(x)))


def _split2(x):
    hi = x.astype(BF16)
    return hi, (x - hi.astype(F32)).astype(BF16)


def _split3(x):
    hi = x.astype(BF16)
    r = x - hi.astype(F32)
    mid = r.astype(BF16)
    return hi, mid, (r - mid.astype(F32)).astype(BF16)


def _dotf(a, b):
    return jnp.dot(a, b, preferred_element_type=F32)


def _lmat_exact(mat01, x):
    hi, mid, lo = _split3(x)
    return _dotf(mat01, hi) + _dotf(mat01, mid) + _dotf(mat01, lo)


def _rmat_exact(x, mat01, pieces=3):
    if pieces == 2:
        hi, lo = _split2(x)
        return _dotf(hi, mat01) + _dotf(lo, mat01)
    hi, mid, lo = _split3(x)
    return _dotf(hi, mat01) + _dotf(mid, mat01) + _dotf(lo, mat01)


def _iota(shape, axis):
    return lax.broadcasted_iota(I32, shape, axis)


def _ada_kernel(c_ref, w_ref, b_ref, o_ref):
    c = c_ref[...]
    o_ref[...] = _dotf(_silu(c).astype(BF16), w_ref[...].astype(BF16)) + b_ref[...]


def ada_modulation(c_all, w_ada, b_ada):
    depth, d, n = w_ada.shape
    bc = c_all.shape[0]
    tn = 1536 if n % 1536 == 0 else n
    return pl.pallas_call(
        _ada_kernel,
        out_shape=jax.ShapeDtypeStruct((depth, bc, n), F32),
        grid=(depth, n // tn),
        in_specs=[pl.BlockSpec((bc, d), lambda l, j: (0, 0)),
                  pl.BlockSpec((None, d, tn), lambda l, j: (l, 0, j)),
                  pl.BlockSpec((None, 1, tn), lambda l, j: (l, 0, j))],
        out_specs=pl.BlockSpec((None, bc, tn), lambda l, j: (l, 0, j)),
        compiler_params=_cparams(("arbitrary", "arbitrary")),
        name="ada_modulation",
    )(c_all, w_ada, b_ada.reshape(depth, 1, n))


def _mod_spec(tm, per_row, chunk):
    if per_row:
        return pl.BlockSpec((None, None, tm, D_MODEL), lambda g, i, l: (l[0], g, i, chunk))
    return pl.BlockSpec((None, None, 1, D_MODEL), lambda g, i, l: (l[0], g, 0, chunk))


def _in_proj_kernel(l_ref, x_ref, sh_ref, sc_ref, w_ref, q_ref, k_ref, v_ref, r_ref):
    del l_ref
    h = (_rms(x_ref[...]) * (1.0 + sc_ref[...]) + sh_ref[...]).astype(BF16)
    q_ref[...] = (_dotf(h, w_ref[:, 0:W_A]) * (DK_A ** -0.5)).astype(BF16)
    k_ref[...] = _dotf(h, w_ref[:, W_A:2 * W_A])
    v_ref[...] = _dotf(h, w_ref[:, 2 * W_A:3 * W_A])
    r_ref[...] = _dotf(h, w_ref[:, PK_REST:D_PACK])


def in_projection(lidx, x, mod, w_pack, tm):
    g, r, d = x.shape
    per_row = mod.shape[2] > 1
    row = lambda w, dt: jax.ShapeDtypeStruct((g, r, w), dt)
    ospec = lambda w: pl.BlockSpec((None, tm, w), lambda gi, i, l: (gi, i, 0))
    out_specs = [ospec(W_A), ospec(W_A), ospec(W_A), ospec(W_REST)]
    out_shape = [row(W_A, BF16), row(W_A, F32), row(W_A, F32), row(W_REST, F32)]
    grid_spec = pltpu.PrefetchScalarGridSpec(
        num_scalar_prefetch=1, grid=(g, r // tm),
        in_specs=[pl.BlockSpec((None, tm, d), lambda gi, i, l: (gi, i, 0)),
                  _mod_spec(tm, per_row, 0), _mod_spec(tm, per_row, 1),
                  pl.BlockSpec((None, d, D_PACK), lambda gi, i, l: (l[0], 0, 0))],
        out_specs=out_specs)
    return pl.pallas_call(
        _in_proj_kernel,
        out_shape=out_shape,
        grid_spec=grid_spec,
        compiler_params=_cparams(("arbitrary", "arbitrary")),
        name="in_projection",
    )(lidx, x, mod, mod, w_pack)


def _in_proj_prompt_kernel(l_ref, x_ref, sh_ref, sc_ref, w_ref, kall_in_ref, vall_in_ref,
                           q_ref, r_ref, kt_ref, vb_ref, kall_ref, vall_ref, k_scr):
    del l_ref, kall_in_ref, vall_in_ref
    tm = x_ref.shape[0]
    h = (_rms(x_ref[...]) * (1.0 + sc_ref[...]) + sh_ref[...]).astype(BF16)
    q_ref[...] = (_dotf(h, w_ref[:, 0:W_A]) * (DK_A ** -0.5)).astype(BF16)
    k_scr[...] = _dotf(h, w_ref[:, W_A:2 * W_A])
    kt = k_scr[...].T
    kall_ref[...] = kt
    kt_ref[...] = kt.astype(BF16)
    v = _dotf(h, w_ref[:, 2 * W_A:3 * W_A])
    vb_ref[...] = v.astype(BF16)
    for head in range(H_A):
        vall_ref[pl.ds(head, tm, stride=H_A), :] = v[:, head * DV_A:(head + 1) * DV_A]
    r_ref[...] = _dotf(h, w_ref[:, PK_REST:D_PACK])


def in_projection_prompt(lidx, x, mod, w_pack, k_all, v_all, tm, kblk):
    b, seq, d = x.shape
    sub = kblk // tm
    act = lambda w: pl.BlockSpec((None, tm, w), lambda bi, i, l: (bi, i, 0))
    grid_spec = pltpu.PrefetchScalarGridSpec(
        num_scalar_prefetch=1, grid=(b, seq // tm),
        in_specs=[act(d), _mod_spec(tm, False, 0), _mod_spec(tm, False, 1),
                  pl.BlockSpec((None, d, D_PACK), lambda bi, i, l: (l[0], 0, 0)),
                  pl.BlockSpec(memory_space=pl.ANY), pl.BlockSpec(memory_space=pl.ANY)],
        out_specs=[act(W_A), act(W_REST),
                   pl.BlockSpec((None, None, W_A, tm), lambda bi, i, l: (bi, i // sub, 0, i % sub)),
                   act(W_A),
                   pl.BlockSpec((None, None, W_A, tm), lambda bi, i, l: (l[0], bi, 0, i)),
                   pl.BlockSpec((None, None, H_A * tm, DV_A), lambda bi, i, l: (l[0], bi, i, 0))],
        scratch_shapes=[pltpu.VMEM((tm, W_A), F32)])
    return pl.pallas_call(
        _in_proj_prompt_kernel,
        out_shape=[jax.ShapeDtypeStruct((b, seq, W_A), BF16), jax.ShapeDtypeStruct((b, seq, W_REST), F32),
                   jax.ShapeDtypeStruct((b, seq // kblk, W_A, kblk), BF16), jax.ShapeDtypeStruct((b, seq, W_A), BF16),
                   jax.ShapeDtypeStruct(k_all.shape, F32), jax.ShapeDtypeStruct(v_all.shape, F32)],
        grid_spec=grid_spec,
        input_output_aliases={5: 4, 6: 5},
        compiler_params=_cparams(("arbitrary", "arbitrary")),
        name="in_projection_prompt",
    )(lidx, x, mod, mod, w_pack, k_all, v_all)


def _lambda_value(lqk, lam_init):
    s01 = jnp.sum(lqk[0:1, :] * lqk[1:2, :], axis=1, keepdims=True)
    s23 = jnp.sum(lqk[2:3, :] * lqk[3:4, :], axis=1, keepdims=True)
    return jnp.exp(s01) - jnp.exp(s23) + lam_init


def _attn_prompt_kernel(l_ref, lam_ref, lqk_ref, g_ref, q_ref, kt_ref, v_ref, bias_ref, o_ref,
                        vaug_ref, m_ref, acc_ref, *, blk):
    qi = pl.program_id(2)

    @pl.when(qi == 0)
    def _():
        vaug_ref[:, 0:DV_A] = v_ref[...]
        ones_col = _iota((vaug_ref.shape[0], DV_A), 1) == 0
        vaug_ref[:, DV_A:2 * DV_A] = jnp.where(ones_col, 1.0, 0.0).astype(BF16)

    q = q_ref[...]
    lane = _iota(q.shape, 1)
    zero = jnp.zeros_like(q)
    qm = (jnp.where(lane < DK_A, q, zero), jnp.where(lane >= DK_A, q, zero))
    m_ref[...] = jnp.full(m_ref.shape, NEG, F32)
    acc_ref[...] = jnp.zeros(acc_ref.shape, F32)

    def update(kt, va, bias):
        for m in range(2):
            s = _dotf(qm[m], kt)
            if bias is not None:
                pieces = [s[:, i * blk:(i + 1) * blk] if b is None else s[:, i * blk:(i + 1) * blk] + b
                          for i, b in enumerate(bias)]
                s = pieces[0] if len(pieces) == 1 else jnp.concatenate(pieces, axis=1)
            m_old = m_ref[m]
            m_new = jnp.maximum(m_old, jnp.broadcast_to(jnp.max(s, axis=1, keepdims=True), m_old.shape))
            p = jnp.exp(s - jnp.tile(m_new, (1, s.shape[1] // 128))).astype(BF16)
            acc_ref[m] = jnp.tile(jnp.exp(m_old - m_new), (1, 2)) * acc_ref[m] + _dotf(p, va)
            m_ref[m] = m_new

    def key_block(j):
        return kt_ref[j], vaug_ref[pl.ds(pl.multiple_of(j * 2 * blk, 2 * blk), 2 * blk), :]

    def far(j, carry):
        update(*key_block(j), None)
        return carry

    a = qi >> 1
    odd = (qi & 1) == 1
    lax.fori_loop(0, jnp.where(odd, a, jnp.maximum(a - 1, 0)), far, 0)

    @pl.when(odd)
    def _():
        update(*key_block(a), (bias_ref[1], bias_ref[0]))

    @pl.when(jnp.logical_and(jnp.logical_not(odd), a >= 1))
    def _():
        update(*key_block(a - 1), (None, bias_ref[1]))

    @pl.when(jnp.logical_not(odd))
    def _():
        start = pl.multiple_of(a * 2 * blk, 2 * blk)
        update(kt_ref[a, :, 0:blk], vaug_ref[pl.ds(start, blk), :], (bias_ref[0],))

    layer = l_ref[0]
    lam = _lambda_value(lqk_ref[...], lam_ref[2 * layer])
    acc0 = acc_ref[0]
    acc1 = acc_ref[1]
    o = (acc0[:, 0:DV_A] * (1.0 / acc0[:, DV_A:DV_A + 1])
         - lam * (acc1[:, 0:DV_A] * (1.0 / acc1[:, DV_A:DV_A + 1])))
    o_ref[...] = (_rms(o) * g_ref[...] * lam_ref[2 * layer + 1]).astype(BF16)


def _t5_table(rel_bias, n):
    dist = jnp.arange(n)
    max_exact = T5_BUCKETS // 2
    nf = jnp.maximum(dist, 1).astype(F32)
    large = max_exact + (jnp.log(nf / max_exact) / math.log(T5_MAX_DIST / max_exact)
                         * (T5_BUCKETS - max_exact)).astype(I32)
    bucket = jnp.where(dist < max_exact, dist, jnp.minimum(large, T5_BUCKETS - 1))
    return (rel_bias[bucket] - rel_bias[T5_BUCKETS - 1][None, :]).T.astype(F32)


def _bias_tiles_kernel(rel_ref, o_ref):
    h = pl.program_id(0)
    blk = o_ref.shape[0]
    d = _iota((blk, blk), 0) - _iota((blk, blk), 1) + pl.program_id(1) * blk
    dist = jnp.maximum(d, 0)
    max_exact = T5_BUCKETS // 2
    nf = jnp.maximum(dist, 1).astype(F32)
    large = max_exact + (jnp.log(nf / max_exact) / math.log(T5_MAX_DIST / max_exact)
                         * (T5_BUCKETS - max_exact)).astype(I32)
    bucket = jnp.where(dist < max_exact, dist, jnp.minimum(large, T5_BUCKETS - 1))
    sat = rel_ref[(T5_BUCKETS - 1) * H_A + h]
    val = jnp.zeros((blk, blk), F32)
    for b in range(T5_BUCKETS - 1):
        val = jnp.where(bucket == b, rel_ref[b * H_A + h] - sat, val)
    o_ref[...] = jnp.where(d >= 0, val, NEG)


def attention_bias_tiles(rel_bias, blk):
    return pl.pallas_call(
        _bias_tiles_kernel,
        out_shape=jax.ShapeDtypeStruct((H_A, 2, blk, blk), F32),
        grid=(H_A, 2),
        in_specs=[pl.BlockSpec(memory_space=pltpu.SMEM)],
        out_specs=pl.BlockSpec((None, None, blk, blk), lambda h, t: (h, t, 0, 0)),
        compiler_params=_cparams(("arbitrary", "arbitrary")),
        name="attention_bias_tiles",
    )(rel_bias.reshape(-1))


def attention_prompt(lidx, lam_consts, lambda_qk, subln_g3, q, kt, vb, bias_tiles, blk):
    b, seq, _ = q.shape
    assert blk >= T5_MAX_DIST and blk % 128 == 0 and kt.shape == (b, seq // (2 * blk), W_A, 2 * blk)
    grid_spec = pltpu.PrefetchScalarGridSpec(
        num_scalar_prefetch=1, grid=(b, H_A, seq // blk),
        in_specs=[pl.BlockSpec(memory_space=pltpu.SMEM),
                  pl.BlockSpec((None, 4, DK_A), lambda bi, h, i, l: (l[0], 0, 0)),
                  pl.BlockSpec((None, 1, DV_A), lambda bi, h, i, l: (l[0], 0, 0)),
                  pl.BlockSpec((None, blk, DV_A), lambda bi, h, i, l: (bi, i, h)),
                  pl.BlockSpec((None, seq // (2 * blk), DV_A, 2 * blk), lambda bi, h, i, l: (bi, 0, h, 0)),
                  pl.BlockSpec((None, seq, DV_A), lambda bi, h, i, l: (bi, 0, h)),
                  pl.BlockSpec((None, 2, blk, blk), lambda bi, h, i, l: (h, 0, 0, 0))],
        out_specs=pl.BlockSpec((None, blk, DV_A), lambda bi, h, i, l: (bi, i, h)),
        scratch_shapes=[pltpu.VMEM((seq, 2 * DV_A), BF16), pltpu.VMEM((2, blk, 128), F32),
                        pltpu.VMEM((2, blk, 2 * DV_A), F32)])
    return pl.pallas_call(
        functools.partial(_attn_prompt_kernel, blk=blk),
        out_shape=jax.ShapeDtypeStruct((b, seq, W_A), BF16),
        grid_spec=grid_spec,
        compiler_params=_cparams(("arbitrary", "arbitrary", "arbitrary")),
        name="attention_prompt",
    )(lidx, lam_consts, lambda_qk, subln_g3, q, kt, vb, bias_tiles)


def _bc_prompt_kernel(l_ref, rest_ref, convw_ref, convb_ref, v128_ref, v256_ref, wlr_ref,
                      ob_ref, oc_ref, conv_out_ref, ssm_out_ref, gla_out_ref,
                      xpad_ref, sbd_ref, sg_ref, b_ref, oi_ref, *, nsub):
    del l_ref
    c = pl.program_id(1)
    q = CHUNK
    rows = nsub * q

    @pl.when(c == 0)
    def _():
        xpad_ref[0:8, :] = jnp.zeros((8, XBC_B), F32)
        sbd_ref[...] = jnp.zeros(sbd_ref.shape, F32)
        sg_ref[...] = jnp.zeros(sg_ref.shape, F32)

    xpad_ref[8:8 + rows, :] = rest_ref[:, R_BX:R_BX + XBC_B]
    conv = convb_ref[...]
    for j in range(CONV_W):
        conv = conv + convw_ref[j:j + 1, :] * xpad_ref[5 + j:5 + j + rows, :]
    xbc_all = _silu(conv)
    tail = xpad_ref[5 + rows:8 + rows, :]
    conv_out_ref[...] = tail
    xpad_ref[5:8, :] = tail

    row = _iota((q, q), 0)
    col = _iota((q, q), 1)
    causal = row >= col
    tril = jnp.where(causal, 1.0, 0.0).astype(BF16)
    lane1 = _iota((1, 128), 1)
    is_head = lane1 < H_B
    expand = jnp.where(_iota((128, W_B), 0) == (_iota((128, W_B), 1) >> 6), 1.0, 0.0).astype(BF16)
    grp_lane = _iota((q, 128), 1) >> 6
    head_lane = _iota((q, W_B), 1) >> 6
    same_grp = (_iota((128, W_B), 0) >> 6) == (_iota((128, W_B), 1) >> 7)
    same_head = (_iota((128, W_C), 0) >> 5) == (_iota((128, W_C), 1) >> 6)
    vhead = _iota((q, W_C), 1) >> 6
    a_neg = jnp.where(is_head, -jnp.exp(v128_ref[1:2, :]), 0.0)

    state = sbd_ref[...]
    gla = []
    for u in range(nsub):
        r0 = u * q
        xbc = xbc_all[r0:r0 + q, :]
        misc = rest_ref[r0:r0 + q, R_MISC:R_MISC + 128]
        xs = xbc[:, 0:W_B]
        bmat = xbc[:, W_B:W_B + G_B * N_B].astype(BF16)
        cmat = xbc[:, W_B + G_B * N_B:XBC_B].astype(BF16)
        dt = jnp.where(is_head, _softplus(misc + v128_ref[0:1, :]), 0.0)
        cum = _lmat_exact(tril, dt * a_neg)
        cum_t = cum.T
        ecum = jnp.exp(cum)
        wgt = jnp.exp(cum[q - 1:q, :] - cum) * dt
        dt_e = _rmat_exact(dt, expand, 2)
        ecum_e = _rmat_exact(ecum, expand, 2)
        wgt_e = _rmat_exact(wgt, expand, 2)
        zero_b = jnp.zeros_like(cmat)
        gmat = [lax.dot_general(jnp.where(grp_lane == g, cmat, zero_b), bmat, (((1,), (1,)), ((), ())),
                                preferred_element_type=F32) for g in range(G_B)]
        mcat = []
        for h in range(H_B):
            diff = cum[:, h:h + 1] - cum_t[h:h + 1, :]
            dec = jnp.exp(jnp.where(causal, diff, NEG))
            mcat.append((gmat[h // (H_B // G_B)] * dec).astype(BF16))
        mcat = jnp.concatenate(mcat, axis=1)
        dtx = xs * dt_e
        xbd = jnp.concatenate([jnp.where(head_lane == h, dtx, 0.0) for h in range(H_B)], axis=0).astype(BF16)
        y = _dotf(mcat, xbd) + _dotf(cmat, state.astype(BF16)) * ecum_e
        upd = lax.dot_general(bmat, (xs * wgt_e).astype(BF16), (((0,), (0,)), ((), ())), preferred_element_type=F32)
        state = state * ecum_e[q - 1:q, :] + jnp.where(same_grp, upd, 0.0)
        y = y + v256_ref[0:1, :] * xs
        ob_ref[r0:r0 + q, :] = (_rms(y * _silu(rest_ref[r0:r0 + q, R_BZ:R_BZ + W_B])) * v256_ref[1:2, :]).astype(BF16)
        qg = rest_ref[r0:r0 + q, R_CQ:R_CQ + 128] * (DK_C ** -0.5)
        kg = rest_ref[r0:r0 + q, R_CK:R_CK + 128]
        vg = rest_ref[r0:r0 + q, R_CV:R_CV + W_C]
        gk = _dotf(misc.astype(BF16), wlr_ref[...]) + v128_ref[2:3, :]
        gate = -_softplus(-gk) * (1.0 / GATE_NORM)
        bcum = _lmat_exact(tril, gate)
        gla.append((qg, kg, vg, bcum, qg * jnp.exp(bcum)))
    sbd_ref[...] = state
    ssm_out_ref[...] = state
    span = jnp.max(-gla[0][3][q - 1:q, :])
    for u in range(1, nsub):
        span = jnp.maximum(span, jnp.max(-gla[u][3][q - 1:q, :]))

    @pl.when(span <= 80.0)
    def _():
        causal4 = _iota((q, H_C * q), 0) >= (_iota((q, H_C * q), 1) & (q - 1))
        kmask = (_iota((128, 4 * q), 0) >> 5) == (_iota((128, 4 * q), 1) >> 7)
        for u in range(nsub):
            qg, kg, vg, bcum, qe = gla[u]
            ke = kg * jnp.exp(-bcum)
            kbd = jnp.where(kmask, jnp.tile(ke.T, (1, H_C)), 0.0).astype(BF16)
            att = jnp.where(causal4, _dotf(qe.astype(BF16), kbd), 0.0).astype(BF16)
            vbd = jnp.concatenate([jnp.where(vhead == h, vg, 0.0) for h in range(H_C)], axis=0).astype(BF16)
            oi_ref[u * q:(u + 1) * q, :] = _dotf(att, vbd)

    @pl.when(span > 80.0)
    def _():
        ind = jnp.where((_iota((128, W_C), 0) >> 5) == (_iota((128, W_C), 1) >> 6), 1.0, 0.0).astype(BF16)
        trow = _iota((q, 128), 0)
        for u in range(nsub):
            qg, kg, vg, bcum, qe = gla[u]
            b_ref[...] = bcum

            def body(grp, acc, u=u, qg=qg, bcum=bcum):
                base = pl.multiple_of(grp * 8, 8)
                ks8 = rest_ref[pl.ds(u * q + base, 8), R_CK:R_CK + 128]
                vs8 = rest_ref[pl.ds(u * q + base, 8), R_CV:R_CV + W_C]
                bs8 = b_ref[pl.ds(base, 8), :]
                for j in range(8):
                    d = jnp.exp(jnp.minimum(bcum - bs8[j:j + 1, :], 0.0)) * qg * ks8[j:j + 1, :]
                    d = jnp.where(trow >= base + j, d, 0.0)
                    acc = acc + _dotf(d.astype(BF16), ind) * vs8[j:j + 1, :]
                return acc

            oi_ref[u * q:(u + 1) * q, :] = lax.fori_loop(0, q // 8, body, jnp.zeros((q, W_C), F32))

    sg = sg_ref[...]
    avg = jnp.where((_iota((W_C, W_C), 0) >> 6) == (_iota((W_C, W_C), 1) >> 6), 1.0, 0.0).astype(BF16)
    for u in range(nsub):
        r0 = u * q
        qg, kg, vg, bcum, qe = gla[u]
        o = oi_ref[r0:r0 + q, :] + _dotf(qe.astype(BF16), sg.astype(BF16))
        k2 = (kg * jnp.exp(bcum[q - 1:q, :] - bcum)).astype(BF16)
        updg = lax.dot_general(k2, vg.astype(BF16), (((0,), (0,)), ((), ())), preferred_element_type=F32)
        sg = sg * jnp.exp(bcum.T[:, q - 1:q]) + jnp.where(same_head, updg, 0.0)
        ms = _rmat_exact(o * o, avg, 2) * (1.0 / DV_C)
        oc = o * lax.rsqrt(ms + EPS) * v256_ref[2:3, :] * _silu(rest_ref[r0:r0 + q, R_CG:R_CG + W_C])
        oc_ref[r0:r0 + q, :] = oc.astype(BF16)
    sg_ref[...] = sg
    gla_out_ref[...] = sg


def bc_mixers_prompt(lidx, rest, convw, convb3, vec128, vec256, wlr):
    b, seq, _ = rest.shape
    nsub = next(n for n in (8, 4, 2, 1) if seq % (n * CHUNK) == 0)
    rows = nsub * CHUNK
    par = lambda r, w: pl.BlockSpec((None, r, w), lambda bi, c, l: (l[0], 0, 0))
    st = lambda r, w: pl.BlockSpec((None, r, w), lambda bi, c, l: (bi, 0, 0))
    grid_spec = pltpu.PrefetchScalarGridSpec(
        num_scalar_prefetch=1, grid=(b, seq // rows),
        in_specs=[pl.BlockSpec((None, rows, W_REST), lambda bi, c, l: (bi, c, 0)),
                  par(CONV_W, XBC_B), par(1, XBC_B), par(8, 128), par(8, 256), par(128, 128)],
        out_specs=[pl.BlockSpec((None, rows, W_B), lambda bi, c, l: (bi, c, 0)),
                   pl.BlockSpec((None, rows, W_C), lambda bi, c, l: (bi, c, 0)),
                   st(CONV_W - 1, XBC_B), st(128, W_B), st(128, W_C)],
        scratch_shapes=[pltpu.VMEM((8 + rows, XBC_B), F32), pltpu.VMEM((128, W_B), F32),
                        pltpu.VMEM((128, W_C), F32), pltpu.VMEM((CHUNK, 128), F32),
                        pltpu.VMEM((rows, W_C), F32)])
    return pl.pallas_call(
        functools.partial(_bc_prompt_kernel, nsub=nsub),
        out_shape=[jax.ShapeDtypeStruct((b, seq, W_B), BF16), jax.ShapeDtypeStruct((b, seq, W_C), BF16),
                   jax.ShapeDtypeStruct((b, CONV_W - 1, XBC_B), F32),
                   jax.ShapeDtypeStruct((b, 128, W_B), F32), jax.ShapeDtypeStruct((b, 128, W_C), F32)],
        grid_spec=grid_spec,
        compiler_params=_cparams(("arbitrary", "arbitrary")),
        name="bc_mixers_prompt",
    )(lidx, rest, convw, convb3, vec128, vec256, wlr)


def _unpack_ssm_state(sbd):
    b = sbd.shape[0]
    s = sbd.reshape(b, G_B, N_B, H_B, P_B)
    per_head = [s[:, h // (H_B // G_B), :, h, :] for h in range(H_B)]
    return jnp.swapaxes(jnp.stack(per_head, axis=1), 2, 3)


def _unpack_gla_state(sg):
    b = sg.shape[0]
    s = sg.reshape(b, H_C, DK_C, H_C, DV_C)
    return jnp.stack([s[:, h, :, h, :] for h in range(H_C)], axis=1)


def _out_proj_kernel(l_ref, oa_ref, ob_ref, oc_ref, w_ref, x_ref, g1_ref, sh2_ref, sc2_ref, x1_ref, h2_ref):
    del l_ref
    mix = (_dotf(oa_ref[...], w_ref[0:W_A, :]) + _dotf(ob_ref[...], w_ref[W_A:W_A + W_B, :])
           + _dotf(oc_ref[...], w_ref[W_A + W_B:, :]))
    x1 = x_ref[...] + g1_ref[...] * mix
    x1_ref[...] = x1
    h2_ref[...] = (_rms(x1) * (1.0 + sc2_ref[...]) + sh2_ref[...]).astype(h2_ref.dtype)


def out_projection(lidx, oa, ob, oc, w_out, x, mod, tm, h2_dtype):
    g, r, d = x.shape
    per_row = mod.shape[2] > 1
    act = lambda w: pl.BlockSpec((None, tm, w), lambda gi, i, l: (gi, i, 0))
    grid_spec = pltpu.PrefetchScalarGridSpec(
        num_scalar_prefetch=1, grid=(g, r // tm),
        in_specs=[act(W_A), act(W_B), act(W_C),
                  pl.BlockSpec((None, d, d), lambda gi, i, l: (l[0], 0, 0)),
                  act(d), _mod_spec(tm, per_row, 2), _mod_spec(tm, per_row, 3), _mod_spec(tm, per_row, 4)],
        out_specs=[act(d), act(d)])
    return pl.pallas_call(
        _out_proj_kernel,
        out_shape=[jax.ShapeDtypeStruct((g, r, d), F32), jax.ShapeDtypeStruct((g, r, d), h2_dtype)],
        grid_spec=grid_spec,
        compiler_params=_cparams(("arbitrary", "arbitrary")),
        name="out_projection",
    )(lidx, oa, ob, oc, w_out, x, mod, mod, mod)


def _ffn_kernel(i_ref, h_ref, wg_ref, wu_ref, wd_ref, x1_ref, g2_ref, fg_ref, o_ref, acc_ref, *, final):
    del i_ref
    f = pl.program_id(2)

    @pl.when(f == 0)
    def _():
        acc_ref[...] = jnp.zeros(acc_ref.shape, F32)

    h = h_ref[...]
    a = (_silu(_dotf(h, wg_ref[...])) * _dotf(h, wu_ref[...])).astype(BF16)
    acc_ref[...] += _dotf(a, wd_ref[...])

    @pl.when(f == pl.num_programs(2) - 1)
    def _():
        x2 = x1_ref[...] + g2_ref[...] * acc_ref[...]
        o_ref[...] = _rms(x2) * fg_ref[...] if final else x2


def dense_ffn(iidx, lidx_mod, h2, wg, wu, wd, x1, mod, final_g, tm, tf, final):
    g, r, d = x1.shape
    per_row = mod.shape[2] > 1
    nf = D_FF // tf
    act = lambda w: pl.BlockSpec((None, tm, w), lambda gi, i, f, s: (gi, i, 0))
    if per_row:
        g2 = pl.BlockSpec((None, None, tm, d), lambda gi, i, f, s: (s[1], gi, i, 5))
    else:
        g2 = pl.BlockSpec((None, None, 1, d), lambda gi, i, f, s: (s[1], gi, 0, 5))
    grid_spec = pltpu.PrefetchScalarGridSpec(
        num_scalar_prefetch=1, grid=(g, r // tm, nf),
        in_specs=[act(d),
                  pl.BlockSpec((None, d, tf), lambda gi, i, f, s: (s[0], 0, f)),
                  pl.BlockSpec((None, d, tf), lambda gi, i, f, s: (s[0], 0, f)),
                  pl.BlockSpec((None, tf, d), lambda gi, i, f, s: (s[0], f, 0)),
                  act(d), g2, pl.BlockSpec((1, d), lambda gi, i, f, s: (0, 0))],
        out_specs=act(d),
        scratch_shapes=[pltpu.VMEM((tm, d), F32)])
    sidx = jnp.concatenate([iidx, lidx_mod])
    return pl.pallas_call(
        functools.partial(_ffn_kernel, final=final),
        out_shape=jax.ShapeDtypeStruct((g, r, d), F32),
        grid_spec=grid_spec,
        compiler_params=_cparams(("arbitrary", "arbitrary", "arbitrary")),
        name="dense_ffn",
    )(sidx, h2, wg, wu, wd, x1, mod, final_g)


def _route_kernel(i_ref, h_ref, rw_ref, tri_ref, idx_ref, gate_ref, cnt_ref, carry_ref):
    del i_ref
    i = pl.program_id(0)

    @pl.when(i == 0)
    def _():
        carry_ref[...] = jnp.zeros(carry_ref.shape, F32)

    hh, hl = _split2(h_ref[...])
    wh, wl = _split2(rw_ref[...])
    logits = _dotf(hh, wh) + _dotf(hl, wh) + _dotf(hh, wl)
    lane = _iota(logits.shape, 1).astype(F32)
    logits = jnp.where(lane < N_EXPERTS, logits, NEG)
    m1 = jnp.max(logits, axis=1, keepdims=True)
    i1 = jnp.min(jnp.where(logits == m1, lane, 128.0), axis=1, keepdims=True)
    rest = jnp.where(lane == i1, NEG, logits)
    m2 = jnp.max(rest, axis=1, keepdims=True)
    i2 = jnp.min(jnp.where(rest == m2, lane, 128.0), axis=1, keepdims=True)
    e = jnp.exp(m2 - m1)
    g1 = 1.0 / (1.0 + e)
    sel = jnp.where(lane == i1, 1.0, 0.0) + jnp.where(lane == i2, 1.0, 0.0)
    before = _dotf(tri_ref[...], sel.astype(BF16)) + carry_ref[...]
    r1 = jnp.sum(jnp.where(lane == i1, before, 0.0), axis=1, keepdims=True)
    r2 = jnp.sum(jnp.where(lane == i2, before, 0.0), axis=1, keepdims=True)
    carry_ref[...] = carry_ref[...] + jnp.sum(sel, axis=0, keepdims=True)
    lane8 = _iota(idx_ref.shape, 1).astype(F32)
    idx_ref[...] = jnp.where(lane8 == 0.0, i1, jnp.where(lane8 == 1.0, i2, jnp.where(lane8 == 2.0, r1, r2))).astype(I32)
    gate_ref[...] = jnp.where(lane8 == 0.0, g1, e * g1)
    cnt_ref[...] = jnp.broadcast_to(carry_ref[...], cnt_ref.shape)


def moe_route(iidx, h2, router_pad, tm):
    t, d = h2.shape
    tri = jnp.asarray(np.tril(np.ones((tm, tm), np.float32), -1), BF16)
    grid_spec = pltpu.PrefetchScalarGridSpec(
        num_scalar_prefetch=1, grid=(t // tm,),
        in_specs=[pl.BlockSpec((tm, d), lambda i, s: (i, 0)),
                  pl.BlockSpec((None, d, 128), lambda i, s: (s[0], 0, 0)),
                  pl.BlockSpec((tm, tm), lambda i, s: (0, 0))],
        out_specs=[pl.BlockSpec((tm, 8), lambda i, s: (i, 0)), pl.BlockSpec((tm, 8), lambda i, s: (i, 0)),
                   pl.BlockSpec((8, 128), lambda i, s: (0, 0))],
        scratch_shapes=[pltpu.VMEM((1, 128), F32)])
    return pl.pallas_call(
        _route_kernel,
        out_shape=[jax.ShapeDtypeStruct((t, 8), I32), jax.ShapeDtypeStruct((t, 8), F32),
                   jax.ShapeDtypeStruct((8, 128), F32)],
        grid_spec=grid_spec,
        compiler_params=_cparams(("arbitrary",)),
        name="moe_route",
    )(iidx, h2, router_pad, tri)


def _row_copy(src_ref, src_row, dst_ref, dst_row, sem):
    return pltpu.make_async_copy(src_ref.at[pl.ds(src_row, 1), :], dst_ref.at[pl.ds(dst_row, 1), :], sem)


def _dispatch_kernel(dest_ref, fill_ref, h_ref, xb_ref, sem, zrow_ref, zsem):
    tm = h_ref.shape[0]

    @pl.when(pl.program_id(0) == 0)
    def _():
        zrow_ref[...] = jnp.zeros(zrow_ref.shape, F32)
        zr = zrow_ref.shape[0]
        for phase in ("start", "wait"):
            for e in range(N_EXPERTS + 1):
                lo, hi = fill_ref[2 * e], fill_ref[2 * e + 1]
                lo8 = jnp.minimum(((lo + 7) >> 3) << 3, hi)
                n_big = (hi - lo8) // zr
                mid = lo8 + n_big * zr

                def piece(rows, base, phase=phase):
                    def body(i, carry):
                        at = pl.multiple_of(base + i * rows, 8)
                        cp = pltpu.make_async_copy(zrow_ref.at[pl.ds(0, rows), :], xb_ref.at[pl.ds(at, rows), :], zsem)
                        cp.start() if phase == "start" else cp.wait()
                        return carry
                    return body

                def one(r, carry, phase=phase):
                    cp = _row_copy(zrow_ref, 0, xb_ref, r, zsem)
                    cp.start() if phase == "start" else cp.wait()
                    return carry

                lax.fori_loop(lo, lo8, one, 0)
                lax.fori_loop(0, n_big, piece(zr, lo8), 0)
                lax.fori_loop(0, (hi - mid) >> 3, piece(8, mid), 0)

    def start(r, carry):
        for k in range(TOP_K):
            _row_copy(h_ref, r, xb_ref, dest_ref[0, TOP_K * r + k], sem).start()
        return carry

    lax.fori_loop(0, tm, start, 0, unroll=8)
    for k in range(TOP_K):
        pltpu.make_async_copy(h_ref, xb_ref.at[pl.ds(0, tm), :], sem).wait()


def moe_dispatch(dest3, fill, h2, n_rows, tm):
    t, d = h2.shape
    return pl.pallas_call(
        _dispatch_kernel,
        out_shape=jax.ShapeDtypeStruct((n_rows, d), F32),
        grid=(t // tm,),
        in_specs=[pl.BlockSpec((None, 1, TOP_K * tm), lambda i: (i, 0, 0), memory_space=pltpu.SMEM),
                  pl.BlockSpec(memory_space=pltpu.SMEM),
                  pl.BlockSpec((tm, d), lambda i: (i, 0))],
        out_specs=pl.BlockSpec(memory_space=pl.ANY),
        scratch_shapes=[pltpu.SemaphoreType.DMA(()), pltpu.VMEM((64, d), F32), pltpu.SemaphoreType.DMA(())],
        compiler_params=pltpu.CompilerParams(dimension_semantics=("arbitrary",), vmem_limit_bytes=VMEM_LIMIT,
                                             has_side_effects=True),
        name="moe_dispatch",
    )(dest3, fill, h2)


def _expert_kernel(s_ref, x_ref, wg_ref, wu_ref, wd_ref, y_ref, xb_ref, acc_ref):
    i = pl.program_id(0)
    f = pl.program_id(1)
    nb = pl.num_programs(0)

    @pl.when(i < s_ref[nb + 1])
    def _():
        @pl.when(f == 0)
        def _():
            xb_ref[...] = x_ref[...].astype(BF16)
            acc_ref[...] = jnp.zeros(acc_ref.shape, F32)

        x = xb_ref[...]
        a = (_silu(_dotf(x, wg_ref[...])) * _dotf(x, wu_ref[...])).astype(BF16)
        acc_ref[...] += _dotf(a, wd_ref[...])

        @pl.when(f == pl.num_programs(1) - 1)
        def _():
            y_ref[...] = acc_ref[...]

    @pl.when(jnp.logical_and(i >= s_ref[nb + 1], f == pl.num_programs(1) - 1))
    def _():
        y_ref[...] = jnp.zeros(y_ref.shape, F32)


def moe_experts(sidx, xb, wg, wu, wd, blk, tf):
    n_rows, d = xb.shape
    nb = n_rows // blk
    nf = D_FF // tf

    def row_map(i, f, s):
        return (jnp.minimum(i, s[nb + 1] - 1), 0)

    def f_of(i, f, s):
        return jnp.where(i < s[nb + 1], f, nf - 1)

    grid_spec = pltpu.PrefetchScalarGridSpec(
        num_scalar_prefetch=1, grid=(nb, nf),
        in_specs=[pl.BlockSpec((blk, d), row_map),
                  pl.BlockSpec((None, None, d, tf), lambda i, f, s: (s[nb], s[i], 0, f_of(i, f, s))),
                  pl.BlockSpec((None, None, d, tf), lambda i, f, s: (s[nb], s[i], 0, f_of(i, f, s))),
                  pl.BlockSpec((None, None, tf, d), lambda i, f, s: (s[nb], s[i], f_of(i, f, s), 0))],
        out_specs=pl.BlockSpec((blk, d), lambda i, f, s: (i, 0)),
        scratch_shapes=[pltpu.VMEM((blk, d), BF16), pltpu.VMEM((blk, d), F32)])
    return pl.pallas_call(
        _expert_kernel,
        out_shape=jax.ShapeDtypeStruct((n_rows, d), F32),
        grid_spec=grid_spec,
        compiler_params=_cparams(("arbitrary", "arbitrary")),
        name="moe_experts",
    )(sidx, xb, wg, wu, wd)


def _combine_kernel(dest_ref, yb_ref, gate_ref, x1_ref, g2_ref, fg_ref, o_ref, buf_ref, sem, *, final):
    tm = x1_ref.shape[0]

    def start(r, carry):
        for k in range(TOP_K):
            _row_copy(yb_ref, dest_ref[0, TOP_K * r + k], buf_ref.at[k], r, sem).start()
        return carry

    lax.fori_loop(0, tm, start, 0, unroll=8)
    for k in range(TOP_K):
        pltpu.make_async_copy(yb_ref.at[pl.ds(0, tm), :], buf_ref.at[k], sem).wait()
    gate = gate_ref[...]
    f = gate[:, 0:1] * buf_ref[0] + gate[:, 1:2] * buf_ref[1]
    x2 = x1_ref[...] + g2_ref[...] * f
    o_ref[...] = _rms(x2) * fg_ref[...] if final else x2


def moe_combine(dest3, yb, gates, x1, mod, final_g, tm, final):
    g, r, d = x1.shape
    per_row = mod.shape[1] > 1
    nt = r // tm
    act = pl.BlockSpec((None, tm, d), lambda gi, i: (gi, i, 0))
    return pl.pallas_call(
        functools.partial(_combine_kernel, final=final),
        out_shape=jax.ShapeDtypeStruct((g, r, d), F32),
        grid=(g, nt),
        in_specs=[pl.BlockSpec((None, 1, TOP_K * tm), lambda gi, i: (gi * nt + i, 0, 0), memory_space=pltpu.SMEM),
                  pl.BlockSpec(memory_space=pl.ANY),
                  pl.BlockSpec((tm, 8), lambda gi, i: (gi * nt + i, 0)),
                  act,
                  (pl.BlockSpec((None, tm, d), lambda gi, i: (gi, i, 0)) if per_row
                   else pl.BlockSpec((None, 1, d), lambda gi, i: (gi, 0, 0))),
                  pl.BlockSpec((1, d), lambda gi, i: (0, 0))],
        out_specs=act,
        scratch_shapes=[pltpu.VMEM((TOP_K, tm, d), F32), pltpu.SemaphoreType.DMA(())],
        compiler_params=_cparams(("arbitrary", "arbitrary")),
        name="moe_combine",
    )(dest3, yb, gates, x1, mod, final_g)


def moe_ffn(layer_slot, h2, x1, mod_g2, router_pad, wg, wu, wd, final_g, tm, blk, tf, final):
    g, r, d = x1.shape
    t = g * r
    h2f = h2.reshape(t, d)
    iidx = jnp.array([layer_slot], I32)
    idx, gates, counts = moe_route(iidx, h2f, router_pad, _pick(t, 1024))
    counts = counts[0, :N_EXPERTS].astype(I32)
    padded = (counts + blk - 1) // blk * blk
    pad_end = jnp.cumsum(padded)
    pad_start = pad_end - padded
    first = sum(jnp.where(idx[:, 0:TOP_K] == e, pad_start[e], 0) for e in range(N_EXPERTS))
    dest = first + idx[:, TOP_K:2 * TOP_K]
    nb = -(-(t * TOP_K) // blk) + N_EXPERTS
    n_used = pad_end[-1] // blk
    blk_e = jnp.minimum(jnp.sum(jnp.arange(nb, dtype=I32)[:, None] * blk >= pad_end[None, :], axis=1), N_EXPERTS - 1)
    blk_e = jnp.where(jnp.arange(nb) < n_used, blk_e, blk_e[jnp.maximum(n_used - 1, 0)]).astype(I32)
    dest3 = dest.astype(I32).reshape(t // tm, 1, TOP_K * tm)
    fill = jnp.stack([jnp.append(pad_start + counts, pad_end[-1]),
                      jnp.append(pad_end, nb * blk)], axis=1).reshape(-1).astype(I32)
    xb = moe_dispatch(dest3, fill, h2f, nb * blk, tm)
    sidx = jnp.concatenate([blk_e, iidx, n_used.astype(I32)[None]])
    yb = moe_experts(sidx, xb, wg, wu, wd, blk, tf)
    return moe_combine(dest3, yb, gates, x1, mod_g2, final_g, tm, final)


def _attn_decode_kernel(pt_ref, lam_ref, lqk_ref, g_ref, q_ref, kn_ref, vn_ref, tab_ref, ck_ref, cv_ref, o_ref,
                        kbuf_ref, vbuf_ref, sem, m_ref, s_ref, acc_ref, *, pps):
    step = pl.program_id(1)
    n_steps = pl.num_programs(1)
    layer = pt_ref[pt_ref.shape[0] - 1]
    rows = 2 * H_A
    t = pl.program_id(0) * n_steps + step
    slot = lax.rem(t, 2)

    def start_pages(tt, sl):
        for i in range(pps):
            page = pt_ref[tt * pps + i]
            pltpu.make_async_copy(ck_ref.at[layer, page], kbuf_ref.at[sl, i], sem.at[0, sl]).start()
            pltpu.make_async_copy(cv_ref.at[layer, page], vbuf_ref.at[sl, i], sem.at[1, sl]).start()

    @pl.when(t == 0)
    def _():
        start_pages(0, 0)

    @pl.when(t + 1 < pl.num_programs(0) * n_steps)
    def _():
        start_pages(t + 1, 1 - slot)

    pltpu.make_async_copy(ck_ref.at[layer, pl.ds(0, pps)], kbuf_ref.at[slot], sem.at[0, slot]).wait()
    pltpu.make_async_copy(cv_ref.at[layer, pl.ds(0, pps)], vbuf_ref.at[slot], sem.at[1, slot]).wait()

    @pl.when(step == 0)
    def _():
        m_ref[...] = jnp.full(m_ref.shape, NEG, F32)
        s_ref[...] = jnp.zeros(s_ref.shape, F32)
        acc_ref[...] = jnp.zeros(acc_ref.shape, F32)

    qrow = jnp.broadcast_to(q_ref[...].astype(F32), (rows, W_A))
    own = (_iota((rows, W_A), 1) >> 6) == _iota((rows, W_A), 0)
    q8 = jnp.where(own, qrow, 0.0).astype(BF16)

    def accumulate(s, pv):
        m_old = m_ref[...]
        m_new = jnp.maximum(m_old, jnp.max(s, axis=1, keepdims=True))
        alpha = jnp.exp(m_old - m_new)
        p = jnp.exp(s - m_new)
        s_ref[...] = alpha * s_ref[...] + jnp.sum(p, axis=1, keepdims=True)
        acc_ref[...] = alpha * acc_ref[...] + pv(p)
        m_ref[...] = m_new

    scores = [_dotf(q8, kbuf_ref[slot, i].astype(BF16)) for i in range(pps)]
    scores[pps - 1] = jnp.where(step == n_steps - 1, scores[pps - 1] + tab_ref[0:rows, :], scores[pps - 1])
    s_all = jnp.concatenate(scores, axis=1)

    def value_page(i):
        heads = [vbuf_ref[slot, i, pl.ds(h, PAGE, stride=H_A), :] for h in range(H_A)]
        return jnp.concatenate(heads, axis=1).astype(BF16)

    def pv_pages(p):
        hi, lo = _split2(p)
        out = None
        for i in range(pps):
            v = value_page(i)
            part = _dotf(hi[:, i * PAGE:(i + 1) * PAGE], v) + _dotf(lo[:, i * PAGE:(i + 1) * PAGE], v)
            out = part if out is None else out + part
        return out

    accumulate(s_all, pv_pages)

    @pl.when(step == n_steps - 1)
    def _():
        kn = kn_ref[...].astype(BF16).astype(F32)
        vn = vn_ref[...].astype(BF16).astype(F32)
        s_new = jnp.sum(q8.astype(F32) * kn, axis=1, keepdims=True) + tab_ref[rows:2 * rows, 0:1]
        accumulate(s_new, lambda p: p * vn)
        o = acc_ref[...] * (1.0 / s_ref[...])
        lam = _lambda_value(lqk_ref[...], lam_ref[2 * layer])
        outs = []
        for h in range(H_A):
            blkh = o[:, h * DV_A:(h + 1) * DV_A]
            oh = blkh[2 * h:2 * h + 1, :] - lam * blkh[2 * h + 1:2 * h + 2, :]
            outs.append(_rms(oh) * g_ref[...] * lam_ref[2 * layer + 1])
        o_ref[...] = jnp.concatenate(outs, axis=1).astype(BF16)


def attention_decode(sidx, lam_consts, lambda_qk, subln_g3, q, k_new, v_new, tab, cache_k4, cache_v4, n_pages, pps):
    bd = q.shape[0]
    nl = sidx.shape[0] - 1
    assert n_pages % pps == 0
    row = pl.BlockSpec((None, 1, W_A), lambda b, s, pt: (b, 0, 0))
    grid_spec = pltpu.PrefetchScalarGridSpec(
        num_scalar_prefetch=1, grid=(bd, n_pages // pps),
        in_specs=[pl.BlockSpec(memory_space=pltpu.SMEM),
                  pl.BlockSpec((None, 4, DK_A), lambda b, s, pt: (pt[nl], 0, 0)),
                  pl.BlockSpec((None, 1, DV_A), lambda b, s, pt: (pt[nl], 0, 0)),
                  row, row, row,
                  pl.BlockSpec((16, PAGE), lambda b, s, pt: (0, 0)),
                  pl.BlockSpec(memory_space=pl.ANY), pl.BlockSpec(memory_space=pl.ANY)],
        out_specs=row,
        scratch_shapes=[pltpu.VMEM((2, pps, W_A, PAGE), F32), pltpu.VMEM((2, pps, H_A * PAGE, DV_A), F32),
                        pltpu.SemaphoreType.DMA((2, 2)),
                        pltpu.VMEM((2 * H_A, 1), F32), pltpu.VMEM((2 * H_A, 1), F32),
                        pltpu.VMEM((2 * H_A, W_A), F32)])
    return pl.pallas_call(
        functools.partial(_attn_decode_kernel, pps=pps),
        out_shape=jax.ShapeDtypeStruct((bd, 1, W_A), BF16),
        grid_spec=grid_spec,
        compiler_params=_cparams(("arbitrary", "arbitrary")),
        name="attention_decode",
    )(sidx, lam_consts, lambda_qk, subln_g3, q, k_new, v_new, tab, cache_k4, cache_v4)


def _bc_decode_pre_kernel(l_ref, rest_ref, buf_ref, convw_ref, convb_ref, v128_ref, wlr_ref,
                          xbc_ref, nbuf_ref, dt_ref, dec_ref, eg_ref):
    del l_ref
    u = rest_ref[:, R_BX:R_BX + XBC_B]
    conv = convb_ref[...] + convw_ref[CONV_W - 1:CONV_W, :] * u
    for j in range(CONV_W - 1):
        conv = conv + convw_ref[j:j + 1, :] * buf_ref[j]
    xbc_ref[...] = _silu(conv)
    for j in range(CONV_W - 2):
        nbuf_ref[j] = buf_ref[j + 1]
    nbuf_ref[CONV_W - 2] = u
    misc = rest_ref[:, R_MISC:R_MISC + 128]
    dt = _softplus(misc + v128_ref[0:1, :])
    dt_ref[...] = dt
    dec_ref[...] = jnp.exp(dt * (-jnp.exp(v128_ref[1:2, :])))
    gk = _dotf(misc.astype(BF16), wlr_ref[...]) + v128_ref[2:3, :]
    eg_ref[...] = jnp.exp(-_softplus(-gk) * (1.0 / GATE_NORM))


def bc_decode_pre(lidx, rest, conv_state_t, convw, convb3, vec128, wlr):
    bd = rest.shape[0]
    par = lambda r, w: pl.BlockSpec((None, r, w), lambda i, l: (l[0], 0, 0))
    full = lambda w: pl.BlockSpec((bd, w), lambda i, l: (0, 0))
    grid_spec = pltpu.PrefetchScalarGridSpec(
        num_scalar_prefetch=1, grid=(1,),
        in_specs=[full(W_REST),
                  pl.BlockSpec((None, CONV_W - 1, bd, XBC_B), lambda i, l: (l[0], 0, 0, 0)),
                  par(CONV_W, XBC_B), par(1, XBC_B), par(8, 128), par(128, 128)],
        out_specs=[full(XBC_B), pl.BlockSpec((CONV_W - 1, bd, XBC_B), lambda i, l: (0, 0, 0)),
                   full(128), full(128), full(128)])
    return pl.pallas_call(
        _bc_decode_pre_kernel,
        out_shape=[jax.ShapeDtypeStruct((bd, XBC_B), F32), jax.ShapeDtypeStruct((CONV_W - 1, bd, XBC_B), F32),
                   jax.ShapeDtypeStruct((bd, 128), F32), jax.ShapeDtypeStruct((bd, 128), F32),
                   jax.ShapeDtypeStruct((bd, 128), F32)],
        grid_spec=grid_spec,
        compiler_params=_cparams(("arbitrary",)),
        name="bc_decode_pre",
    )(lidx, rest, conv_state_t, convw, convb3, vec128, wlr)


def _bc_decode_state_kernel(l_ref, ssm_ref, gla_ref, x4_ref, b4_ref, c4_ref, dt4_ref, dec4_ref, bz4_ref,
                            dsk_ref, ng_ref, q4_ref, k4_ref, v4_ref, eg4_ref, cg4_ref, gng_ref,
                            ssm_out_ref, gla_out_ref, ob4_ref, oc4_ref):
    del l_ref
    x4 = x4_ref[...]
    s = ssm_ref[...] * dec4_ref[...] + (dt4_ref[...] * x4) * b4_ref[...]
    ssm_out_ref[...] = s
    y = jnp.sum(c4_ref[...] * s, axis=3, keepdims=True) + dsk_ref[...] * x4
    yg = y * _silu(bz4_ref[...])
    ms = jnp.sum(jnp.sum(yg * yg, axis=2, keepdims=True), axis=1, keepdims=True) * (1.0 / W_B)
    ob4_ref[...] = yg * lax.rsqrt(ms + EPS) * ng_ref[...]
    sg = gla_ref[...] * eg4_ref[...] + k4_ref[...] * v4_ref[...]
    gla_out_ref[...] = sg
    o = jnp.sum(q4_ref[...] * sg, axis=2, keepdims=True)
    oc4_ref[...] = _rms(o) * gng_ref[...] * _silu(cg4_ref[...])


def bc_decode_state(lidx, state_ssm, state_gla, x4, b4, c4, dt4, dec4, bz4, dsk4, ng4, q4, k4, v4, eg4, cg4, gng4):
    bd = x4.shape[0]

    def full(shape):
        n = len(shape)
        return pl.BlockSpec(shape, lambda i, l: (0,) * n)

    def layer(shape):
        n = len(shape)
        return pl.BlockSpec((None,) + shape, lambda i, l: (l[0],) + (0,) * n)

    s_ssm = (bd, H_B, P_B, N_B)
    s_gla = (bd, H_C, DK_C, DV_C)
    col_b = (bd, H_B, P_B, 1)
    row_b = (bd, H_B, 1, N_B)
    one_b = (bd, H_B, 1, 1)
    col_c = (bd, H_C, DK_C, 1)
    row_c = (bd, H_C, 1, DV_C)
    grid_spec = pltpu.PrefetchScalarGridSpec(
        num_scalar_prefetch=1, grid=(1,),
        in_specs=[layer(s_ssm), layer(s_gla), full(col_b), full(row_b), full(row_b), full(one_b), full(one_b),
                  full(col_b), layer((1, H_B, 1, 1)), layer((1, H_B, P_B, 1)),
                  full(col_c), full(col_c), full(row_c), full(col_c), full(row_c), layer((1, 1, 1, DV_C))],
        out_specs=[full(s_ssm), full(s_gla), full(col_b), full(row_c)])
    return pl.pallas_call(
        _bc_decode_state_kernel,
        out_shape=[jax.ShapeDtypeStruct(s_ssm, F32), jax.ShapeDtypeStruct(s_gla, F32),
                   jax.ShapeDtypeStruct(col_b, F32), jax.ShapeDtypeStruct(row_c, F32)],
        grid_spec=grid_spec,
        compiler_params=_cparams(("arbitrary",)),
        name="bc_decode_state",
    )(lidx, state_ssm, state_gla, x4, b4, c4, dt4, dec4, bz4, dsk4, ng4, q4, k4, v4, eg4, cg4, gng4)


def _pack_params(w_in, rel_bias, subln_g, conv_b, dt_bias, a_log, d_skip, ssd_norm_g, gla_w_lr, gla_b_lr,
                 gla_norm_g, router_w):
    depth = w_in.shape[0]
    pad = jnp.zeros(w_in.shape[:2] + (D_PACK - D_IN,), w_in.dtype)
    w_pack = jnp.concatenate([w_in[..., :OFF_BDT], w_in[..., OFF_CQ:OFF_CLR], w_in[..., OFF_BDT:OFF_CQ],
                              w_in[..., OFF_CLR:], pad], axis=-1).astype(BF16)
    vec128 = jnp.zeros((depth, 8, 128), F32)
    vec128 = vec128.at[:, 0, :H_B].set(dt_bias).at[:, 1, :H_B].set(a_log).at[:, 2, :].set(gla_b_lr)
    vec256 = jnp.zeros((depth, 8, 256), F32)
    vec256 = (vec256.at[:, 0, :].set(jnp.repeat(d_skip, P_B, axis=1)).at[:, 1, :].set(ssd_norm_g)
              .at[:, 2, :].set(jnp.tile(gla_norm_g, (1, H_C))))
    wlr = jnp.zeros((depth, 128, 128), F32).at[:, MISC_LR:MISC_LR + GATE_RANK, :].set(gla_w_lr).astype(BF16)
    router_pad = jnp.zeros(router_w.shape[:2] + (128,), F32).at[..., :N_EXPERTS].set(router_w)
    lam_init = [0.8 - 0.6 * math.exp(-0.3 * l) for l in range(depth)]
    lam_consts = jnp.asarray(np.array([[li, 1.0 - li] for li in lam_init], np.float32).reshape(-1))
    return dict(w_pack=w_pack, vec128=vec128, vec256=vec256, wlr=wlr, router_pad=router_pad,
                lam_consts=lam_consts, subln_g3=subln_g.reshape(depth, 1, DV_A),
                convb3=conv_b.reshape(depth, 1, XBC_B))


def _pick(n, pref):
    if n <= pref:
        return n
    t = pref
    while n % t:
        t //= 2
    return t


def _channel_mixer(l, depth, h2, x1, mod, pk, wts, tm, moe_blk, final):
    i = l // 2
    lmod = jnp.array([l], I32)
    if l % 2 == 0:
        return dense_ffn(jnp.array([i], I32), lmod, h2, wts['ffn_g'], wts['ffn_u'], wts['ffn_d'], x1, mod,
                         wts['final_g'], tm, FF_TILE, final)
    mod_g2 = mod[l, :, :, 5 * D_MODEL:6 * D_MODEL]
    return moe_ffn(i, h2, x1, mod_g2, pk['router_pad'], wts['moe_g'], wts['moe_u'], wts['moe_d'], wts['final_g'],
                   _pick(x1.shape[0] * x1.shape[1], 512), moe_blk, FF_TILE, final)


def _run_prompt(x, mod, pk, wts, lambda_qk, conv_w, rel_bias):
    b, seq, d = x.shape
    depth = mod.shape[0]
    tm = _pick(seq // 2, 512)
    bias_tiles = attention_bias_tiles(rel_bias, tm)
    k_all = jnp.zeros((depth, b, W_A, seq), F32)
    v_all = jnp.zeros((depth, b, H_A * seq, DV_A), F32)
    convs, ssms, glas = [], [], []
    for l in range(depth):
        lidx = jnp.array([l], I32)
        q, rest, kt, vb, k_all, v_all = in_projection_prompt(lidx, x, mod, pk['w_pack'], k_all, v_all, tm, 2 * tm)
        oa = attention_prompt(lidx, pk['lam_consts'], lambda_qk, pk['subln_g3'], q, kt, vb, bias_tiles, tm)
        ob, oc, conv_s, ssm_s, gla_s = bc_mixers_prompt(lidx, rest, conv_w, pk['convb3'], pk['vec128'],
                                                        pk['vec256'], pk['wlr'])
        moe = l % 2 == 1
        x1, h2 = out_projection(lidx, oa, ob, oc, wts['w_out'], x, mod, tm, F32 if moe else BF16)
        x = _channel_mixer(l, depth, h2, x1, mod, pk, wts, tm, _pick(b * seq, 512), l == depth - 1)
        convs.append(conv_s)
        ssms.append(_unpack_ssm_state(ssm_s))
        glas.append(_unpack_gla_state(gla_s))
    ks = jnp.transpose(k_all.reshape(depth, b, 2 * H_A, DK_A, seq), (0, 1, 4, 2, 3))
    vs = v_all.reshape(depth, b, seq, H_A, DV_A)
    return x, ks, vs, jnp.stack(convs), jnp.stack(ssms), jnp.stack(glas)


def _run_decode(x, mod, pk, wts, lambda_qk, conv_w, rel_bias, cache_k, cache_v, page_table,
                state_conv, state_ssm, state_gla, d_skip, ssd_norm_g, gla_norm_g):
    bd = x.shape[0]
    depth = mod.shape[0]
    n_pages = page_table.shape[1]
    past = n_pages * PAGE
    pps = _pick(n_pages, 32)
    xg = x.reshape(1, bd, D_MODEL)
    pool = cache_k.shape[1]
    assert PAGE >= T5_MAX_DIST
    cache_k4 = jnp.transpose(cache_k, (0, 1, 3, 4, 2)).reshape(depth, pool, 2 * H_A * DK_A, PAGE)
    cache_v4 = cache_v.reshape(depth, pool, PAGE * H_A, DV_A)
    table = _t5_table(rel_bias, PAGE + 1)
    dist = past - ((n_pages - 1) * PAGE + jnp.arange(PAGE))
    tab = jnp.concatenate([jnp.repeat(table[:, dist], 2, axis=0),
                           jnp.broadcast_to(jnp.repeat(table[:, 0], 2)[:, None], (2 * H_A, PAGE))], axis=0)
    conv_t = jnp.swapaxes(state_conv, 1, 2)
    grp = np.arange(H_B) // (H_B // G_B)
    dsk4 = d_skip.reshape(depth, 1, H_B, 1, 1)
    ng4 = ssd_norm_g.reshape(depth, 1, H_B, P_B, 1)
    gng4 = gla_norm_g.reshape(depth, 1, 1, 1, DV_C)
    pt_flat = page_table.reshape(-1).astype(I32)
    ks, vs, convs, ssms, glas = [], [], [], [], []
    for l in range(depth):
        lidx = jnp.array([l], I32)
        q, k, v, rest = in_projection(lidx, xg, mod, pk['w_pack'], bd)
        sidx = jnp.concatenate([pt_flat, lidx])
        oa = attention_decode(sidx, pk['lam_consts'], lambda_qk, pk['subln_g3'], q.reshape(bd, 1, W_A),
                              k.reshape(bd, 1, W_A), v.reshape(bd, 1, W_A), tab, cache_k4, cache_v4, n_pages, pps)
        rest2 = rest.reshape(bd, W_REST)
        xbc, nbuf, dt, dec, eg = bc_decode_pre(lidx, rest2, conv_t, conv_w, pk['convb3'], pk['vec128'], pk['wlr'])
        x4 = xbc[:, :W_B].reshape(bd, H_B, P_B, 1)
        b4 = xbc[:, W_B:W_B + G_B * N_B].reshape(bd, G_B, 1, N_B)[:, grp]
        c4 = xbc[:, W_B + G_B * N_B:].reshape(bd, G_B, 1, N_B)[:, grp]
        dt4 = dt[:, :H_B].reshape(bd, H_B, 1, 1)
        dec4 = dec[:, :H_B].reshape(bd, H_B, 1, 1)
        bz4 = rest2[:, R_BZ:R_BZ + W_B].reshape(bd, H_B, P_B, 1)
        q4 = (rest2[:, R_CQ:R_CQ + 128] * (DK_C ** -0.5)).reshape(bd, H_C, DK_C, 1)
        k4 = rest2[:, R_CK:R_CK + 128].reshape(bd, H_C, DK_C, 1)
        v4 = rest2[:, R_CV:R_CV + W_C].reshape(bd, H_C, 1, DV_C)
        eg4 = eg.reshape(bd, H_C, DK_C, 1)
        cg4 = rest2[:, R_CG:R_CG + W_C].reshape(bd, H_C, 1, DV_C)
        ssm_s, gla_s, ob4, oc4 = bc_decode_state(lidx, state_ssm, state_gla, x4, b4, c4, dt4, dec4, bz4, dsk4, ng4,
                                                 q4, k4, v4, eg4, cg4, gng4)
        ob = ob4.reshape(1, bd, W_B).astype(BF16)
        oc = oc4.reshape(1, bd, W_C).astype(BF16)
        moe = l % 2 == 1
        x1, h2 = out_projection(lidx, oa.reshape(1, bd, W_A), ob, oc, wts['w_out'], xg, mod, bd, F32 if moe else BF16)
        xg = _channel_mixer(l, depth, h2, x1, mod, pk, wts, bd, 64, l == depth - 1)
        ks.append(k.reshape(bd, 1, 2 * H_A, DK_A))
        vs.append(v.reshape(bd, 1, H_A, DV_A))
        convs.append(jnp.swapaxes(nbuf, 0, 1))
        ssms.append(ssm_s)
        glas.append(gla_s)
    return (xg.reshape(bd, 1, D_MODEL), jnp.stack(ks), jnp.stack(vs), jnp.stack(convs), jnp.stack(ssms),
            jnp.stack(glas))


def kernel(x_prompt, x_sample, c_prompt, c_sample, cache_k, cache_v, page_table, state_conv, state_ssm, state_gla,
           w_ada, b_ada, w_in, w_out, rel_bias, lambda_qk, subln_g, conv_w, conv_b, dt_bias, a_log, d_skip,
           ssd_norm_g, gla_w_lr, gla_b_lr, gla_norm_g, ffn_w_gate, ffn_w_up, ffn_w_down, router_w,
           moe_w_gate, moe_w_up, moe_w_down, final_norm_g):
    depth = w_in.shape[0]
    bp, bd = c_prompt.shape[0], c_sample.shape[0]
    pk = _pack_params(w_in, rel_bias, subln_g, conv_b, dt_bias, a_log, d_skip, ssd_norm_g, gla_w_lr, gla_b_lr,
                      gla_norm_g, router_w)
    wts = dict(w_out=w_out.astype(BF16), ffn_g=ffn_w_gate.astype(BF16), ffn_u=ffn_w_up.astype(BF16),
               ffn_d=ffn_w_down.astype(BF16), moe_g=moe_w_gate.astype(BF16), moe_u=moe_w_up.astype(BF16),
               moe_d=moe_w_down.astype(BF16), final_g=final_norm_g.reshape(1, D_MODEL))
    n_c = bp + bd
    n_cp = -(-n_c // 8) * 8
    c_all = jnp.concatenate([c_prompt, c_sample, jnp.zeros((n_cp - n_c, D_MODEL), F32)], axis=0)
    mod = ada_modulation(c_all, w_ada, b_ada)
    mod_p = mod[:, :bp].reshape(depth, bp, 1, 6 * D_MODEL)
    mod_d = mod[:, bp:n_c].reshape(depth, 1, bd, 6 * D_MODEL)
    yp, kp, vp, convp, ssmp, glap = _run_prompt(x_prompt, mod_p, pk, wts, lambda_qk, conv_w, rel_bias)
    yd, kd, vd, convd, ssmd, glad = _run_decode(x_sample, mod_d, pk, wts, lambda_qk, conv_w, rel_bias, cache_k,
                                                cache_v, page_table, state_conv, state_ssm, state_gla, d_skip,
                                                ssd_norm_g, gla_norm_g)
    return (yp, yd, kp, vp, convp, ssmp, glap, kd, vd, convd, ssmd, glad)
```
